```python
import math
import jax, jax.numpy as jnp
from jax import lax
import numpy as np


D_MODEL = 1024
BATCH = 8
SEQ = 4096
DEPTH = 1
DEC_BATCH = 32
DEC_SEQ = 64
PAST_LEN = 2048

CHUNK = 64
D_MIX = D_MODEL
D_A = D_MIX // 2
D_B = D_MIX - D_A
SGU_LEN = 128
N_HEADS_A = 4
HEAD_A = D_A // N_HEADS_A
SSM_GROUP = 16
N_GROUPS_B = D_B // SSM_GROUP
SSM_STATE = 64
D_IN = 2 * D_A + D_B
N_EXPERTS = 256
TOP_K = 8
D_EXPERT = 256
D_SHARED = 256
ROUTE_SCALE = 2.5
MOE_BLOCK = 128
EPS = 1e-6

kernel_name = "hybrid_sgu_s5_moe_stream_step"


def rmsnorm(x, g):
    xf = x.astype(jnp.float32)
    inv = lax.rsqrt(jnp.mean(xf * xf, axis=-1, keepdims=True) + EPS)
    return (xf * inv).astype(x.dtype) * g


def swiglu(x, wg, wu, wd):
    return (jax.nn.silu(x @ wg) * (x @ wu)) @ wd


def sgu_mixer(u, v, w_s, b_s, g_v):
    bn, t, _ = u.shape
    ln = min(t, SGU_LEN)
    vh = rmsnorm(v.reshape(bn, t, N_HEADS_A, HEAD_A), g_v)
    pos = jnp.arange(ln)
    mask = (pos[:, None] // CHUNK) >= (pos[None, :] // CHUNK)
    w = jnp.where(mask[None], w_s[:, :ln, :ln], 0.0)
    vb = vh.reshape(bn, t // ln, ln, N_HEADS_A, HEAD_A)
    s = jnp.einsum('hij,bcjhd->bcihd', w, vb) + b_s[:, :ln].T[None, None, :, :, None]
    return u * s.reshape(bn, t, D_A), vh.reshape(bn, t, D_A)


def _cplx_combine(e1, e2):
    a1r, a1i, b1r, b1i = e1
    a2r, a2i, b2r, b2i = e2
    return (a2r * a1r - a2i * a1i, a2r * a1i + a2i * a1r,
            a2r * b1r - a2i * b1i + b2r, a2r * b1i + a2i * b1r + b2i)


def s5_mixer(u, h0_re, h0_im, lam_re, lam_im, log_dt, b_re, b_im, c_re, c_im, d_skip, w_glu, b_glu):
    f32 = jnp.float32
    bn, t, _ = u.shape
    uf = u.astype(f32).reshape(bn, t, N_GROUPS_B, SSM_GROUP)
    dt = jnp.exp(log_dt.astype(f32))[:, None]
    lr, li = lam_re.astype(f32), lam_im.astype(f32)
    mag = jnp.exp(lr * dt)
    ar, ai = mag * jnp.cos(li * dt), mag * jnp.sin(li * dt)
    den = lr * lr + li * li
    nr, ni = ar - 1.0, ai
    kr, ki = (nr * lr + ni * li) / den, (ni * lr - nr * li) / den
    br, bi = b_re.astype(f32), b_im.astype(f32)
    bbr = kr[..., None] * br - ki[..., None] * bi
    bbi = kr[..., None] * bi + ki[..., None] * br
    cr, ci = c_re.astype(f32), c_im.astype(f32)
    blk = CHUNK if t % CHUNK == 0 else t
    nblk = t // blk
    steps = jnp.arange(1, blk + 1, dtype=f32)[:, None, None]
    pmag = jnp.exp(lr[None] * dt[None] * steps)
    pow_r = pmag * jnp.cos(li[None] * dt[None] * steps)
    pow_i = pmag * jnp.sin(li[None] * dt[None] * steps)
    ub = uf.reshape(bn, nblk, blk, N_GROUPS_B, SSM_GROUP).transpose(1, 0, 2, 3, 4)

    def block_step(carry, ublk):
        hr, hi = carry
        bur = jnp.einsum('btgh,gph->btgp', ublk, bbr)
        bui = jnp.einsum('btgh,gph->btgp', ublk, bbi)
        a_r = jnp.broadcast_to(ar, bur.shape)
        a_i = jnp.broadcast_to(ai, bur.shape)
        _, _, loc_r, loc_i = lax.associative_scan(_cplx_combine, (a_r, a_i, bur, bui), axis=1)
        sr = pow_r[None] * hr[:, None] - pow_i[None] * hi[:, None] + loc_r
        si = pow_r[None] * hi[:, None] + pow_i[None] * hr[:, None] + loc_i
        y = jnp.einsum('btgp,ghp->btgh', sr, cr) - jnp.einsum('btgp,ghp->btgh', si, ci)
        return (sr[:, -1], si[:, -1]), y

    (hr, hi), ys = lax.scan(block_step, (h0_re.astype(f32), h0_im.astype(f32)), ub)
    y = ys.transpose(1, 0, 2, 3, 4).reshape(bn, t, N_GROUPS_B, SSM_GROUP) + d_skip.astype(f32) * uf
    g = jax.nn.gelu(y.reshape(bn, t, D_B)).astype(u.dtype)
    out = g * jax.nn.sigmoid(g @ w_glu + b_glu)
    return out, hr, hi


def routed_experts(xt, idx, wts, w_gate, w_up, w_down):
    n = xt.shape[0]
    a = n * TOP_K
    e_flat = idx.reshape(-1)
    tok_flat = jnp.arange(a, dtype=jnp.int32) // TOP_K
    w_flat = wts.reshape(-1)
    order = jnp.argsort(e_flat)
    e_sorted = e_flat[order]
    counts = jnp.bincount(e_flat, length=N_EXPERTS)
    padded = (counts + MOE_BLOCK - 1) // MOE_BLOCK * MOE_BLOCK
    pad_end = jnp.cumsum(padded)
    pad_start = pad_end - padded
    start = jnp.cumsum(counts) - counts
    dest = pad_start[e_sorted] + (jnp.arange(a, dtype=jnp.int32) - start[e_sorted])
    n_blocks = -(-a // MOE_BLOCK) + N_EXPERTS
    r = n_blocks * MOE_BLOCK
    row_tok = jnp.zeros((r,), jnp.int32).at[dest].set(tok_flat[order])
    row_w = jnp.zeros((r,), jnp.float32).at[dest].set(w_flat[order])
    blk_start = jnp.arange(n_blocks, dtype=pad_end.dtype) * MOE_BLOCK
    blk_exp = jnp.minimum(jnp.searchsorted(pad_end, blk_start, side='right'), N_EXPERTS - 1)

    def step(acc, inp):
        tok, w, e = inp
        xb = xt[tok]
        yb = swiglu(xb, w_gate[e], w_up[e], w_down[e]) * w[:, None].astype(xt.dtype)
        return acc.at[tok].add(yb), None

    acc, _ = lax.scan(step, jnp.zeros_like(xt),
                      (row_tok.reshape(n_blocks, MOE_BLOCK), row_w.reshape(n_blocks, MOE_BLOCK), blk_exp))
    return acc


def moe_ffn(h, w_router, router_bias, w_gate, w_up, w_down, ws_gate, ws_up, ws_down):
    bn, t, d = h.shape
    xt = h.reshape(-1, d)
    scores = jax.nn.sigmoid((xt @ w_router).astype(jnp.float32))
    _, idx = lax.top_k(scores + router_bias.astype(jnp.float32), TOP_K)
    sel = jnp.take_along_axis(scores, idx, axis=-1)
    wts = ROUTE_SCALE * sel / jnp.sum(sel, axis=-1, keepdims=True)
    y = routed_experts(xt, idx, wts, w_gate, w_up, w_down) + swiglu(xt, ws_gate, ws_up, ws_down)
    return y.reshape(bn, t, d)


def trunk_layer(x, c, h0_re, h0_im, norm1_g, norm2_g, w_ada, b_ada, w_in, w_s, b_s, g_v,
                lam_re, lam_im, log_dt, b_re, b_im, c_re, c_im, d_skip, w_glu, b_glu,
                out_g_a, out_g_b, w_out, w_router, router_bias, w_gate, w_up, w_down,
                ws_gate, ws_up, ws_down):
    mod = (jax.nn.silu(c) @ w_ada + b_ada)[:, None, :]
    sh1, sc1, g1, sh2, sc2, g2 = jnp.split(mod, 6, axis=-1)
    h = rmsnorm(x, norm1_g) * (1.0 + sc1) + sh1
    z = h @ w_in
    u_a, v_a, u_b = jnp.split(z, [D_A, 2 * D_A], axis=-1)
    y_a, v_rows = sgu_mixer(u_a, v_a, w_s, b_s, g_v)
    y_b, hr, hi = s5_mixer(u_b, h0_re, h0_im, lam_re, lam_im, log_dt, b_re, b_im,
                           c_re, c_im, d_skip, w_glu, b_glu)
    mix = jnp.concatenate([rmsnorm(y_a, out_g_a), rmsnorm(y_b, out_g_b)], axis=-1) @ w_out
    x = x + g1 * mix
    h2 = rmsnorm(x, norm2_g) * (1.0 + sc2) + sh2
    x = x + g2 * moe_ffn(h2, w_router, router_bias, w_gate, w_up, w_down, ws_gate, ws_up, ws_down)
    return x, v_rows, hr, hi


def setup_inputs(seed: int = 0) -> dict:
    key = jax.random.key(seed)
    ks = jax.random.split(key, 40)
    f32 = jnp.float32

    def nrm(k, shape, scale):
        return jax.random.normal(k, shape, f32) * scale

    def gain(k, shape):
        return 1.0 + 0.02 * jax.random.normal(k, shape, f32)

    L = DEPTH
    G, P, H = N_GROUPS_B, SSM_STATE, SSM_GROUP
    n_idx = jnp.arange(SSM_STATE, dtype=f32)
    return {
        "x_prompt": nrm(ks[0], (BATCH, SEQ, D_MODEL), 1.0),
        "x_sample": nrm(ks[1], (DEC_BATCH, DEC_SEQ, D_MODEL), 1.0),
        "c_prompt": nrm(ks[2], (BATCH, D_MODEL), 1.0),
        "c_sample": nrm(ks[3], (DEC_BATCH, D_MODEL), 1.0),
        "state_ssm_re": nrm(ks[4], (L, DEC_BATCH, G, P), 0.5),
        "state_ssm_im": nrm(ks[5], (L, DEC_BATCH, G, P), 0.5),
        "norm1_g": gain(ks[6], (L, D_MODEL)),
        "norm2_g": gain(ks[7], (L, D_MODEL)),
        "w_ada": nrm(ks[8], (L, D_MODEL, 6 * D_MODEL), 0.5 * D_MODEL ** -0.5),
        "b_ada": nrm(ks[9], (L, 6 * D_MODEL), 0.02),
        "w_in": nrm(ks[10], (L, D_MODEL, D_IN), D_MODEL ** -0.5),
        "w_s": nrm(ks[11], (L, N_HEADS_A, SGU_LEN, SGU_LEN), 0.5 * SGU_LEN ** -0.5),
        "b_s": gain(ks[12], (L, N_HEADS_A, SGU_LEN)),
        "g_v": gain(ks[13], (L, N_HEADS_A, HEAD_A)),
        "lam_re": -0.5 + nrm(ks[14], (L, G, P), 0.01),
        "lam_im": math.pi * n_idx + nrm(ks[15], (L, G, P), 0.01),
        "log_dt": jax.random.uniform(ks[16], (L, G), f32, math.log(1e-3), math.log(1e-1)),
        "b_re": nrm(ks[17], (L, G, P, H), (2 * H) ** -0.5),
        "b_im": nrm(ks[18], (L, G, P, H), (2 * H) ** -0.5),
        "c_re": nrm(ks[19], (L, G, H, P), P ** -0.5),
        "c_im": nrm(ks[20], (L, G, H, P), P ** -0.5),
        "d_skip": nrm(ks[21], (L, G, H), 1.0),
        "w_glu": nrm(ks[22], (L, D_B, D_B), D_B ** -0.5),
        "b_glu": nrm(ks[23], (L, D_B), 0.02),
        "out_g_a": gain(ks[24], (L, D_A)),
        "out_g_b": gain(ks[25], (L, D_B)),
        "w_out": nrm(ks[26], (L, D_MIX, D_MODEL), D_MIX ** -0.5),
        "w_router": nrm(ks[27], (L, D_MODEL, N_EXPERTS), D_MODEL ** -0.5),
        "router_bias": nrm(ks[28], (L, N_EXPERTS), 0.01),
        "w_gate": nrm(ks[29], (L, N_EXPERTS, D_MODEL, D_EXPERT), D_MODEL ** -0.5),
        "w_up": nrm(ks[30], (L, N_EXPERTS, D_MODEL, D_EXPERT), D_MODEL ** -0.5),
        "w_down": nrm(ks[31], (L, N_EXPERTS, D_EXPERT, D_MODEL), D_EXPERT ** -0.5),
        "ws_gate": nrm(ks[32], (L, D_MODEL, D_SHARED), D_MODEL ** -0.5),
        "ws_up": nrm(ks[33], (L, D_MODEL, D_SHARED), D_MODEL ** -0.5),
        "ws_down": nrm(ks[34], (L, D_SHARED, D_MODEL), D_SHARED ** -0.5),
        "final_g": gain(ks[35], (D_MODEL,)),
    }


def reference(x_prompt, x_sample, c_prompt, c_sample, state_ssm_re, state_ssm_im,
              norm1_g, norm2_g, w_ada, b_ada, w_in, w_s, b_s, g_v, lam_re, lam_im, log_dt,
              b_re, b_im, c_re, c_im, d_skip, w_glu, b_glu, out_g_a, out_g_b, w_out,
              w_router, router_bias, w_gate, w_up, w_down, ws_gate, ws_up, ws_down, final_g):
    layer_w = (norm1_g, norm2_g, w_ada, b_ada, w_in, w_s, b_s, g_v, lam_re, lam_im, log_dt,
               b_re, b_im, c_re, c_im, d_skip, w_glu, b_glu, out_g_a, out_g_b, w_out,
               w_router, router_bias, w_gate, w_up, w_down, ws_gate, ws_up, ws_down)
    x_p, x_s = x_prompt, x_sample
    re_p, im_p, re_s, im_s, v_s = [], [], [], [], []
    for l in range(DEPTH):
        lw = tuple(w[l] for w in layer_w)
        h0 = jnp.zeros((x_p.shape[0], N_GROUPS_B, SSM_STATE), jnp.float32)
        x_p, _, hr, hi = trunk_layer(x_p, c_prompt, h0, h0, *lw)
        re_p.append(hr)
        im_p.append(hi)
        x_s, v_rows, hr, hi = trunk_layer(x_s, c_sample, state_ssm_re[l], state_ssm_im[l], *lw)
        re_s.append(hr)
        im_s.append(hi)
        v_s.append(v_rows)
    y_prompt = rmsnorm(x_p, final_g)
    y_sample = rmsnorm(x_s, final_g)
    ssm_re_prompt = jnp.stack(re_p)
    ssm_im_prompt = jnp.stack(im_p)
    ssm_re_sample = jnp.stack(re_s)
    ssm_im_sample = jnp.stack(im_s)
    sgu_v_sample = jnp.stack(v_s)
    return (y_prompt, y_sample, ssm_re_prompt, ssm_im_prompt, ssm_re_sample, ssm_im_sample, sgu_v_sample)
```

```python
import functools
import math

import jax
import jax.numpy as jnp
from jax import lax
from jax.experimental import pallas as pl
from jax.experimental.pallas import tpu as pltpu

F32 = jnp.float32
BF16 = jnp.bfloat16
I32 = jnp.int32

D_MODEL = 1024
D_A = 512
D_B = 512
N_HEADS_A = 4
HEAD_A = 128
SSM_GROUP = 16
N_GROUPS_B = 32
SSM_STATE = 64
D_IN = 2 * D_A + D_B
N_EXPERTS = 256
TOP_K = 8
D_EXPERT = 256
ROUTE_SCALE = 2.5
CHUNK = 64
SGU_LEN = 128
EPS = 1e-6

STATE_CHUNKS = 4
GROUPS_PER_CHUNK = N_GROUPS_B // STATE_CHUNKS
HALF = GROUPS_PER_CHUNK * SSM_STATE
STATE_COLS = STATE_CHUNKS * 2 * HALF
SCAN_W = 256
SUB = 8
LANES = 128

TAB_PW = 0
TAB_MD = 8
TAB_P8 = 11
N_TABS = 12

MIX_ROWS = 256
MOE_BLK = 128
TOK_TILE = 256
SLAB = 128
VMEM_LIMIT = 56 * 1024 * 1024


def _dot(a, b):
    return jnp.dot(a, b, preferred_element_type=F32)


def _rms(x):
    return lax.rsqrt(jnp.mean(x * x, axis=-1, keepdims=True) + EPS)


def _adaln_body(c_ref, w_ref, b_ref, o_ref):
    c = c_ref[...]
    o_ref[...] = _dot(jax.nn.silu(c).astype(BF16), w_ref[...].astype(BF16)) + b_ref[...]


def _adaln(c, w_ada, b_ada):
    n = c.shape[0]
    cols = w_ada.shape[1]
    blk = 1536
    return pl.pallas_call(
        _adaln_body,
        out_shape=jax.ShapeDtypeStruct((n, cols), F32),
        grid=(cols // blk,),
        in_specs=[pl.BlockSpec((n, D_MODEL), lambda j: (0, 0)),
                  pl.BlockSpec((D_MODEL, blk), lambda j: (0, j)),
                  pl.BlockSpec((1, blk), lambda j: (0, j))],
        out_specs=pl.BlockSpec((n, blk), lambda j: (0, j)),
        compiler_params=pltpu.CompilerParams(dimension_semantics=("arbitrary",)),
        name="adaln",
    )(c, w_ada, b_ada.reshape(1, cols))


def _lay(re, im):
    lead = re.shape[:-2]
    re = re.reshape(lead + (STATE_CHUNKS, HALF))
    im = im.reshape(lead + (STATE_CHUNKS, HALF))
    return jnp.concatenate([re, im], axis=-1).reshape(lead + (STATE_COLS,))


def _unlay(v):
    lead = v.shape[:-1]
    v = v.reshape(lead + (STATE_CHUNKS, 2, HALF))
    re = v[..., 0, :].reshape(lead + (N_GROUPS_B, SSM_STATE))
    im = v[..., 1, :].reshape(lead + (N_GROUPS_B, SSM_STATE))
    return re, im


def _s5_tables(lam_re, lam_im, log_dt, b_re, b_im, c_re, c_im):
    dt = jnp.exp(log_dt.astype(F32))[:, None]
    lr, li = lam_re.astype(F32), lam_im.astype(F32)

    def apow(k):
        mag = jnp.exp(lr * dt * k)
        return mag * jnp.cos(li * dt * k), mag * jnp.sin(li * dt * k)

    ar, ai = apow(1.0)
    den = lr * lr + li * li
    nr, ni = ar - 1.0, ai
    kr, ki = (nr * lr + ni * li) / den, (ni * lr - nr * li) / den
    br, bi = b_re.astype(F32), b_im.astype(F32)
    bbr = kr[..., None] * br - ki[..., None] * bi
    bbi = kr[..., None] * bi + ki[..., None] * br
    eye = jnp.eye(GROUPS_PER_CHUNK, dtype=F32)

    def bproj(bb):
        bb = bb.reshape(STATE_CHUNKS, GROUPS_PER_CHUNK, SSM_STATE, SSM_GROUP)
        w = jnp.einsum("mgph,gk->mghkp", bb, eye)
        return w.reshape(STATE_CHUNKS, GROUPS_PER_CHUNK * SSM_GROUP, HALF)

    wb = jnp.concatenate([bproj(bbr), bproj(bbi)], axis=-1).astype(BF16)

    def cproj(cc):
        cc = cc.reshape(STATE_CHUNKS, GROUPS_PER_CHUNK, SSM_GROUP, SSM_STATE)
        w = jnp.einsum("mghp,gk->mgpkh", cc, eye)
        return w.reshape(STATE_CHUNKS, HALF, GROUPS_PER_CHUNK * SSM_GROUP)

    wc = jnp.concatenate([cproj(c_re.astype(F32)), cproj(-c_im.astype(F32))], axis=1).astype(BF16)

    rows = jnp.arange(SUB, dtype=F32)
    tabs = []
    for i in range(SUB):
        pr, pi = apow(float(i + 1))
        tabs.append(jnp.broadcast_to(_lay(pr, pi)[None], (SUB, STATE_COLS)))
    for d in (1, 2, 4):
        pr, pi = apow(float(SUB * d))
        keep = (rows >= d).astype(F32)[:, None]
        tabs.append(_lay(pr, pi)[None] * keep)
    pr, pi = apow(SUB * rows[:, None, None])
    tabs.append(_lay(pr, pi))
    return wb, wc, jnp.stack(tabs)


def _cmul(ar, ai, br, bi):
    return ar * br - ai * bi, ar * bi + ai * br


def _s5_scan_block(bu_ref, row0, h_ref, tab_ref):
    row_id = lax.broadcasted_iota(I32, (SUB, SCAN_W), 0)
    tiles = SCAN_W // LANES
    for m in range(STATE_CHUNKS):
        for hf in range(HALF // SCAN_W):
            c_re0 = m * 2 * HALF + hf * SCAN_W
            c_im0 = c_re0 + HALF
            cre = pl.ds(c_re0, SCAN_W)
            cim = pl.ds(c_im0, SCAN_W)

            def tab(slot):
                return tab_ref[slot, :, cre], tab_ref[slot, :, cim]

            def load(i, c0):
                rows = pl.ds(row0 + i, SUB, stride=SUB)
                return jnp.concatenate([bu_ref[c0 // LANES + j, rows, :] for j in range(tiles)], axis=1)

            def store(i, c0, val):
                rows = pl.ds(row0 + i, SUB, stride=SUB)
                for j in range(tiles):
                    bu_ref[c0 // LANES + j, rows, :] = val[:, j * LANES:(j + 1) * LANES]

            a_re, a_im = tab(TAB_PW)
            s_re = load(0, c_re0)
            s_im = load(0, c_im0)
            loc = [(s_re, s_im)]
            for i in range(1, SUB):
                p_re, p_im = _cmul(a_re, a_im, s_re, s_im)
                s_re = p_re + load(i, c_re0)
                s_im = p_im + load(i, c_im0)
                loc.append((s_re, s_im))
            e_re, e_im = s_re, s_im
            for n, d in enumerate((1, 2, 4)):
                m_re, m_im = tab(TAB_MD + n)
                q_re, q_im = _cmul(m_re, m_im, pltpu.roll(e_re, d, 0), pltpu.roll(e_im, d, 0))
                e_re, e_im = e_re + q_re, e_im + q_im
            p8_re, p8_im = tab(TAB_P8)
            c_re, c_im = _cmul(p8_re, p8_im, h_ref[:, cre], h_ref[:, cim])
            c_re = c_re + jnp.where(row_id >= 1, pltpu.roll(e_re, 1, 0), 0.0)
            c_im = c_im + jnp.where(row_id >= 1, pltpu.roll(e_im, 1, 0), 0.0)
            for i in range(SUB):
                w_re, w_im = tab(TAB_PW + i)
                q_re, q_im = _cmul(w_re, w_im, c_re, c_im)
                f_re, f_im = loc[i][0] + q_re, loc[i][1] + q_im
                store(i, c_re0, f_re)
                store(i, c_im0, f_im)
            h_ref[:, cre] = jnp.broadcast_to(f_re[SUB - 1:SUB, :], (SUB, SCAN_W))
            h_ref[:, cim] = jnp.broadcast_to(f_im[SUB - 1:SUB, :], (SUB, SCAN_W))


def _mix_body(seqs, ln, chain, emit_v, n_alias,
              x_ref, mod_ref, h0_ref, cnt0_ref, n1g_ref, win_ref, ws_ref, bsf_ref, gv_ref,
              wb_ref, wc_ref, tab_ref, dsk_ref, wglu_ref, bglu_ref, oga_ref, ogb_ref, wout_ref,
              n2g_ref, wr_ref, rb_ref, wsg_ref, wsu_ref, wsd_ref, ltri_ref, *rest):
    outs = rest[n_alias:]
    if emit_v:
        xs1_ref, h2_ref, slab_ref, hfin_ref, cnt_ref, v_ref = outs[:6]
        bu_ref, h_scr, cnt_scr = outs[6:]
    else:
        xs1_ref, h2_ref, slab_ref, hfin_ref, cnt_ref = outs[:5]
        v_ref = None
        bu_ref, h_scr, cnt_scr = outs[5:]

    t_rows = MIX_ROWS
    rows_per_seq = t_rows // seqs
    first = jnp.logical_and(pl.program_id(0) == 0, pl.program_id(1) == 0)

    @pl.when(first)
    def _():
        cnt_scr[...] = cnt0_ref[...]

    x = x_ref[...].reshape(t_rows, D_MODEL)

    def modrow(j):
        parts = [jnp.broadcast_to(mod_ref[s, j:j + 1, :], (rows_per_seq, D_MODEL)) for s in range(seqs)]
        return parts[0] if seqs == 1 else jnp.concatenate(parts, axis=0)

    h = (x * _rms(x)) * n1g_ref[...] * (1.0 + modrow(1)) + modrow(0)
    z = _dot(h.astype(BF16), win_ref[...])
    u_a = z[:, :D_A]
    v_a = z[:, D_A:2 * D_A]
    u_b = z[:, 2 * D_A:]

    vh_parts = []
    for hh in range(N_HEADS_A):
        cols = slice(hh * HEAD_A, (hh + 1) * HEAD_A)
        vv = v_a[:, cols]
        vh_parts.append((vv * _rms(vv)) * gv_ref[:, cols])
    if emit_v:
        v_ref[...] = jnp.concatenate(vh_parts, axis=1).reshape(seqs, rows_per_seq, D_A)
    n_sgu = t_rows // ln
    s_parts = []
    for hh in range(N_HEADS_A):
        vb = vh_parts[hh].astype(BF16)
        w_h = ws_ref[hh]
        s_parts.append(jnp.concatenate(
            [_dot(w_h, vb[c * ln:(c + 1) * ln, :]) for c in range(n_sgu)], axis=0))
    bsf = bsf_ref[...]
    s_mix = jnp.concatenate(s_parts, axis=1) + jnp.concatenate([bsf] * n_sgu, axis=0)
    y_a = u_a * s_mix

    ub16 = u_b.astype(BF16)
    gc = GROUPS_PER_CHUNK * SSM_GROUP
    tiles_per_chunk = 2 * HALF // LANES
    for m in range(STATE_CHUNKS):
        bu = _dot(ub16[:, m * gc:(m + 1) * gc], wb_ref[m])
        for j in range(tiles_per_chunk):
            bu_ref[m * tiles_per_chunk + j] = bu[:, j * LANES:(j + 1) * LANES]
    n_blk = t_rows // CHUNK
    if chain:
        @pl.when(pl.program_id(1) == 0)
        def _():
            h_scr[...] = h0_ref[0]
    for blk in range(n_blk):
        if not chain:
            h_scr[...] = h0_ref[blk]
        _s5_scan_block(bu_ref, blk * CHUNK, h_scr, tab_ref)
        if not chain:
            hfin_ref[blk] = h_scr[...]
    if chain:
        hfin_ref[0] = h_scr[...]
    y_parts = []
    for m in range(STATE_CHUNKS):
        st = jnp.concatenate([bu_ref[m * tiles_per_chunk + j] for j in range(tiles_per_chunk)], axis=1)
        y_parts.append(_dot(st.astype(BF16), wc_ref[m]))
    y_s = jnp.concatenate(y_parts, axis=1) + dsk_ref[...] * u_b
    g_b = jax.nn.gelu(y_s)
    y_b = g_b * jax.nn.sigmoid(_dot(g_b.astype(BF16), wglu_ref[...]) + bglu_ref[...])

    na = (y_a * _rms(y_a)) * oga_ref[...]
    nb = (y_b * _rms(y_b)) * ogb_ref[...]
    mix = _dot(jnp.concatenate([na, nb], axis=1).astype(BF16), wout_ref[...])
    x1 = x + modrow(2) * mix

    h2 = (x1 * _rms(x1)) * n2g_ref[...] * (1.0 + modrow(4)) + modrow(3)
    h2_ref[...] = h2
    h2b = h2.astype(BF16)
    act = jax.nn.silu(_dot(h2b, wsg_ref[...])) * _dot(h2b, wsu_ref[...])
    shared = _dot(act.astype(BF16), wsd_ref[...])
    xs1_ref[...] = x1 + modrow(5) * shared

    scores = jax.nn.sigmoid(_dot(h2b, wr_ref[...]))
    lane = lax.broadcasted_iota(I32, (t_rows, N_EXPERTS), 1).astype(F32)
    work = scores + rb_ref[...]
    onehot = jnp.zeros((t_rows, N_EXPERTS), F32)
    idxs, sels = [], []
    for _ in range(TOP_K):
        top = jnp.max(work, axis=-1, keepdims=True)
        idx = jnp.min(jnp.where(work == top, lane, float(N_EXPERTS)), axis=-1, keepdims=True)
        pick = lane == idx
        sels.append(jnp.sum(jnp.where(pick, scores, 0.0), axis=-1, keepdims=True))
        idxs.append(idx)
        work = jnp.where(pick, -jnp.inf, work)
        onehot = jnp.where(pick, 1.0, onehot)
    total = sels[0]
    for k in range(1, TOP_K):
        total = total + sels[k]
    ranktab = _dot(ltri_ref[...], onehot.astype(BF16)) + cnt_scr[0:1, :]
    slab_lane = lax.broadcasted_iota(I32, (t_rows, SLAB), 1)
    slab = jnp.zeros((t_rows, SLAB), I32)
    for k in range(TOP_K):
        rank = jnp.sum(jnp.where(lane == idxs[k], ranktab, 0.0), axis=-1, keepdims=True)
        wt = ROUTE_SCALE * sels[k] / total
        wbits = lax.bitcast_convert_type(jnp.broadcast_to(wt, (t_rows, SLAB)), I32)
        slab = jnp.where(slab_lane == k, idxs[k].astype(I32), slab)
        slab = jnp.where(slab_lane == TOP_K + k, rank.astype(I32), slab)
        slab = jnp.where(slab_lane == 2 * TOP_K + k, wbits, slab)
    slab_ref[...] = slab
    cnt_scr[...] = cnt_scr[...] + jnp.sum(onehot, axis=0, keepdims=True)
    cnt_ref[...] = cnt_scr[...]


def _mix(x, mod, h0, cnt0, weights, bufs, *, n_total, row_offset, seqs, ln, chain, emit_v):
    bsz, seq_len, _ = x.shape
    rows_per_seq = MIX_ROWS // seqs
    assert seq_len % rows_per_seq == 0 and bsz % seqs == 0
    nb_outer, nt = bsz // seqs, seq_len // rows_per_seq
    n_h = 1 if chain else MIX_ROWS // CHUNK
    off = row_offset // MIX_ROWS

    def tok_map(b, t):
        return (off + b * nt + t, 0)

    def const(shape):
        nd = len(shape)
        return pl.BlockSpec(shape, lambda b, t: (0,) * nd)

    in_specs = [
        pl.BlockSpec((seqs, rows_per_seq, D_MODEL), lambda b, t: (b, t, 0)),
        pl.BlockSpec((seqs, 6, D_MODEL), lambda b, t: (b, 0, 0)),
        pl.BlockSpec((n_h, SUB, STATE_COLS), lambda b, t: (b, 0, 0)),
        const((SUB, N_EXPERTS)),
    ] + [const(w.shape) for w in weights] + [pl.BlockSpec(memory_space=pl.ANY)] * len(bufs)
    out_shape = [
        jax.ShapeDtypeStruct((n_total, D_MODEL), F32),
        jax.ShapeDtypeStruct((n_total, D_MODEL), F32),
        jax.ShapeDtypeStruct((n_total, SLAB), I32),
        jax.ShapeDtypeStruct((nb_outer * n_h, SUB, STATE_COLS), F32),
        jax.ShapeDtypeStruct((SUB, N_EXPERTS), F32),
    ]
    out_specs = [
        pl.BlockSpec((MIX_ROWS, D_MODEL), tok_map),
        pl.BlockSpec((MIX_ROWS, D_MODEL), tok_map),
        pl.BlockSpec((MIX_ROWS, SLAB), tok_map),
        pl.BlockSpec((n_h, SUB, STATE_COLS), lambda b, t: (b, 0, 0)),
        const((SUB, N_EXPERTS)),
    ]
    if emit_v:
        out_shape.append(jax.ShapeDtypeStruct((bsz, seq_len, D_A), F32))
        out_specs.append(pl.BlockSpec((seqs, rows_per_seq, D_A), lambda b, t: (b, t, 0)))
    n_in = 4 + len(weights)
    return pl.pallas_call(
        functools.partial(_mix_body, seqs, ln, chain, emit_v, len(bufs)),
        out_shape=out_shape,
        grid=(nb_outer, nt),
        in_specs=in_specs,
        out_specs=out_specs,
        scratch_shapes=[pltpu.VMEM((STATE_COLS // LANES, MIX_ROWS, LANES), F32),
                        pltpu.VMEM((SUB, STATE_COLS), F32),
                        pltpu.VMEM((SUB, N_EXPERTS), F32)],
        input_output_aliases={n_in + j: j for j in range(len(bufs))},
        compiler_params=pltpu.CompilerParams(dimension_semantics=("arbitrary", "arbitrary"),
                                             vmem_limit_bytes=VMEM_LIMIT),
        name="mix_chain" if chain else "mix_step",
    )(x, mod, h0, cnt0, *weights, *bufs)


def _row_copy(src_hbm, src_row, dst_hbm, dst_row, sem):
    return pltpu.make_async_copy(src_hbm.at[pl.ds(src_row, 1)], dst_hbm.at[pl.ds(dst_row, 1)], sem)


def _dispatch_body(idx_ref, rank_ref, pstart_ref, pend_ref, h2_hbm, zero_hbm, xs_hbm, sem):
    i = pl.program_id(0)
    n_blocks = xs_hbm.shape[0] // MOE_BLK

    @pl.when(i == 0)
    def _():
        def clear(row):
            cp = pltpu.make_async_copy(zero_hbm, xs_hbm.at[pl.ds(pl.multiple_of(row, MOE_BLK), MOE_BLK)], sem)
            cp.start()
            cp.wait()

        def fill(e, carry):
            @pl.when(pend_ref[e] > pstart_ref[e])
            def _():
                clear(pend_ref[e] - MOE_BLK)
            return carry
        lax.fori_loop(0, N_EXPERTS, fill, 0)

        def tail(b, carry):
            clear(b * MOE_BLK)
            return carry
        lax.fori_loop(pend_ref[N_EXPERTS - 1] // MOE_BLK, n_blocks, tail, 0)

    def issue(t, carry):
        for k in range(TOP_K):
            e = idx_ref[0, 0, t * TOP_K + k]
            dst = pstart_ref[e] + rank_ref[0, 0, t * TOP_K + k]
            _row_copy(h2_hbm, i * TOK_TILE + t, xs_hbm, dst, sem).start()
        return carry
    lax.fori_loop(0, TOK_TILE, issue, 0)
    for _ in range(TOP_K):
        pltpu.make_async_copy(h2_hbm.at[pl.ds(0, TOK_TILE)], xs_hbm.at[pl.ds(0, TOK_TILE)], sem).wait()


def _dispatch(idx, rank, pstart, pend, h2, n_rows):
    n_tok = h2.shape[0]
    n_tiles = n_tok // TOK_TILE
    smem_tile = pl.BlockSpec((1, 1, TOK_TILE * TOP_K), lambda i: (i, 0, 0), memory_space=pltpu.SMEM)
    smem_all = pl.BlockSpec(memory_space=pltpu.SMEM)
    return pl.pallas_call(
        _dispatch_body,
        out_shape=jax.ShapeDtypeStruct((n_rows, D_MODEL), F32),
        grid=(n_tiles,),
        in_specs=[smem_tile, smem_tile, smem_all, smem_all,
                  pl.BlockSpec(memory_space=pl.ANY), pl.BlockSpec(memory_space=pl.ANY)],
        out_specs=pl.BlockSpec(memory_space=pl.ANY),
        scratch_shapes=[pltpu.SemaphoreType.DMA],
        compiler_params=pltpu.CompilerParams(dimension_semantics=("arbitrary",)),
        name="moe_dispatch",
    )(idx.reshape(n_tiles, 1, TOK_TILE * TOP_K), rank.reshape(n_tiles, 1, TOK_TILE * TOP_K),
      pstart, pend, h2, jnp.zeros((MOE_BLK, D_MODEL), F32))


def _experts_body(bexp_ref, nused_ref, xs_ref, wg_ref, wu_ref, wd_ref, ys_ref):
    used = pl.program_id(0) < nused_ref[0]

    @pl.when(used)
    def _():
        xb = xs_ref[...].astype(BF16)
        gate = _dot(xb, wg_ref[0].astype(BF16))
        up = _dot(xb, wu_ref[0].astype(BF16))
        act = (jax.nn.silu(gate) * up).astype(BF16)
        ys_ref[...] = _dot(act, wd_ref[0].astype(BF16))

    @pl.when(jnp.logical_not(used))
    def _():
        ys_ref[...] = jnp.zeros_like(ys_ref)


def _experts(blk_exp, n_used, xs, w_gate, w_up, w_down):
    n_blocks = xs.shape[0] // MOE_BLK

    def row_map(b, bexp, nused):
        return (jnp.minimum(b, nused[0] - 1), 0)

    def out_map(b, bexp, nused):
        return (b, 0)

    def w_map(b, bexp, nused):
        return (bexp[jnp.minimum(b, nused[0] - 1)], 0, 0)

    return pl.pallas_call(
        _experts_body,
        out_shape=jax.ShapeDtypeStruct(xs.shape, F32),
        grid_spec=pltpu.PrefetchScalarGridSpec(
            num_scalar_prefetch=2,
            grid=(n_blocks,),
            in_specs=[pl.BlockSpec((MOE_BLK, D_MODEL), row_map),
                      pl.BlockSpec((1, D_MODEL, D_EXPERT), w_map),
                      pl.BlockSpec((1, D_MODEL, D_EXPERT), w_map),
                      pl.BlockSpec((1, D_EXPERT, D_MODEL), w_map)],
            out_specs=pl.BlockSpec((MOE_BLK, D_MODEL), out_map)),
        compiler_params=pltpu.CompilerParams(dimension_semantics=("arbitrary",),
                                             vmem_limit_bytes=VMEM_LIMIT),
        name="moe_experts",
    )(blk_exp, n_used, xs, w_gate, w_up, w_down)


def _combine_body(idx_ref, rank_ref, pstart_ref, xs1_ref, g2_ref, wts_ref, fg_ref, ys_hbm,
                  y_ref, buf_ref, sem):
    def issue(t, carry):
        for k in range(TOP_K):
            e = idx_ref[0, 0, t * TOP_K + k]
            src = pstart_ref[e] + rank_ref[0, 0, t * TOP_K + k]
            pltpu.make_async_copy(ys_hbm.at[pl.ds(src, 1)], buf_ref.at[k, pl.ds(t, 1)], sem).start()
        return carry
    lax.fori_loop(0, TOK_TILE, issue, 0)
    for k in range(TOP_K):
        pltpu.make_async_copy(ys_hbm.at[pl.ds(0, TOK_TILE)], buf_ref.at[k], sem).wait()
    wts = wts_ref[...]
    acc = wts[:, 0:1] * buf_ref[0]
    for k in range(1, TOP_K):
        acc = acc + wts[:, k:k + 1] * buf_ref[k]
    seg = TOK_TILE // g2_ref.shape[0]
    g2 = jnp.concatenate([jnp.broadcast_to(g2_ref[s], (seg, D_MODEL)) for s in range(g2_ref.shape[0])],
                         axis=0)
    x = xs1_ref[...] + g2 * acc
    y_ref[...] = (x * _rms(x)) * fg_ref[...]


def _combine(idx, rank, pstart, xs1, g2_blocks, wts, final_g, ys):
    n_tok = xs1.shape[0]
    n_tiles = n_tok // TOK_TILE
    segs = TOK_TILE // CHUNK
    smem_tile = pl.BlockSpec((1, 1, TOK_TILE * TOP_K), lambda i: (i, 0, 0), memory_space=pltpu.SMEM)
    return pl.pallas_call(
        _combine_body,
        out_shape=jax.ShapeDtypeStruct((n_tok, D_MODEL), F32),
        grid=(n_tiles,),
        in_specs=[smem_tile, smem_tile, pl.BlockSpec(memory_space=pltpu.SMEM),
                  pl.BlockSpec((TOK_TILE, D_MODEL), lambda i: (i, 0)),
                  pl.BlockSpec((segs, 1, D_MODEL), lambda i: (i, 0, 0)),
                  pl.BlockSpec((TOK_TILE, TOP_K), lambda i: (i, 0)),
                  pl.BlockSpec((1, D_MODEL), lambda i: (0, 0)),
                  pl.BlockSpec(memory_space=pl.ANY)],
        out_specs=pl.BlockSpec((TOK_TILE, D_MODEL), lambda i: (i, 0)),
        scratch_shapes=[pltpu.VMEM((TOP_K, TOK_TILE, D_MODEL), F32), pltpu.SemaphoreType.DMA],
        compiler_params=pltpu.CompilerParams(dimension_semantics=("arbitrary",),
                                             vmem_limit_bytes=VMEM_LIMIT),
        name="moe_combine",
    )(idx.reshape(n_tiles, 1, TOK_TILE * TOP_K), rank.reshape(n_tiles, 1, TOK_TILE * TOP_K),
      pstart, xs1, g2_blocks, wts, final_g.reshape(1, D_MODEL), ys)


def kernel(x_prompt, x_sample, c_prompt, c_sample, state_ssm_re, state_ssm_im, norm1_g, norm2_g,
           w_ada, b_ada, w_in, w_s, b_s, g_v, lam_re, lam_im, log_dt, b_re, b_im, c_re, c_im,
           d_skip, w_glu, b_glu, out_g_a, out_g_b, w_out, w_router, router_bias, w_gate, w_up,
           w_down, ws_gate, ws_up, ws_down, final_g):
    depth = norm1_g.shape[0]
    assert depth == 1
    bp, sp, _ = x_prompt.shape
    bs, ss, _ = x_sample.shape
    n_p, n_s = bp * sp, bs * ss
    n_tok = n_p + n_s
    l = 0

    mod = _adaln(jnp.concatenate([c_prompt, c_sample], axis=0), w_ada[l], b_ada[l])
    mod = mod.reshape(bp + bs, 6, D_MODEL)

    wb, wc, tabs = _s5_tables(lam_re[l], lam_im[l], log_dt[l], b_re[l], b_im[l], c_re[l], c_im[l])
    pos = jnp.arange(SGU_LEN)
    mask = (pos[:, None] // CHUNK) >= (pos[None, :] // CHUNK)
    ws_masked = jnp.where(mask[None], w_s[l], 0.0)
    row = lax.broadcasted_iota(I32, (MIX_ROWS, MIX_ROWS), 0)
    col = lax.broadcasted_iota(I32, (MIX_ROWS, MIX_ROWS), 1)
    ltri = (col < row).astype(BF16)

    def mix_weights(ln):
        bsf = jnp.repeat(b_s[l][:, :ln].T, HEAD_A, axis=1)
        return [norm1_g[l].reshape(1, D_MODEL), w_in[l].astype(BF16),
                ws_masked[:, :ln, :ln].astype(BF16), bsf, g_v[l].reshape(1, D_A),
                wb, wc, tabs, d_skip[l].reshape(1, D_B), w_glu[l].astype(BF16),
                b_glu[l].reshape(1, D_B), out_g_a[l].reshape(1, D_A), out_g_b[l].reshape(1, D_B),
                w_out[l].astype(BF16), norm2_g[l].reshape(1, D_MODEL), w_router[l].astype(BF16),
                router_bias[l].reshape(1, N_EXPERTS), ws_gate[l].astype(BF16),
                ws_up[l].astype(BF16), ws_down[l].astype(BF16), ltri]

    h0_p = jnp.zeros((bp, SUB, STATE_COLS), F32)
    cnt0 = jnp.zeros((SUB, N_EXPERTS), F32)
    bufs = (jnp.zeros((n_tok, D_MODEL), F32), jnp.zeros((n_tok, D_MODEL), F32),
            jnp.zeros((n_tok, SLAB), I32))
    xs1, h2, slab, hfin_p, cnt_p = _mix(
        x_prompt, mod[:bp], h0_p, cnt0, mix_weights(SGU_LEN), bufs,
        n_total=n_tok, row_offset=0, seqs=1, ln=SGU_LEN, chain=True, emit_v=False)
    h0_s = jnp.broadcast_to(_lay(state_ssm_re[l], state_ssm_im[l])[:, None, :], (bs, SUB, STATE_COLS))
    seqs_s = MIX_ROWS // ss
    xs1, h2, slab, hfin_s, cnt_all, v_rows = _mix(
        x_sample, mod[bp:], h0_s, cnt_p, mix_weights(ss), (xs1, h2, slab),
        n_total=n_tok, row_offset=n_p, seqs=seqs_s, ln=ss, chain=False, emit_v=True)

    idx = slab[:, :TOP_K]
    rank = slab[:, TOP_K:2 * TOP_K]
    wts = lax.bitcast_convert_type(slab[:, 2 * TOP_K:3 * TOP_K], F32)
    counts = cnt_all[0].astype(I32)
    padded = (counts + MOE_BLK - 1) // MOE_BLK * MOE_BLK
    pend = jnp.cumsum(padded).astype(I32)
    pstart = pend - padded
    n_blocks = n_tok * TOP_K // MOE_BLK + N_EXPERTS
    n_used = (pend[-1:] // MOE_BLK).astype(I32)
    blk_start = jnp.arange(n_blocks, dtype=I32) * MOE_BLK
    blk_exp = jnp.minimum(jnp.searchsorted(pend, blk_start, side="right"), N_EXPERTS - 1).astype(I32)

    xs = _dispatch(idx, rank, pstart, pend, h2, n_blocks * MOE_BLK)
    ys = _experts(blk_exp, n_used, xs, w_gate[l], w_up[l], w_down[l])

    g2 = mod[:, 5, :]
    g2_blocks = jnp.concatenate([jnp.repeat(g2[:bp], sp // CHUNK, axis=0),
                                 jnp.repeat(g2[bp:], ss // CHUNK, axis=0)], axis=0)
    y = _combine(idx, rank, pstart, xs1, g2_blocks.reshape(n_tok // CHUNK, 1, D_MODEL), wts,
                 final_g, ys)

    y_prompt = y[:n_p].reshape(bp, sp, D_MODEL)
    y_sample = y[n_p:].reshape(bs, ss, D_MODEL)
    re_p, im_p = _unlay(hfin_p[:, 0, :])
    re_s, im_s = _unlay(hfin_s[:, 0, :])
    return (y_prompt, y_sample, re_p[None], im_p[None], re_s[None], im_s[None], v_rows[None])
```

```python
import functools

import jax
import jax.numpy as jnp
from jax import lax
from jax.experimental import pallas as pl
from jax.experimental.pallas import tpu as pltpu

F32 = jnp.float32
BF16 = jnp.bfloat16
I32 = jnp.int32

D_MODEL = 1024
D_A = 512
D_B = 512
N_HEADS_A = 4
HEAD_A = 128
SSM_GROUP = 16
N_GROUPS_B = 32
SSM_STATE = 64
N_EXPERTS = 256
TOP_K = 8
D_EXPERT = 256
ROUTE_SCALE = 2.5
CHUNK = 64
SGU_LEN = 128
EPS = 1e-6

STATE_CHUNKS = 4
GROUPS_PER_CHUNK = N_GROUPS_B // STATE_CHUNKS
HALF = GROUPS_PER_CHUNK * SSM_STATE
STATE_COLS = STATE_CHUNKS * 2 * HALF
SCAN_W = 256
SUB = 8
LANES = 128
ROW_TILES = D_MODEL // LANES

TAB_PW = 0
TAB_MD = 8
TAB_P8 = 11

MIX_ROWS = 256
MOE_BLK = 256
TOK_TILE = 256
SLAB = 128
VMEM_LIMIT = 56 * 1024 * 1024


def _dot(a, b):
    return jnp.dot(a, b, preferred_element_type=F32)


def _rms(x):
    return lax.rsqrt(jnp.mean(x * x, axis=-1, keepdims=True) + EPS)


def _tile_rows(j, n):
    return pl.ds(j, n, stride=ROW_TILES)


def _adaln_body(c_ref, w_ref, b_ref, o_ref):
    c = c_ref[...]
    o_ref[...] = _dot(jax.nn.silu(c).astype(BF16), w_ref[...].astype(BF16)) + b_ref[...]


def _adaln(c, w_ada, b_ada):
    n = c.shape[0]
    cols = w_ada.shape[1]
    blk = 1536
    return pl.pallas_call(
        _adaln_body,
        out_shape=jax.ShapeDtypeStruct((n, cols), F32),
        grid=(cols // blk,),
        in_specs=[pl.BlockSpec((n, D_MODEL), lambda j: (0, 0)),
                  pl.BlockSpec((D_MODEL, blk), lambda j: (0, j)),
                  pl.BlockSpec((1, blk), lambda j: (0, j))],
        out_specs=pl.BlockSpec((n, blk), lambda j: (0, j)),
        compiler_params=pltpu.CompilerParams(dimension_semantics=("arbitrary",)),
        name="adaln",
    )(c, w_ada, b_ada.reshape(1, cols))


def _lay(re, im):
    lead = re.shape[:-2]
    re = re.reshape(lead + (STATE_CHUNKS, HALF))
    im = im.reshape(lead + (STATE_CHUNKS, HALF))
    return jnp.concatenate([re, im], axis=-1).reshape(lead + (STATE_COLS,))


def _unlay(v):
    lead = v.shape[:-1]
    v = v.reshape(lead + (STATE_CHUNKS, 2, HALF))
    re = v[..., 0, :].reshape(lead + (N_GROUPS_B, SSM_STATE))
    im = v[..., 1, :].reshape(lead + (N_GROUPS_B, SSM_STATE))
    return re, im


def _s5_tables(lam_re, lam_im, log_dt, b_re, b_im, c_re, c_im):
    dt = jnp.exp(log_dt.astype(F32))[:, None]
    lr, li = lam_re.astype(F32), lam_im.astype(F32)

    def apow(k):
        mag = jnp.exp(lr * dt * k)
        return mag * jnp.cos(li * dt * k), mag * jnp.sin(li * dt * k)

    ar, ai = apow(1.0)
    den = lr * lr + li * li
    nr, ni = ar - 1.0, ai
    kr, ki = (nr * lr + ni * li) / den, (ni * lr - nr * li) / den
    br, bi = b_re.astype(F32), b_im.astype(F32)
    bbr = kr[..., None] * br - ki[..., None] * bi
    bbi = kr[..., None] * bi + ki[..., None] * br
    eye = jnp.eye(GROUPS_PER_CHUNK, dtype=F32)

    def bproj(bb):
        bb = bb.reshape(STATE_CHUNKS, GROUPS_PER_CHUNK, SSM_STATE, SSM_GROUP)
        w = jnp.einsum("mgph,gk->mghkp", bb, eye)
        return w.reshape(STATE_CHUNKS, GROUPS_PER_CHUNK * SSM_GROUP, HALF)

    wb = jnp.concatenate([bproj(bbr), bproj(bbi)], axis=-1).astype(BF16)

    def cproj(cc):
        cc = cc.reshape(STATE_CHUNKS, GROUPS_PER_CHUNK, SSM_GROUP, SSM_STATE)
        w = jnp.einsum("mghp,gk->mgpkh", cc, eye)
        return w.reshape(STATE_CHUNKS, HALF, GROUPS_PER_CHUNK * SSM_GROUP)

    wc = jnp.concatenate([cproj(c_re.astype(F32)), cproj(-c_im.astype(F32))], axis=1).astype(BF16)

    rows = jnp.arange(SUB, dtype=F32)
    tabs = []
    for i in range(SUB):
        pr, pi = apow(float(i + 1))
        tabs.append(jnp.broadcast_to(_lay(pr, pi)[None], (SUB, STATE_COLS)))
    for d in (1, 2, 4):
        pr, pi = apow(float(SUB * d))
        keep = (rows >= d).astype(F32)[:, None]
        tabs.append(_lay(pr, pi)[None] * keep)
    pr, pi = apow(SUB * rows[:, None, None])
    tabs.append(_lay(pr, pi))
    return wb, wc, jnp.stack(tabs)


def _cmul(ar, ai, br, bi):
    return ar * br - ai * bi, ar * bi + ai * br


def _s5_scan_block(bu_ref, row0, h_ref, tab_ref):
    row_id = lax.broadcasted_iota(I32, (SUB, SCAN_W), 0)
    tiles = SCAN_W // LANES
    for m in range(STATE_CHUNKS):
        for hf in range(HALF // SCAN_W):
            c_re0 = m * 2 * HALF + hf * SCAN_W
            c_im0 = c_re0 + HALF
            cre = pl.ds(c_re0, SCAN_W)
            cim = pl.ds(c_im0, SCAN_W)

            def tab(slot):
                return tab_ref[slot, :, cre], tab_ref[slot, :, cim]

            def load(i, c0):
                rows = pl.ds(row0 + i, SUB, stride=SUB)
                return jnp.concatenate([bu_ref[c0 // LANES + j, rows, :] for j in range(tiles)], axis=1)

            def store(i, c0, val):
                rows = pl.ds(row0 + i, SUB, stride=SUB)
                for j in range(tiles):
                    bu_ref[c0 // LANES + j, rows, :] = val[:, j * LANES:(j + 1) * LANES]

            a_re, a_im = tab(TAB_PW)
            s_re = load(0, c_re0)
            s_im = load(0, c_im0)
            loc = [(s_re, s_im)]
            for i in range(1, SUB):
                p_re, p_im = _cmul(a_re, a_im, s_re, s_im)
                s_re = p_re + load(i, c_re0)
                s_im = p_im + load(i, c_im0)
                loc.append((s_re, s_im))
            e_re, e_im = s_re, s_im
            for n, d in enumerate((1, 2, 4)):
                m_re, m_im = tab(TAB_MD + n)
                q_re, q_im = _cmul(m_re, m_im, pltpu.roll(e_re, d, 0), pltpu.roll(e_im, d, 0))
                e_re, e_im = e_re + q_re, e_im + q_im
            p8_re, p8_im = tab(TAB_P8)
            c_re, c_im = _cmul(p8_re, p8_im, h_ref[:, cre], h_ref[:, cim])
            c_re = c_re + jnp.where(row_id >= 1, pltpu.roll(e_re, 1, 0), 0.0)
            c_im = c_im + jnp.where(row_id >= 1, pltpu.roll(e_im, 1, 0), 0.0)
            for i in range(SUB):
                w_re, w_im = tab(TAB_PW + i)
                q_re, q_im = _cmul(w_re, w_im, c_re, c_im)
                f_re, f_im = loc[i][0] + q_re, loc[i][1] + q_im
                store(i, c_re0, f_re)
                store(i, c_im0, f_im)
            h_ref[:, cre] = jnp.broadcast_to(f_re[SUB - 1:SUB, :], (SUB, SCAN_W))
            h_ref[:, cim] = jnp.broadcast_to(f_im[SUB - 1:SUB, :], (SUB, SCAN_W))


def _mix_tile(seqs, ln, chain, new_seq, x_ref, mod_ref, h0_ref, ws_ref, bsf_ref, shared, outs,
              hfin_ref, v_ref, scratch):
    (n1g_ref, win_ref, gv_ref, wb_ref, wc_ref, tab_ref, dsk_ref, wglu_ref, bglu_ref, oga_ref,
     ogb_ref, wout_ref, n2g_ref, wr_ref, rb_ref, wsg_ref, wsu_ref, wsd_ref, ltri_ref) = shared
    xs1_ref, h2t_ref, slab_ref, wslab_ref, cnt_ref = outs
    bu_ref, h_scr, cnt_scr = scratch
    t_rows = MIX_ROWS
    rows_per_seq = t_rows // seqs

    x = x_ref[...].reshape(t_rows, D_MODEL)

    def modrow(j):
        parts = [jnp.broadcast_to(mod_ref[s, j:j + 1, :], (rows_per_seq, D_MODEL)) for s in range(seqs)]
        return parts[0] if seqs == 1 else jnp.concatenate(parts, axis=0)

    h = (x * _rms(x)) * n1g_ref[...] * (1.0 + modrow(1)) + modrow(0)
    z = _dot(h.astype(BF16), win_ref[...])
    u_a = z[:, :D_A]
    v_a = z[:, D_A:2 * D_A]
    u_b = z[:, 2 * D_A:]

    vh_parts = []
    for hh in range(N_HEADS_A):
        cols = slice(hh * HEAD_A, (hh + 1) * HEAD_A)
        vv = v_a[:, cols]
        vh_parts.append((vv * _rms(vv)) * gv_ref[:, cols])
    if v_ref is not None:
        v_ref[...] = jnp.concatenate(vh_parts, axis=1).reshape(seqs, rows_per_seq, D_A)
    n_sgu = t_rows // ln
    s_parts = []
    for hh in range(N_HEADS_A):
        vb = vh_parts[hh].astype(BF16)
        w_h = ws_ref[hh]
        s_parts.append(jnp.concatenate(
            [_dot(w_h, vb[c * ln:(c + 1) * ln, :]) for c in range(n_sgu)], axis=0))
    bsf = bsf_ref[...]
    s_mix = jnp.concatenate(s_parts, axis=1) + jnp.concatenate([bsf] * n_sgu, axis=0)
    y_a = u_a * s_mix

    ub16 = u_b.astype(BF16)
    gc = GROUPS_PER_CHUNK * SSM_GROUP
    tiles_per_chunk = 2 * HALF // LANES
    for m in range(STATE_CHUNKS):
        bu = _dot(ub16[:, m * gc:(m + 1) * gc], wb_ref[m])
        for j in range(tiles_per_chunk):
            bu_ref[m * tiles_per_chunk + j] = bu[:, j * LANES:(j + 1) * LANES]
    n_blk = t_rows // CHUNK
    if chain:
        @pl.when(new_seq)
        def _():
            h_scr[...] = jnp.zeros_like(h_scr)
    for blk in range(n_blk):
        if not chain:
            h_scr[...] = h0_ref[blk]
        _s5_scan_block(bu_ref, blk * CHUNK, h_scr, tab_ref)
        if not chain:
            hfin_ref[blk] = h_scr[...]
    if chain:
        hfin_ref[0] = h_scr[...]
    y_parts = []
    for m in range(STATE_CHUNKS):
        st = jnp.concatenate([bu_ref[m * tiles_per_chunk + j] for j in range(tiles_per_chunk)], axis=1)
        y_parts.append(_dot(st.astype(BF16), wc_ref[m]))
    y_s = jnp.concatenate(y_parts, axis=1) + dsk_ref[...] * u_b
    g_b = jax.nn.gelu(y_s)
    y_b = g_b * jax.nn.sigmoid(_dot(g_b.astype(BF16), wglu_ref[...]) + bglu_ref[...])

    na = (y_a * _rms(y_a)) * oga_ref[...]
    nb = (y_b * _rms(y_b)) * ogb_ref[...]
    mix = _dot(jnp.concatenate([na, nb], axis=1).astype(BF16), wout_ref[...])
    x1 = x + modrow(2) * mix

    h2 = (x1 * _rms(x1)) * n2g_ref[...] * (1.0 + modrow(4)) + modrow(3)
    for j in range(ROW_TILES):
        h2t_ref[_tile_rows(j, t_rows), :] = h2[:, j * LANES:(j + 1) * LANES]
    h2b = h2.astype(BF16)
    act = jax.nn.silu(_dot(h2b, wsg_ref[...])) * _dot(h2b, wsu_ref[...])
    shared_out = _dot(act.astype(BF16), wsd_ref[...])
    xs1_ref[...] = x1 + modrow(5) * shared_out

    scores = jax.nn.sigmoid(_dot(h2b, wr_ref[...]))
    lane = lax.broadcasted_iota(I32, (t_rows, N_EXPERTS), 1).astype(F32)
    work = scores + rb_ref[...]
    onehot = jnp.zeros((t_rows, N_EXPERTS), F32)
    idxs, sels = [], []
    for _ in range(TOP_K):
        top = jnp.max(work, axis=-1, keepdims=True)
        idx = jnp.min(jnp.where(work == top, lane, float(N_EXPERTS)), axis=-1, keepdims=True)
        pick = lane == idx
        sels.append(jnp.sum(jnp.where(pick, scores, 0.0), axis=-1, keepdims=True))
        idxs.append(idx)
        work = jnp.where(pick, -jnp.inf, work)
        onehot = jnp.where(pick, 1.0, onehot)
    total = sels[0]
    for k in range(1, TOP_K):
        total = total + sels[k]
    ranktab = _dot(ltri_ref[...], onehot.astype(BF16)) + cnt_scr[0:1, :]
    slab_lane = lax.broadcasted_iota(I32, (t_rows, SLAB), 1)
    slab = jnp.zeros((t_rows, SLAB), I32)
    wslab = jnp.zeros((t_rows, SLAB), F32)
    for k in range(TOP_K):
        rank = jnp.sum(jnp.where(lane == idxs[k], ranktab, 0.0), axis=-1, keepdims=True)
        slab = jnp.where(slab_lane == k, idxs[k].astype(I32), slab)
        slab = jnp.where(slab_lane == TOP_K + k, rank.astype(I32), slab)
        wslab = jnp.where(slab_lane == k, ROUTE_SCALE * sels[k] / total, wslab)
    slab_ref[...] = slab
    wslab_ref[...] = wslab
    cnt_scr[...] = cnt_scr[...] + jnp.sum(onehot, axis=0, keepdims=True)
    cnt_ref[...] = cnt_scr[...]


def _mix_body(n_prompt_tiles, tiles_per_seq, seqs_s, ln_s,
              xp_ref, xsm_ref, modp_ref, mods_ref, h0s_ref, wsp_ref, bsfp_ref, wss_ref, bsfs_ref,
              *rest):
    shared = rest[:19]
    xs1_ref, h2t_ref, slab_ref, wslab_ref, cnt_ref, hfp_ref, hfs_ref, v_ref = rest[19:27]
    scratch = rest[27:]
    outs = (xs1_ref, h2t_ref, slab_ref, wslab_ref, cnt_ref)
    s = pl.program_id(0)

    @pl.when(s == 0)
    def _():
        scratch[2][...] = jnp.zeros_like(scratch[2])

    @pl.when(s < n_prompt_tiles)
    def _():
        _mix_tile(1, SGU_LEN, True, lax.rem(s, tiles_per_seq) == 0, xp_ref, modp_ref, None,
                  wsp_ref, bsfp_ref, shared, outs, hfp_ref, None, scratch)

    @pl.when(s >= n_prompt_tiles)
    def _():
        _mix_tile(seqs_s, ln_s, False, None, xsm_ref, mods_ref, h0s_ref,
                  wss_ref, bsfs_ref, shared, outs, hfs_ref, v_ref, scratch)


def _mix(x_prompt, x_sample, mod, h0_s, ws_p, bsf_p, ws_s, bsf_s, shared):
    bp, sp, _ = x_prompt.shape
    bs, ss, _ = x_sample.shape
    assert sp % MIX_ROWS == 0 and MIX_ROWS % ss == 0 and ss == CHUNK
    seqs_s = MIX_ROWS // ss
    assert bs % seqs_s == 0 and bp % seqs_s == 0
    tiles_per_seq = sp // MIX_ROWS
    n_pt = bp * tiles_per_seq
    n_st = bs // seqs_s
    n_tok = bp * sp + bs * ss

    def p_tile(s):
        return jnp.minimum(s, n_pt - 1)

    def s_tile(s):
        return jnp.maximum(s - n_pt, 0)

    def const(shape):
        nd = len(shape)
        return pl.BlockSpec(shape, lambda s: (0,) * nd)

    in_specs = [
        pl.BlockSpec((1, MIX_ROWS, D_MODEL), lambda s: (p_tile(s) // tiles_per_seq, p_tile(s) % tiles_per_seq, 0)),
        pl.BlockSpec((seqs_s, ss, D_MODEL), lambda s: (s_tile(s), 0, 0)),
        pl.BlockSpec((1, 6, D_MODEL), lambda s: (p_tile(s) // tiles_per_seq, 0, 0)),
        pl.BlockSpec((seqs_s, 6, D_MODEL), lambda s: (bp // seqs_s + s_tile(s), 0, 0)),
        pl.BlockSpec((seqs_s, SUB, STATE_COLS), lambda s: (s_tile(s), 0, 0)),
        const(ws_p.shape), const(bsf_p.shape), const(ws_s.shape), const(bsf_s.shape),
    ] + [const(w.shape) for w in shared]
    out_shape = [
        jax.ShapeDtypeStruct((n_tok, D_MODEL), F32),
        jax.ShapeDtypeStruct((n_tok * ROW_TILES, LANES), F32),
        jax.ShapeDtypeStruct((n_tok, SLAB), I32),
        jax.ShapeDtypeStruct((n_tok, SLAB), F32),
        jax.ShapeDtypeStruct((SUB, N_EXPERTS), F32),
        jax.ShapeDtypeStruct((bp, SUB, STATE_COLS), F32),
        jax.ShapeDtypeStruct((bs, SUB, STATE_COLS), F32),
        jax.ShapeDtypeStruct((bs, ss, D_A), F32),
    ]
    out_specs = [
        pl.BlockSpec((MIX_ROWS, D_MODEL), lambda s: (s, 0)),
        pl.BlockSpec((MIX_ROWS * ROW_TILES, LANES), lambda s: (s, 0)),
        pl.BlockSpec((MIX_ROWS, SLAB), lambda s: (s, 0)),
        pl.BlockSpec((MIX_ROWS, SLAB), lambda s: (s, 0)),
        const((SUB, N_EXPERTS)),
        pl.BlockSpec((1, SUB, STATE_COLS), lambda s: (p_tile(s) // tiles_per_seq, 0, 0)),
        pl.BlockSpec((seqs_s, SUB, STATE_COLS), lambda s: (s_tile(s), 0, 0)),
        pl.BlockSpec((seqs_s, ss, D_A), lambda s: (s_tile(s), 0, 0)),
    ]
    return pl.pallas_call(
        functools.partial(_mix_body, n_pt, tiles_per_seq, seqs_s, ss),
        out_shape=out_shape,
        grid=(n_pt + n_st,),
        in_specs=in_specs,
        out_specs=out_specs,
        scratch_shapes=[pltpu.VMEM((STATE_COLS // LANES, MIX_ROWS, LANES), F32),
                        pltpu.VMEM((SUB, STATE_COLS), F32),
                        pltpu.VMEM((SUB, N_EXPERTS), F32)],
        compiler_params=pltpu.CompilerParams(dimension_semantics=("arbitrary",),
                                             vmem_limit_bytes=VMEM_LIMIT),
        name="mix",
    )(x_prompt, x_sample, mod, mod, h0_s, ws_p, bsf_p, ws_s, bsf_s, *shared)


def _token_rows(t):
    return pl.ds(pl.multiple_of(t * ROW_TILES, ROW_TILES), ROW_TILES)


def _dispatch_body(idx_ref, rank_ref, pstart_ref, pend_ref, h2t_ref, zero_hbm, xs_hbm, sem, zsem):
    i = pl.program_id(0)
    blk_rows = MOE_BLK * ROW_TILES
    n_blocks = xs_hbm.shape[0] // blk_rows

    @pl.when(i == 0)
    def _():
        def clear(blk):
            dst = xs_hbm.at[pl.ds(pl.multiple_of(blk * blk_rows, blk_rows), blk_rows)]
            return pltpu.make_async_copy(zero_hbm, dst, zsem)

        def each(fn):
            def fill(e, carry):
                @pl.when(pend_ref[e] > pstart_ref[e])
                def _():
                    fn(clear(pend_ref[e] // MOE_BLK - 1))
                return carry
            lax.fori_loop(0, N_EXPERTS, fill, 0)

            def tail(b, carry):
                fn(clear(b))
                return carry
            lax.fori_loop(pend_ref[N_EXPERTS - 1] // MOE_BLK, n_blocks, tail, 0)

        each(lambda cp: cp.start())
        each(lambda cp: cp.wait())

    def issue(t, carry):
        for k in range(TOP_K):
            e = idx_ref[0, 0, t * TOP_K + k]
            dst = pstart_ref[e] + rank_ref[0, 0, t * TOP_K + k]
            pltpu.make_async_copy(h2t_ref.at[_token_rows(t)], xs_hbm.at[_token_rows(dst)],
                                  sem).start(priority=k % 2)
        return carry
    lax.fori_loop(0, TOK_TILE, issue, 0)
    for _ in range(TOP_K):
        pltpu.make_async_copy(h2t_ref, xs_hbm.at[pl.ds(0, TOK_TILE * ROW_TILES)], sem).wait()


def _dispatch(idx, rank, pstart, pend, h2t, n_rows):
    n_tiles = h2t.shape[0] // (TOK_TILE * ROW_TILES)
    smem_tile = pl.BlockSpec((1, 1, TOK_TILE * TOP_K), lambda i: (i, 0, 0), memory_space=pltpu.SMEM)
    smem_all = pl.BlockSpec(memory_space=pltpu.SMEM)
    return pl.pallas_call(
        _dispatch_body,
        out_shape=jax.ShapeDtypeStruct((n_rows * ROW_TILES, LANES), F32),
        grid=(n_tiles,),
        in_specs=[smem_tile, smem_tile, smem_all, smem_all,
                  pl.BlockSpec((TOK_TILE * ROW_TILES, LANES), lambda i: (i, 0)),
                  pl.BlockSpec(memory_space=pl.ANY)],
        out_specs=pl.BlockSpec(memory_space=pl.ANY),
        scratch_shapes=[pltpu.SemaphoreType.DMA, pltpu.SemaphoreType.DMA],
        compiler_params=pltpu.CompilerParams(dimension_semantics=("arbitrary",)),
        name="moe_dispatch",
    )(idx.reshape(n_tiles, 1, TOK_TILE * TOP_K), rank.reshape(n_tiles, 1, TOK_TILE * TOP_K),
      pstart, pend, h2t, jnp.zeros((MOE_BLK * ROW_TILES, LANES), F32))


def _experts_body(bexp_ref, nused_ref, xs_ref, wg_ref, wu_ref, wd_ref, ys_ref):
    used = pl.program_id(0) < nused_ref[0]

    @pl.when(used)
    def _():
        xb = jnp.concatenate([xs_ref[_tile_rows(j, MOE_BLK), :] for j in range(ROW_TILES)],
                             axis=1).astype(BF16)
        gate = _dot(xb, wg_ref[0].astype(BF16))
        up = _dot(xb, wu_ref[0].astype(BF16))
        act = (jax.nn.silu(gate) * up).astype(BF16)
        y = _dot(act, wd_ref[0].astype(BF16))
        for j in range(ROW_TILES):
            ys_ref[_tile_rows(j, MOE_BLK), :] = y[:, j * LANES:(j + 1) * LANES]

    @pl.when(jnp.logical_not(used))
    def _():
        ys_ref[...] = jnp.zeros_like(ys_ref)


def _experts(blk_exp, n_used, xs, w_gate, w_up, w_down):
    blk_rows = MOE_BLK * ROW_TILES
    n_blocks = xs.shape[0] // blk_rows

    def row_map(b, bexp, nused):
        return (jnp.minimum(b, nused[0] - 1), 0)

    def out_map(b, bexp, nused):
        return (b, 0)

    def w_map(b, bexp, nused):
        return (bexp[jnp.minimum(b, nused[0] - 1)], 0, 0)

    return pl.pallas_call(
        _experts_body,
        out_shape=jax.ShapeDtypeStruct(xs.shape, F32),
        grid_spec=pltpu.PrefetchScalarGridSpec(
            num_scalar_prefetch=2,
            grid=(n_blocks,),
            in_specs=[pl.BlockSpec((blk_rows, LANES), row_map),
                      pl.BlockSpec((1, D_MODEL, D_EXPERT), w_map),
                      pl.BlockSpec((1, D_MODEL, D_EXPERT), w_map),
                      pl.BlockSpec((1, D_EXPERT, D_MODEL), w_map)],
            out_specs=pl.BlockSpec((blk_rows, LANES), out_map)),
        compiler_params=pltpu.CompilerParams(dimension_semantics=("arbitrary",),
                                             vmem_limit_bytes=VMEM_LIMIT),
        name="moe_experts",
    )(blk_exp, n_used, xs, w_gate, w_up, w_down)


def _combine_body(n_prompt_tiles, idx_ref, rank_ref, pstart_ref, xs1_ref, g2_ref, wts_ref, fg_ref,
                  ys_hbm, yp_ref, ysm_ref, buf_ref, sem):
    i = pl.program_id(0)

    def issue(t, carry):
        for k in range(TOP_K):
            e = idx_ref[0, 0, t * TOP_K + k]
            src = pstart_ref[e] + rank_ref[0, 0, t * TOP_K + k]
            pltpu.make_async_copy(ys_hbm.at[_token_rows(src)], buf_ref.at[k, _token_rows(t)],
                                  sem).start(priority=k % 2)
        return carry
    lax.fori_loop(0, TOK_TILE, issue, 0)
    for k in range(TOP_K):
        pltpu.make_async_copy(ys_hbm.at[pl.ds(0, TOK_TILE * ROW_TILES)], buf_ref.at[k], sem).wait()
    wts = wts_ref[...]
    parts = []
    for j in range(ROW_TILES):
        acc = wts[:, 0:1] * buf_ref[0, _tile_rows(j, TOK_TILE), :]
        for k in range(1, TOP_K):
            acc = acc + wts[:, k:k + 1] * buf_ref[k, _tile_rows(j, TOK_TILE), :]
        parts.append(acc)
    seg = TOK_TILE // g2_ref.shape[0]
    g2 = jnp.concatenate([jnp.broadcast_to(g2_ref[s], (seg, D_MODEL)) for s in range(g2_ref.shape[0])],
                         axis=0)
    x = xs1_ref[...] + g2 * jnp.concatenate(parts, axis=1)
    y = (x * _rms(x)) * fg_ref[...]

    @pl.when(i < n_prompt_tiles)
    def _():
        yp_ref[...] = y

    @pl.when(i >= n_prompt_tiles)
    def _():
        ysm_ref[...] = y


def _combine(idx, rank, pstart, xs1, g2_blocks, wts, final_g, ys, n_p):
    n_tok = xs1.shape[0]
    n_tiles = n_tok // TOK_TILE
    n_pt = n_p // TOK_TILE
    segs = TOK_TILE // CHUNK
    smem_tile = pl.BlockSpec((1, 1, TOK_TILE * TOP_K), lambda i: (i, 0, 0), memory_space=pltpu.SMEM)
    return pl.pallas_call(
        functools.partial(_combine_body, n_pt),
        out_shape=[jax.ShapeDtypeStruct((n_p, D_MODEL), F32),
                   jax.ShapeDtypeStruct((n_tok - n_p, D_MODEL), F32)],
        grid=(n_tiles,),
        in_specs=[smem_tile, smem_tile, pl.BlockSpec(memory_space=pltpu.SMEM),
                  pl.BlockSpec((TOK_TILE, D_MODEL), lambda i: (i, 0)),
                  pl.BlockSpec((segs, 1, D_MODEL), lambda i: (i, 0, 0)),
                  pl.BlockSpec((TOK_TILE, TOP_K), lambda i: (i, 0)),
                  pl.BlockSpec((1, D_MODEL), lambda i: (0, 0)),
                  pl.BlockSpec(memory_space=pl.ANY)],
        out_specs=[pl.BlockSpec((TOK_TILE, D_MODEL), lambda i: (jnp.minimum(i, n_pt - 1), 0)),
                   pl.BlockSpec((TOK_TILE, D_MODEL), lambda i: (jnp.maximum(i - n_pt, 0), 0))],
        scratch_shapes=[pltpu.VMEM((TOP_K, TOK_TILE * ROW_TILES, LANES), F32),
                        pltpu.SemaphoreType.DMA],
        compiler_params=pltpu.CompilerParams(dimension_semantics=("arbitrary",),
                                             vmem_limit_bytes=VMEM_LIMIT),
        name="moe_combine",
    )(idx.reshape(n_tiles, 1, TOK_TILE * TOP_K), rank.reshape(n_tiles, 1, TOK_TILE * TOP_K),
      pstart, xs1, g2_blocks, wts, final_g.reshape(1, D_MODEL), ys)


def kernel(x_prompt, x_sample, c_prompt, c_sample, state_ssm_re, state_ssm_im, norm1_g, norm2_g,
           w_ada, b_ada, w_in, w_s, b_s, g_v, lam_re, lam_im, log_dt, b_re, b_im, c_re, c_im,
           d_skip, w_glu, b_glu, out_g_a, out_g_b, w_out, w_router, router_bias, w_gate, w_up,
           w_down, ws_gate, ws_up, ws_down, final_g):
    assert norm1_g.shape[0] == 1
    bp, sp, _ = x_prompt.shape
    bs, ss, _ = x_sample.shape
    n_p, n_s = bp * sp, bs * ss
    n_tok = n_p + n_s
    l = 0

    mod = _adaln(jnp.concatenate([c_prompt, c_sample], axis=0), w_ada[l], b_ada[l])
    mod = mod.reshape(bp + bs, 6, D_MODEL)

    wb, wc, tabs = _s5_tables(lam_re[l], lam_im[l], log_dt[l], b_re[l], b_im[l], c_re[l], c_im[l])
    pos = jnp.arange(SGU_LEN)
    mask = (pos[:, None] // CHUNK) >= (pos[None, :] // CHUNK)
    ws_masked = jnp.where(mask[None], w_s[l], 0.0)
    row = lax.broadcasted_iota(I32, (MIX_ROWS, MIX_ROWS), 0)
    col = lax.broadcasted_iota(I32, (MIX_ROWS, MIX_ROWS), 1)
    ltri = (col < row).astype(BF16)

    def sgu_weights(ln):
        bsf = jnp.repeat(b_s[l][:, :ln].T, HEAD_A, axis=1)
        return ws_masked[:, :ln, :ln].astype(BF16), bsf

    ws_p, bsf_p = sgu_weights(SGU_LEN)
    ws_s, bsf_s = sgu_weights(ss)
    shared = [norm1_g[l].reshape(1, D_MODEL), w_in[l].astype(BF16), g_v[l].reshape(1, D_A),
              wb, wc, tabs, d_skip[l].reshape(1, D_B), w_glu[l].astype(BF16),
              b_glu[l].reshape(1, D_B), out_g_a[l].reshape(1, D_A), out_g_b[l].reshape(1, D_B),
              w_out[l].astype(BF16), norm2_g[l].reshape(1, D_MODEL), w_router[l].astype(BF16),
              router_bias[l].reshape(1, N_EXPERTS), ws_gate[l].astype(BF16),
              ws_up[l].astype(BF16), ws_down[l].astype(BF16), ltri]
    h0_s = jnp.broadcast_to(_lay(state_ssm_re[l], state_ssm_im[l])[:, None, :], (bs, SUB, STATE_COLS))
    xs1, h2t, slab, wslab, cnt_all, hfin_p, hfin_s, v_rows = _mix(
        x_prompt, x_sample, mod, h0_s, ws_p, bsf_p, ws_s, bsf_s, shared)

    idx = slab[:, :TOP_K]
    rank = slab[:, TOP_K:2 * TOP_K]
    wts = wslab[:, :TOP_K]
    counts = cnt_all[0].astype(I32)
    padded = (counts + MOE_BLK - 1) // MOE_BLK * MOE_BLK
    pend = jnp.cumsum(padded).astype(I32)
    pstart = pend - padded
    n_blocks = -(-n_tok * TOP_K // MOE_BLK) + N_EXPERTS
    n_used = (pend[-1:] // MOE_BLK).astype(I32)
    blk_start = jnp.arange(n_blocks, dtype=I32) * MOE_BLK
    blk_exp = jnp.sum((pend[None, :] <= blk_start[:, None]).astype(I32), axis=1)
    blk_exp = jnp.minimum(blk_exp, N_EXPERTS - 1)

    xs = _dispatch(idx, rank, pstart, pend, h2t, n_blocks * MOE_BLK)
    ys = _experts(blk_exp, n_used, xs, w_gate[l], w_up[l], w_down[l])

    g2 = mod[:, 5, :]
    g2_blocks = jnp.concatenate([jnp.repeat(g2[:bp], sp // CHUNK, axis=0),
                                 jnp.repeat(g2[bp:], ss // CHUNK, axis=0)], axis=0)
    y_p, y_s = _combine(idx, rank, pstart, xs1, g2_blocks.reshape(n_tok // CHUNK, 1, D_MODEL), wts,
                        final_g, ys, n_p)

    re_p, im_p = _unlay(hfin_p[:, 0, :])
    re_s, im_s = _unlay(hfin_s[:, 0, :])
    return (y_p.reshape(bp, sp, D_MODEL), y_s.reshape(bs, ss, D_MODEL),
            re_p[None], im_p[None], re_s[None], im_s[None], v_rows[None])
```

```python
import functools

import jax
import jax.numpy as jnp
from jax import lax
from jax.experimental import pallas as pl
from jax.experimental.pallas import tpu as pltpu

F32 = jnp.float32
BF16 = jnp.bfloat16
I32 = jnp.int32

D_MODEL = 1024
D_A = 512
D_B = 512
N_HEADS_A = 4
HEAD_A = 128
SSM_GROUP = 16
N_GROUPS_B = 32
SSM_STATE = 64
N_EXPERTS = 256
TOP_K = 8
D_EXPERT = 256
ROUTE_SCALE = 2.5
CHUNK = 64
SGU_LEN = 128
EPS = 1e-6

STATE_CHUNKS = 4
GROUPS_PER_CHUNK = N_GROUPS_B // STATE_CHUNKS
HALF = GROUPS_PER_CHUNK * SSM_STATE
STATE_COLS = STATE_CHUNKS * 2 * HALF
SCAN_W = 256
SUB = 8
LANES = 128
ROW_TILES = D_MODEL // LANES

TAB_PW = 0
TAB_MD = 8
TAB_P8 = 11

MIX_ROWS = 256
MOE_BLK = 256
TOK_TILE = 256
SLAB = 128
VMEM_LIMIT = 56 * 1024 * 1024


def _dot(a, b):
    return jnp.dot(a, b, preferred_element_type=F32)


def _rms(x):
    return lax.rsqrt(jnp.mean(x * x, axis=-1, keepdims=True) + EPS)


def _tile_rows(j, n):
    return pl.ds(j, n, stride=ROW_TILES)


def _adaln_body(c_ref, w_ref, b_ref, o_ref):
    c = c_ref[...]
    o_ref[...] = _dot(jax.nn.silu(c).astype(BF16), w_ref[...].astype(BF16)) + b_ref[...]


def _adaln(c, w_ada, b_ada):
    n = c.shape[0]
    cols = w_ada.shape[1]
    blk = 1536
    return pl.pallas_call(
        _adaln_body,
        out_shape=jax.ShapeDtypeStruct((n, cols), F32),
        grid=(cols // blk,),
        in_specs=[pl.BlockSpec((n, D_MODEL), lambda j: (0, 0)),
                  pl.BlockSpec((D_MODEL, blk), lambda j: (0, j)),
                  pl.BlockSpec((1, blk), lambda j: (0, j))],
        out_specs=pl.BlockSpec((n, blk), lambda j: (0, j)),
        compiler_params=pltpu.CompilerParams(dimension_semantics=("arbitrary",)),
        name="adaln",
    )(c, w_ada, b_ada.reshape(1, cols))


def _lay(re, im):
    lead = re.shape[:-2]
    re = re.reshape(lead + (STATE_CHUNKS, HALF))
    im = im.reshape(lead + (STATE_CHUNKS, HALF))
    return jnp.concatenate([re, im], axis=-1).reshape(lead + (STATE_COLS,))


def _unlay(v):
    lead = v.shape[:-1]
    v = v.reshape(lead + (STATE_CHUNKS, 2, HALF))
    re = v[..., 0, :].reshape(lead + (N_GROUPS_B, SSM_STATE))
    im = v[..., 1, :].reshape(lead + (N_GROUPS_B, SSM_STATE))
    return re, im


def _s5_tables(lam_re, lam_im, log_dt, b_re, b_im, c_re, c_im):
    dt = jnp.exp(log_dt.astype(F32))[:, None]
    lr, li = lam_re.astype(F32), lam_im.astype(F32)

    def apow(k):
        mag = jnp.exp(lr * dt * k)
        return mag * jnp.cos(li * dt * k), mag * jnp.sin(li * dt * k)

    ar, ai = apow(1.0)
    den = lr * lr + li * li
    nr, ni = ar - 1.0, ai
    kr, ki = (nr * lr + ni * li) / den, (ni * lr - nr * li) / den
    br, bi = b_re.astype(F32), b_im.astype(F32)
    bbr = kr[..., None] * br - ki[..., None] * bi
    bbi = kr[..., None] * bi + ki[..., None] * br
    eye = jnp.eye(GROUPS_PER_CHUNK, dtype=F32)

    def bproj(bb):
        bb = bb.reshape(STATE_CHUNKS, GROUPS_PER_CHUNK, SSM_STATE, SSM_GROUP)
        w = jnp.einsum("mgph,gk->mghkp", bb, eye)
        return w.reshape(STATE_CHUNKS, GROUPS_PER_CHUNK * SSM_GROUP, HALF)

    wb = jnp.concatenate([bproj(bbr), bproj(bbi)], axis=-1).astype(BF16)

    def cproj(cc):
        cc = cc.reshape(STATE_CHUNKS, GROUPS_PER_CHUNK, SSM_GROUP, SSM_STATE)
        w = jnp.einsum("mghp,gk->mgpkh", cc, eye)
        return w.reshape(STATE_CHUNKS, HALF, GROUPS_PER_CHUNK * SSM_GROUP)

    wc = jnp.concatenate([cproj(c_re.astype(F32)), cproj(-c_im.astype(F32))], axis=1).astype(BF16)

    rows = jnp.arange(SUB, dtype=F32)
    tabs = []
    for i in range(SUB):
        pr, pi = apow(float(i + 1))
        tabs.append(jnp.broadcast_to(_lay(pr, pi)[None], (SUB, STATE_COLS)))
    for d in (1, 2, 4):
        pr, pi = apow(float(SUB * d))
        keep = (rows >= d).astype(F32)[:, None]
        tabs.append(_lay(pr, pi)[None] * keep)
    pr, pi = apow(SUB * rows[:, None, None])
    tabs.append(_lay(pr, pi))
    return wb, wc, jnp.stack(tabs)


def _cmul(ar, ai, br, bi):
    return ar * br - ai * bi, ar * bi + ai * br


def _s5_scan_block(bu_ref, row0, h_ref, tab_ref):
    row_id = lax.broadcasted_iota(I32, (SUB, SCAN_W), 0)
    tiles = SCAN_W // LANES
    for m in range(STATE_CHUNKS):
        for hf in range(HALF // SCAN_W):
            c_re0 = m * 2 * HALF + hf * SCAN_W
            c_im0 = c_re0 + HALF
            cre = pl.ds(c_re0, SCAN_W)
            cim = pl.ds(c_im0, SCAN_W)

            def tab(slot):
                return tab_ref[slot, :, cre], tab_ref[slot, :, cim]

            def load(i, c0):
                rows = pl.ds(row0 + i, SUB, stride=SUB)
                return jnp.concatenate([bu_ref[c0 // LANES + j, rows, :] for j in range(tiles)], axis=1)

            def store(i, c0, val):
                rows = pl.ds(row0 + i, SUB, stride=SUB)
                for j in range(tiles):
                    bu_ref[c0 // LANES + j, rows, :] = val[:, j * LANES:(j + 1) * LANES]

            a_re, a_im = tab(TAB_PW)
            s_re = load(0, c_re0)
            s_im = load(0, c_im0)
            loc = [(s_re, s_im)]
            for i in range(1, SUB):
                p_re, p_im = _cmul(a_re, a_im, s_re, s_im)
                s_re = p_re + load(i, c_re0)
                s_im = p_im + load(i, c_im0)
                loc.append((s_re, s_im))
            e_re, e_im = s_re, s_im
            for n, d in enumerate((1, 2, 4)):
                m_re, m_im = tab(TAB_MD + n)
                q_re, q_im = _cmul(m_re, m_im, pltpu.roll(e_re, d, 0), pltpu.roll(e_im, d, 0))
                e_re, e_im = e_re + q_re, e_im + q_im
            p8_re, p8_im = tab(TAB_P8)
            c_re, c_im = _cmul(p8_re, p8_im, h_ref[:, cre], h_ref[:, cim])
            c_re = c_re + jnp.where(row_id >= 1, pltpu.roll(e_re, 1, 0), 0.0)
            c_im = c_im + jnp.where(row_id >= 1, pltpu.roll(e_im, 1, 0), 0.0)
            for i in range(SUB):
                w_re, w_im = tab(TAB_PW + i)
                q_re, q_im = _cmul(w_re, w_im, c_re, c_im)
                f_re, f_im = loc[i][0] + q_re, loc[i][1] + q_im
                store(i, c_re0, f_re)
                store(i, c_im0, f_im)
            h_ref[:, cre] = jnp.broadcast_to(f_re[SUB - 1:SUB, :], (SUB, SCAN_W))
            h_ref[:, cim] = jnp.broadcast_to(f_im[SUB - 1:SUB, :], (SUB, SCAN_W))


def _mix_tile(seqs, ln, chain, new_seq, x_ref, mod_ref, h0_ref, ws_ref, bsf_ref, shared, outs,
              hfin_ref, v_ref, scratch):
    (n1g_ref, win_ref, gv_ref, wb_ref, wc_ref, tab_ref, dsk_ref, wglu_ref, bglu_ref, oga_ref,
     ogb_ref, wout_ref, n2g_ref, wr_ref, rb_ref, wsg_ref, wsu_ref, wsd_ref, ltri_ref) = shared
    xs1_ref, h2t_ref, slab_ref, wslab_ref, cnt_ref = outs
    bu_ref, h_scr, cnt_scr = scratch
    t_rows = MIX_ROWS
    rows_per_seq = t_rows // seqs

    x = x_ref[...].reshape(t_rows, D_MODEL)

    def modrow(j):
        parts = [jnp.broadcast_to(mod_ref[s, j:j + 1, :], (rows_per_seq, D_MODEL)) for s in range(seqs)]
        return parts[0] if seqs == 1 else jnp.concatenate(parts, axis=0)

    h = (x * _rms(x)) * n1g_ref[...] * (1.0 + modrow(1)) + modrow(0)
    z = _dot(h.astype(BF16), win_ref[...])
    u_a = z[:, :D_A]
    v_a = z[:, D_A:2 * D_A]
    u_b = z[:, 2 * D_A:]

    vh_parts = []
    for hh in range(N_HEADS_A):
        cols = slice(hh * HEAD_A, (hh + 1) * HEAD_A)
        vv = v_a[:, cols]
        vh_parts.append((vv * _rms(vv)) * gv_ref[:, cols])
    if v_ref is not None:
        v_ref[...] = jnp.concatenate(vh_parts, axis=1).reshape(seqs, rows_per_seq, D_A)
    n_sgu = t_rows // ln
    s_parts = []
    for hh in range(N_HEADS_A):
        vb = vh_parts[hh].astype(BF16)
        w_h = ws_ref[hh]
        s_parts.append(jnp.concatenate(
            [_dot(w_h, vb[c * ln:(c + 1) * ln, :]) for c in range(n_sgu)], axis=0))
    bsf = bsf_ref[...]
    s_mix = jnp.concatenate(s_parts, axis=1) + jnp.concatenate([bsf] * n_sgu, axis=0)
    y_a = u_a * s_mix

    ub16 = u_b.astype(BF16)
    gc = GROUPS_PER_CHUNK * SSM_GROUP
    tiles_per_chunk = 2 * HALF // LANES
    for m in range(STATE_CHUNKS):
        bu = _dot(ub16[:, m * gc:(m + 1) * gc], wb_ref[m])
        for j in range(tiles_per_chunk):
            bu_ref[m * tiles_per_chunk + j] = bu[:, j * LANES:(j + 1) * LANES]
    n_blk = t_rows // CHUNK
    if chain:
        @pl.when(new_seq)
        def _():
            h_scr[...] = jnp.zeros_like(h_scr)
    for blk in range(n_blk):
        if not chain:
            h_scr[...] = h0_ref[blk]
        _s5_scan_block(bu_ref, blk * CHUNK, h_scr, tab_ref)
        if not chain:
            hfin_ref[blk] = h_scr[...]
    if chain:
        hfin_ref[0] = h_scr[...]
    y_parts = []
    for m in range(STATE_CHUNKS):
        st = jnp.concatenate([bu_ref[m * tiles_per_chunk + j] for j in range(tiles_per_chunk)], axis=1)
        y_parts.append(_dot(st.astype(BF16), wc_ref[m]))
    y_s = jnp.concatenate(y_parts, axis=1) + dsk_ref[...] * u_b
    g_b = jax.nn.gelu(y_s)
    y_b = g_b * jax.nn.sigmoid(_dot(g_b.astype(BF16), wglu_ref[...]) + bglu_ref[...])

    na = (y_a * _rms(y_a)) * oga_ref[...]
    nb = (y_b * _rms(y_b)) * ogb_ref[...]
    mix = _dot(jnp.concatenate([na, nb], axis=1).astype(BF16), wout_ref[...])
    x1 = x + modrow(2) * mix

    h2 = (x1 * _rms(x1)) * n2g_ref[...] * (1.0 + modrow(4)) + modrow(3)
    for j in range(ROW_TILES):
        h2t_ref[_tile_rows(j, t_rows), :] = h2[:, j * LANES:(j + 1) * LANES]
    h2b = h2.astype(BF16)
    act = jax.nn.silu(_dot(h2b, wsg_ref[...])) * _dot(h2b, wsu_ref[...])
    shared_out = _dot(act.astype(BF16), wsd_ref[...])
    xs1_ref[...] = x1 + modrow(5) * shared_out

    scores = jax.nn.sigmoid(_dot(h2b, wr_ref[...]))
    lane = lax.broadcasted_iota(I32, (t_rows, N_EXPERTS), 1).astype(F32)
    work = scores + rb_ref[...]
    onehot = jnp.zeros((t_rows, N_EXPERTS), F32)
    idxs, sels = [], []
    for _ in range(TOP_K):
        top = jnp.max(work, axis=-1, keepdims=True)
        idx = jnp.min(jnp.where(work == top, lane, float(N_EXPERTS)), axis=-1, keepdims=True)
        pick = lane == idx
        sels.append(jnp.sum(jnp.where(pick, scores, 0.0), axis=-1, keepdims=True))
        idxs.append(idx)
        work = jnp.where(pick, -jnp.inf, work)
        onehot = jnp.where(pick, 1.0, onehot)
    total = sels[0]
    for k in range(1, TOP_K):
        total = total + sels[k]
    ranktab = _dot(ltri_ref[...], onehot.astype(BF16)) + cnt_scr[0:1, :]
    slab_lane = lax.broadcasted_iota(I32, (t_rows, SLAB), 1)
    slab = jnp.zeros((t_rows, SLAB), I32)
    wslab = jnp.zeros((t_rows, SLAB), F32)
    for k in range(TOP_K):
        rank = jnp.sum(jnp.where(lane == idxs[k], ranktab, 0.0), axis=-1, keepdims=True)
        slab = jnp.where(slab_lane == k, idxs[k].astype(I32), slab)
        slab = jnp.where(slab_lane == TOP_K + k, rank.astype(I32), slab)
        wslab = jnp.where(slab_lane == k, ROUTE_SCALE * sels[k] / total, wslab)
    slab_ref[...] = slab
    wslab_ref[...] = wslab
    cnt_scr[...] = cnt_scr[...] + jnp.sum(onehot, axis=0, keepdims=True)
    cnt_ref[...] = cnt_scr[...]


def _mix_body(n_prompt_tiles, tiles_per_seq, seqs_s, ln_s,
              xp_ref, xsm_ref, modp_ref, mods_ref, h0s_ref, wsp_ref, bsfp_ref, wss_ref, bsfs_ref,
              *rest):
    shared = rest[:19]
    xs1_ref, h2t_ref, slab_ref, wslab_ref, cnt_ref, hfp_ref, hfs_ref, v_ref = rest[19:27]
    scratch = rest[27:]
    outs = (xs1_ref, h2t_ref, slab_ref, wslab_ref, cnt_ref)
    s = pl.program_id(0)

    @pl.when(s == 0)
    def _():
        scratch[2][...] = jnp.zeros_like(scratch[2])

    @pl.when(s < n_prompt_tiles)
    def _():
        _mix_tile(1, SGU_LEN, True, lax.rem(s, tiles_per_seq) == 0, xp_ref, modp_ref, None,
                  wsp_ref, bsfp_ref, shared, outs, hfp_ref, None, scratch)

    @pl.when(s >= n_prompt_tiles)
    def _():
        _mix_tile(seqs_s, ln_s, False, None, xsm_ref, mods_ref, h0s_ref,
                  wss_ref, bsfs_ref, shared, outs, hfs_ref, v_ref, scratch)


def _mix(x_prompt, x_sample, mod, h0_s, ws_p, bsf_p, ws_s, bsf_s, shared):
    bp, sp, _ = x_prompt.shape
    bs, ss, _ = x_sample.shape
    assert sp % MIX_ROWS == 0 and MIX_ROWS % ss == 0 and ss == CHUNK
    seqs_s = MIX_ROWS // ss
    assert bs % seqs_s == 0 and bp % seqs_s == 0
    tiles_per_seq = sp // MIX_ROWS
    n_pt = bp * tiles_per_seq
    n_st = bs // seqs_s
    n_tok = bp * sp + bs * ss

    def p_tile(s):
        return jnp.minimum(s, n_pt - 1)

    def s_tile(s):
        return jnp.maximum(s - n_pt, 0)

    def const(shape):
        nd = len(shape)
        return pl.BlockSpec(shape, lambda s: (0,) * nd)

    in_specs = [
        pl.BlockSpec((1, MIX_ROWS, D_MODEL), lambda s: (p_tile(s) // tiles_per_seq, p_tile(s) % tiles_per_seq, 0)),
        pl.BlockSpec((seqs_s, ss, D_MODEL), lambda s: (s_tile(s), 0, 0)),
        pl.BlockSpec((1, 6, D_MODEL), lambda s: (p_tile(s) // tiles_per_seq, 0, 0)),
        pl.BlockSpec((seqs_s, 6, D_MODEL), lambda s: (bp // seqs_s + s_tile(s), 0, 0)),
        pl.BlockSpec((seqs_s, SUB, STATE_COLS), lambda s: (s_tile(s), 0, 0)),
        const(ws_p.shape), const(bsf_p.shape), const(ws_s.shape), const(bsf_s.shape),
    ] + [const(w.shape) for w in shared]
    out_shape = [
        jax.ShapeDtypeStruct((n_tok, D_MODEL), F32),
        jax.ShapeDtypeStruct((n_tok * ROW_TILES, LANES), F32),
        jax.ShapeDtypeStruct((n_tok, SLAB), I32),
        jax.ShapeDtypeStruct((n_tok, SLAB), F32),
        jax.ShapeDtypeStruct((SUB, N_EXPERTS), F32),
        jax.ShapeDtypeStruct((bp, SUB, STATE_COLS), F32),
        jax.ShapeDtypeStruct((bs, SUB, STATE_COLS), F32),
        jax.ShapeDtypeStruct((bs, ss, D_A), F32),
    ]
    out_specs = [
        pl.BlockSpec((MIX_ROWS, D_MODEL), lambda s: (s, 0)),
        pl.BlockSpec((MIX_ROWS * ROW_TILES, LANES), lambda s: (s, 0)),
        pl.BlockSpec((MIX_ROWS, SLAB), lambda s: (s, 0)),
        pl.BlockSpec((MIX_ROWS, SLAB), lambda s: (s, 0)),
        const((SUB, N_EXPERTS)),
        pl.BlockSpec((1, SUB, STATE_COLS), lambda s: (p_tile(s) // tiles_per_seq, 0, 0)),
        pl.BlockSpec((seqs_s, SUB, STATE_COLS), lambda s: (s_tile(s), 0, 0)),
        pl.BlockSpec((seqs_s, ss, D_A), lambda s: (s_tile(s), 0, 0)),
    ]
    return pl.pallas_call(
        functools.partial(_mix_body, n_pt, tiles_per_seq, seqs_s, ss),
        out_shape=out_shape,
        grid=(n_pt + n_st,),
        in_specs=in_specs,
        out_specs=out_specs,
        scratch_shapes=[pltpu.VMEM((STATE_COLS // LANES, MIX_ROWS, LANES), F32),
                        pltpu.VMEM((SUB, STATE_COLS), F32),
                        pltpu.VMEM((SUB, N_EXPERTS), F32)],
        compiler_params=pltpu.CompilerParams(dimension_semantics=("arbitrary",),
                                             vmem_limit_bytes=VMEM_LIMIT),
        name="mix",
    )(x_prompt, x_sample, mod, mod, h0_s, ws_p, bsf_p, ws_s, bsf_s, *shared)


def _token_rows(t):
    return pl.ds(pl.multiple_of(t * ROW_TILES, ROW_TILES), ROW_TILES)


def _dispatch_body(idx_ref, rank_ref, pstart_ref, pend_ref, h2t_ref, xs_hbm, zero_ref, sem, zsem):
    i = pl.program_id(0)
    blk_rows = MOE_BLK * ROW_TILES
    n_blocks = xs_hbm.shape[0] // blk_rows

    @pl.when(i == 0)
    def _():
        zero_ref[...] = jnp.zeros_like(zero_ref)

        def clear(blk):
            dst = xs_hbm.at[pl.ds(pl.multiple_of(blk * blk_rows, blk_rows), blk_rows)]
            return pltpu.make_async_copy(zero_ref, dst, zsem)

        def each(fn):
            def fill(e, carry):
                @pl.when(pend_ref[e] > pstart_ref[e])
                def _():
                    fn(clear(pend_ref[e] // MOE_BLK - 1))
                return carry
            lax.fori_loop(0, N_EXPERTS, fill, 0)

            def tail(b, carry):
                fn(clear(b))
                return carry
            lax.fori_loop(pend_ref[N_EXPERTS - 1] // MOE_BLK, n_blocks, tail, 0)

        each(lambda cp: cp.start())
        each(lambda cp: cp.wait())

    def issue(t, carry):
        for k in range(TOP_K):
            e = idx_ref[0, 0, t * TOP_K + k]
            dst = pstart_ref[e] + rank_ref[0, 0, t * TOP_K + k]
            pltpu.make_async_copy(h2t_ref.at[_token_rows(t)], xs_hbm.at[_token_rows(dst)],
                                  sem).start(priority=k % 2)
        return carry
    lax.fori_loop(0, TOK_TILE, issue, 0)
    for _ in range(TOP_K):
        pltpu.make_async_copy(h2t_ref, xs_hbm.at[pl.ds(0, TOK_TILE * ROW_TILES)], sem).wait()


def _dispatch(idx, rank, pstart, pend, h2t, n_rows):
    n_tiles = h2t.shape[0] // (TOK_TILE * ROW_TILES)
    smem_tile = pl.BlockSpec((1, 1, TOK_TILE * TOP_K), lambda i: (i, 0, 0), memory_space=pltpu.SMEM)
    smem_all = pl.BlockSpec(memory_space=pltpu.SMEM)
    return pl.pallas_call(
        _dispatch_body,
        out_shape=jax.ShapeDtypeStruct((n_rows * ROW_TILES, LANES), F32),
        grid=(n_tiles,),
        in_specs=[smem_tile, smem_tile, smem_all, smem_all,
                  pl.BlockSpec((TOK_TILE * ROW_TILES, LANES), lambda i: (i, 0))],
        out_specs=pl.BlockSpec(memory_space=pl.ANY),
        scratch_shapes=[pltpu.VMEM((MOE_BLK * ROW_TILES, LANES), F32),
                        pltpu.SemaphoreType.DMA, pltpu.SemaphoreType.DMA],
        compiler_params=pltpu.CompilerParams(dimension_semantics=("arbitrary",)),
        name="moe_dispatch",
    )(idx.reshape(n_tiles, 1, TOK_TILE * TOP_K), rank.reshape(n_tiles, 1, TOK_TILE * TOP_K),
      pstart, pend, h2t)


def _experts_body(bexp_ref, nused_ref, xs_ref, wg_ref, wu_ref, wd_ref, ys_ref):
    used = pl.program_id(0) < nused_ref[0]

    @pl.when(used)
    def _():
        xb = jnp.concatenate([xs_ref[_tile_rows(j, MOE_BLK), :] for j in range(ROW_TILES)],
                             axis=1).astype(BF16)
        gate = _dot(xb, wg_ref[0].astype(BF16))
        up = _dot(xb, wu_ref[0].astype(BF16))
        act = (jax.nn.silu(gate) * up).astype(BF16)
        y = _dot(act, wd_ref[0].astype(BF16))
        for j in range(ROW_TILES):
            ys_ref[_tile_rows(j, MOE_BLK), :] = y[:, j * LANES:(j + 1) * LANES]

    @pl.when(jnp.logical_not(used))
    def _():
        ys_ref[...] = jnp.zeros_like(ys_ref)


def _experts(blk_exp, n_used, xs, w_gate, w_up, w_down):
    blk_rows = MOE_BLK * ROW_TILES
    n_blocks = xs.shape[0] // blk_rows

    def row_map(b, bexp, nused):
        return (jnp.minimum(b, nused[0] - 1), 0)

    def out_map(b, bexp, nused):
        return (b, 0)

    def w_map(b, bexp, nused):
        return (bexp[jnp.minimum(b, nused[0] - 1)], 0, 0)

    return pl.pallas_call(
        _experts_body,
        out_shape=jax.ShapeDtypeStruct(xs.shape, F32),
        grid_spec=pltpu.PrefetchScalarGridSpec(
            num_scalar_prefetch=2,
            grid=(n_blocks,),
            in_specs=[pl.BlockSpec((blk_rows, LANES), row_map),
                      pl.BlockSpec((1, D_MODEL, D_EXPERT), w_map),
                      pl.BlockSpec((1, D_MODEL, D_EXPERT), w_map),
                      pl.BlockSpec((1, D_EXPERT, D_MODEL), w_map)],
            out_specs=pl.BlockSpec((blk_rows, LANES), out_map)),
        compiler_params=pltpu.CompilerParams(dimension_semantics=("arbitrary",),
                                             vmem_limit_bytes=VMEM_LIMIT),
        name="moe_experts",
    )(blk_exp, n_used, xs, w_gate, w_up, w_down)


def _combine_body(n_prompt_tiles, idx_ref, rank_ref, pstart_ref, xs1_ref, g2_ref, wts_ref, fg_ref,
                  ys_hbm, yp_ref, ysm_ref, buf_ref, sem):
    i = pl.program_id(0)

    def issue(t, carry):
        for k in range(TOP_K):
            e = idx_ref[0, 0, t * TOP_K + k]
            src = pstart_ref[e] + rank_ref[0, 0, t * TOP_K + k]
            pltpu.make_async_copy(ys_hbm.at[_token_rows(src)], buf_ref.at[k, _token_rows(t)],
                                  sem).start(priority=k % 2)
        return carry
    lax.fori_loop(0, TOK_TILE, issue, 0)
    for k in range(TOP_K):
        pltpu.make_async_copy(ys_hbm.at[pl.ds(0, TOK_TILE * ROW_TILES)], buf_ref.at[k], sem).wait()
    wts = wts_ref[...]
    parts = []
    for j in range(ROW_TILES):
        acc = wts[:, 0:1] * buf_ref[0, _tile_rows(j, TOK_TILE), :]
        for k in range(1, TOP_K):
            acc = acc + wts[:, k:k + 1] * buf_ref[k, _tile_rows(j, TOK_TILE), :]
        parts.append(acc)
    seg = TOK_TILE // g2_ref.shape[0]
    g2 = jnp.concatenate([jnp.broadcast_to(g2_ref[s], (seg, D_MODEL)) for s in range(g2_ref.shape[0])],
                         axis=0)
    x = xs1_ref[...] + g2 * jnp.concatenate(parts, axis=1)
    y = (x * _rms(x)) * fg_ref[...]

    @pl.when(i < n_prompt_tiles)
    def _():
        yp_ref[...] = y

    @pl.when(i >= n_prompt_tiles)
    def _():
        ysm_ref[...] = y


def _combine(idx, rank, pstart, xs1, g2_blocks, wts, final_g, ys, n_p):
    n_tok = xs1.shape[0]
    n_tiles = n_tok // TOK_TILE
    n_pt = n_p // TOK_TILE
    segs = TOK_TILE // CHUNK
    smem_tile = pl.BlockSpec((1, 1, TOK_TILE * TOP_K), lambda i: (i, 0, 0), memory_space=pltpu.SMEM)
    return pl.pallas_call(
        functools.partial(_combine_body, n_pt),
        out_shape=[jax.ShapeDtypeStruct((n_p, D_MODEL), F32),
                   jax.ShapeDtypeStruct((n_tok - n_p, D_MODEL), F32)],
        grid=(n_tiles,),
        in_specs=[smem_tile, smem_tile, pl.BlockSpec(memory_space=pltpu.SMEM),
                  pl.BlockSpec((TOK_TILE, D_MODEL), lambda i: (i, 0)),
                  pl.BlockSpec((segs, 1, D_MODEL), lambda i: (i, 0, 0)),
                  pl.BlockSpec((TOK_TILE, TOP_K), lambda i: (i, 0)),
                  pl.BlockSpec((1, D_MODEL), lambda i: (0, 0)),
                  pl.BlockSpec(memory_space=pl.ANY)],
        out_specs=[pl.BlockSpec((TOK_TILE, D_MODEL), lambda i: (jnp.minimum(i, n_pt - 1), 0)),
                   pl.BlockSpec((TOK_TILE, D_MODEL), lambda i: (jnp.maximum(i - n_pt, 0), 0))],
        scratch_shapes=[pltpu.VMEM((TOP_K, TOK_TILE * ROW_TILES, LANES), F32),
                        pltpu.SemaphoreType.DMA],
        compiler_params=pltpu.CompilerParams(dimension_semantics=("arbitrary",),
                                             vmem_limit_bytes=VMEM_LIMIT),
        name="moe_combine",
    )(idx.reshape(n_tiles, 1, TOK_TILE * TOP_K), rank.reshape(n_tiles, 1, TOK_TILE * TOP_K),
      pstart, xs1, g2_blocks, wts, final_g.reshape(1, D_MODEL), ys)


def kernel(x_prompt, x_sample, c_prompt, c_sample, state_ssm_re, state_ssm_im, norm1_g, norm2_g,
           w_ada, b_ada, w_in, w_s, b_s, g_v, lam_re, lam_im, log_dt, b_re, b_im, c_re, c_im,
           d_skip, w_glu, b_glu, out_g_a, out_g_b, w_out, w_router, router_bias, w_gate, w_up,
           w_down, ws_gate, ws_up, ws_down, final_g):
    assert norm1_g.shape[0] == 1
    bp, sp, _ = x_prompt.shape
    bs, ss, _ = x_sample.shape
    n_p, n_s = bp * sp, bs * ss
    n_tok = n_p + n_s
    l = 0

    mod = _adaln(jnp.concatenate([c_prompt, c_sample], axis=0), w_ada[l], b_ada[l])
    mod = mod.reshape(bp + bs, 6, D_MODEL)

    wb, wc, tabs = _s5_tables(lam_re[l], lam_im[l], log_dt[l], b_re[l], b_im[l], c_re[l], c_im[l])
    pos = jnp.arange(SGU_LEN)
    mask = (pos[:, None] // CHUNK) >= (pos[None, :] // CHUNK)
    ws_masked = jnp.where(mask[None], w_s[l], 0.0)
    row = lax.broadcasted_iota(I32, (MIX_ROWS, MIX_ROWS), 0)
    col = lax.broadcasted_iota(I32, (MIX_ROWS, MIX_ROWS), 1)
    ltri = (col < row).astype(BF16)

    def sgu_weights(ln):
        bsf = jnp.repeat(b_s[l][:, :ln].T, HEAD_A, axis=1)
        return ws_masked[:, :ln, :ln].astype(BF16), bsf

    ws_p, bsf_p = sgu_weights(SGU_LEN)
    ws_s, bsf_s = sgu_weights(ss)
    shared = [norm1_g[l].reshape(1, D_MODEL), w_in[l].astype(BF16), g_v[l].reshape(1, D_A),
              wb, wc, tabs, d_skip[l].reshape(1, D_B), w_glu[l].astype(BF16),
              b_glu[l].reshape(1, D_B), out_g_a[l].reshape(1, D_A), out_g_b[l].reshape(1, D_B),
              w_out[l].astype(BF16), norm2_g[l].reshape(1, D_MODEL), w_router[l].astype(BF16),
              router_bias[l].reshape(1, N_EXPERTS), ws_gate[l].astype(BF16),
              ws_up[l].astype(BF16), ws_down[l].astype(BF16), ltri]
    h0_s = jnp.broadcast_to(_lay(state_ssm_re[l], state_ssm_im[l])[:, None, :], (bs, SUB, STATE_COLS))
    xs1, h2t, slab, wslab, cnt_all, hfin_p, hfin_s, v_rows = _mix(
        x_prompt, x_sample, mod, h0_s, ws_p, bsf_p, ws_s, bsf_s, shared)

    idx = slab[:, :TOP_K]
    rank = slab[:, TOP_K:2 * TOP_K]
    wts = wslab[:, :TOP_K]
    counts = cnt_all[0].astype(I32)
    padded = (counts + MOE_BLK - 1) // MOE_BLK * MOE_BLK
    pend = jnp.cumsum(padded).astype(I32)
    pstart = pend - padded
    n_blocks = -(-n_tok * TOP_K // MOE_BLK) + N_EXPERTS
    n_used = (pend[-1:] // MOE_BLK).astype(I32)
    blk_start = jnp.arange(n_blocks, dtype=I32) * MOE_BLK
    blk_exp = jnp.sum((pend[None, :] <= blk_start[:, None]).astype(I32), axis=1)
    blk_exp = jnp.minimum(blk_exp, N_EXPERTS - 1)

    xs = _dispatch(idx, rank, pstart, pend, h2t, n_blocks * MOE_BLK)
    ys = _experts(blk_exp, n_used, xs, w_gate[l], w_up[l], w_down[l])

    g2 = mod[:, 5, :]
    g2_blocks = jnp.concatenate([jnp.repeat(g2[:bp], sp // CHUNK, axis=0),
                                 jnp.repeat(g2[bp:], ss // CHUNK, axis=0)], axis=0)
    y_p, y_s = _combine(idx, rank, pstart, xs1, g2_blocks.reshape(n_tok // CHUNK, 1, D_MODEL), wts,
                        final_g, ys, n_p)

    re_p, im_p = _unlay(hfin_p[:, 0, :])
    re_s, im_s = _unlay(hfin_s[:, 0, :])
    return (y_p.reshape(bp, sp, D_MODEL), y_s.reshape(bs, ss, D_MODEL),
            re_p[None], im_p[None], re_s[None], im_s[None], v_rows[None])
```

```python
import functools

import jax
import jax.numpy as jnp
from jax import lax
from jax.experimental import pallas as pl
from jax.experimental.pallas import tpu as pltpu

F32 = jnp.float32
BF16 = jnp.bfloat16
I32 = jnp.int32

D_MODEL = 1024
D_A = 512
D_B = 512
N_HEADS_A = 4
HEAD_A = 128
SSM_GROUP = 16
N_GROUPS_B = 32
SSM_STATE = 64
N_EXPERTS = 256
TOP_K = 8
D_EXPERT = 256
ROUTE_SCALE = 2.5
CHUNK = 64
SGU_LEN = 128
EPS = 1e-6

STATE_CHUNKS = 4
GROUPS_PER_CHUNK = N_GROUPS_B // STATE_CHUNKS
HALF = GROUPS_PER_CHUNK * SSM_STATE
STATE_COLS = STATE_CHUNKS * 2 * HALF
SCAN_W = 256
SUB = 8
LANES = 128
ROW_TILES = D_MODEL // LANES

TAB_PW = 0
TAB_MD = 8
TAB_P8 = 11

MIX_ROWS = 256
MOE_BLK = 256
TOK_TILE = 256
SLAB = 128
VMEM_LIMIT = 56 * 1024 * 1024


def _dot(a, b):
    return jnp.dot(a, b, preferred_element_type=F32)


def _rms(x):
    return lax.rsqrt(jnp.mean(x * x, axis=-1, keepdims=True) + EPS)


def _tile_rows(j, n):
    return pl.ds(j, n, stride=ROW_TILES)


def _adaln_body(c_ref, w_ref, b_ref, o_ref):
    c = c_ref[...]
    o_ref[...] = _dot(jax.nn.silu(c).astype(BF16), w_ref[...].astype(BF16)) + b_ref[...]


def _adaln(c, w_ada, b_ada):
    n = c.shape[0]
    cols = w_ada.shape[1]
    blk = 1536
    return pl.pallas_call(
        _adaln_body,
        out_shape=jax.ShapeDtypeStruct((n, cols), F32),
        grid=(cols // blk,),
        in_specs=[pl.BlockSpec((n, D_MODEL), lambda j: (0, 0)),
                  pl.BlockSpec((D_MODEL, blk), lambda j: (0, j)),
                  pl.BlockSpec((1, blk), lambda j: (0, j))],
        out_specs=pl.BlockSpec((n, blk), lambda j: (0, j)),
        compiler_params=pltpu.CompilerParams(dimension_semantics=("arbitrary",)),
        name="adaln",
    )(c, w_ada, b_ada.reshape(1, cols))


def _lay(re, im):
    lead = re.shape[:-2]
    re = re.reshape(lead + (STATE_CHUNKS, HALF))
    im = im.reshape(lead + (STATE_CHUNKS, HALF))
    return jnp.concatenate([re, im], axis=-1).reshape(lead + (STATE_COLS,))


def _unlay(v):
    lead = v.shape[:-1]
    v = v.reshape(lead + (STATE_CHUNKS, 2, HALF))
    re = v[..., 0, :].reshape(lead + (N_GROUPS_B, SSM_STATE))
    im = v[..., 1, :].reshape(lead + (N_GROUPS_B, SSM_STATE))
    return re, im


def _s5_tables(lam_re, lam_im, log_dt, b_re, b_im, c_re, c_im):
    dt = jnp.exp(log_dt.astype(F32))[:, None]
    lr, li = lam_re.astype(F32), lam_im.astype(F32)

    def apow(k):
        mag = jnp.exp(lr * dt * k)
        return mag * jnp.cos(li * dt * k), mag * jnp.sin(li * dt * k)

    ar, ai = apow(1.0)
    den = lr * lr + li * li
    nr, ni = ar - 1.0, ai
    kr, ki = (nr * lr + ni * li) / den, (ni * lr - nr * li) / den
    br, bi = b_re.astype(F32), b_im.astype(F32)
    bbr = kr[..., None] * br - ki[..., None] * bi
    bbi = kr[..., None] * bi + ki[..., None] * br
    eye = jnp.eye(GROUPS_PER_CHUNK, dtype=F32)

    def bproj(bb):
        bb = bb.reshape(STATE_CHUNKS, GROUPS_PER_CHUNK, SSM_STATE, SSM_GROUP)
        w = jnp.einsum("mgph,gk->mghkp", bb, eye)
        return w.reshape(STATE_CHUNKS, GROUPS_PER_CHUNK * SSM_GROUP, HALF)

    wb = jnp.concatenate([bproj(bbr), bproj(bbi)], axis=-1).astype(BF16)

    def cproj(cc):
        cc = cc.reshape(STATE_CHUNKS, GROUPS_PER_CHUNK, SSM_GROUP, SSM_STATE)
        w = jnp.einsum("mghp,gk->mgpkh", cc, eye)
        return w.reshape(STATE_CHUNKS, HALF, GROUPS_PER_CHUNK * SSM_GROUP)

    wc = jnp.concatenate([cproj(c_re.astype(F32)), cproj(-c_im.astype(F32))], axis=1).astype(BF16)

    rows = jnp.arange(SUB, dtype=F32)
    tabs = []
    for i in range(SUB):
        pr, pi = apow(float(i + 1))
        tabs.append(jnp.broadcast_to(_lay(pr, pi)[None], (SUB, STATE_COLS)))
    for d in (1, 2, 4):
        pr, pi = apow(float(SUB * d))
        keep = (rows >= d).astype(F32)[:, None]
        tabs.append(_lay(pr, pi)[None] * keep)
    pr, pi = apow(SUB * rows[:, None, None])
    tabs.append(_lay(pr, pi))
    return wb, wc, jnp.stack(tabs)


def _cmul(ar, ai, br, bi):
    return ar * br - ai * bi, ar * bi + ai * br


def _s5_scan_block(bu_ref, row0, h_ref, tab_ref):
    row_id = lax.broadcasted_iota(I32, (SUB, SCAN_W), 0)
    tiles = SCAN_W // LANES
    for m in range(STATE_CHUNKS):
        for hf in range(HALF // SCAN_W):
            c_re0 = m * 2 * HALF + hf * SCAN_W
            c_im0 = c_re0 + HALF
            cre = pl.ds(c_re0, SCAN_W)
            cim = pl.ds(c_im0, SCAN_W)

            def tab(slot):
                return tab_ref[slot, :, cre], tab_ref[slot, :, cim]

            def load(i, c0):
                rows = pl.ds(row0 + i, SUB, stride=SUB)
                return jnp.concatenate([bu_ref[c0 // LANES + j, rows, :] for j in range(tiles)], axis=1)

            def store(i, c0, val):
                rows = pl.ds(row0 + i, SUB, stride=SUB)
                for j in range(tiles):
                    bu_ref[c0 // LANES + j, rows, :] = val[:, j * LANES:(j + 1) * LANES]

            a_re, a_im = tab(TAB_PW)
            s_re = load(0, c_re0)
            s_im = load(0, c_im0)
            loc = [(s_re, s_im)]
            for i in range(1, SUB):
                p_re, p_im = _cmul(a_re, a_im, s_re, s_im)
                s_re = p_re + load(i, c_re0)
                s_im = p_im + load(i, c_im0)
                loc.append((s_re, s_im))
            e_re, e_im = s_re, s_im
            for n, d in enumerate((1, 2, 4)):
                m_re, m_im = tab(TAB_MD + n)
                q_re, q_im = _cmul(m_re, m_im, pltpu.roll(e_re, d, 0), pltpu.roll(e_im, d, 0))
                e_re, e_im = e_re + q_re, e_im + q_im
            p8_re, p8_im = tab(TAB_P8)
            c_re, c_im = _cmul(p8_re, p8_im, h_ref[:, cre], h_ref[:, cim])
            c_re = c_re + jnp.where(row_id >= 1, pltpu.roll(e_re, 1, 0), 0.0)
            c_im = c_im + jnp.where(row_id >= 1, pltpu.roll(e_im, 1, 0), 0.0)
            for i in range(SUB):
                w_re, w_im = tab(TAB_PW + i)
                q_re, q_im = _cmul(w_re, w_im, c_re, c_im)
                f_re, f_im = loc[i][0] + q_re, loc[i][1] + q_im
                store(i, c_re0, f_re)
                store(i, c_im0, f_im)
            h_ref[:, cre] = jnp.broadcast_to(f_re[SUB - 1:SUB, :], (SUB, SCAN_W))
            h_ref[:, cim] = jnp.broadcast_to(f_im[SUB - 1:SUB, :], (SUB, SCAN_W))


def _mix_tile(seqs, ln, chain, new_seq, x_ref, mod_ref, h0_ref, ws_ref, bsf_ref, shared, outs,
              hfin_ref, v_ref, scratch):
    (n1g_ref, win_ref, gv_ref, wb_ref, wc_ref, tab_ref, dsk_ref, wglu_ref, bglu_ref, oga_ref,
     ogb_ref, wout_ref, n2g_ref, wr_ref, rb_ref, wsg_ref, wsu_ref, wsd_ref, ltri_ref) = shared
    xs1_ref, h2t_ref, slab_ref, wslab_ref, cnt_ref = outs
    bu_ref, h_scr, cnt_scr = scratch
    t_rows = MIX_ROWS
    rows_per_seq = t_rows // seqs

    x = x_ref[...].reshape(t_rows, D_MODEL)

    def modrow(j):
        parts = [jnp.broadcast_to(mod_ref[s, j:j + 1, :], (rows_per_seq, D_MODEL)) for s in range(seqs)]
        return parts[0] if seqs == 1 else jnp.concatenate(parts, axis=0)

    h = (x * _rms(x)) * n1g_ref[...] * (1.0 + modrow(1)) + modrow(0)
    z = _dot(h.astype(BF16), win_ref[...])
    u_a = z[:, :D_A]
    v_a = z[:, D_A:2 * D_A]
    u_b = z[:, 2 * D_A:]

    vh_parts = []
    for hh in range(N_HEADS_A):
        cols = slice(hh * HEAD_A, (hh + 1) * HEAD_A)
        vv = v_a[:, cols]
        vh_parts.append((vv * _rms(vv)) * gv_ref[:, cols])
    if v_ref is not None:
        v_ref[...] = jnp.concatenate(vh_parts, axis=1).reshape(seqs, rows_per_seq, D_A)
    n_sgu = t_rows // ln
    s_parts = []
    for hh in range(N_HEADS_A):
        vb = vh_parts[hh].astype(BF16)
        w_h = ws_ref[hh]
        s_parts.append(jnp.concatenate(
            [_dot(w_h, vb[c * ln:(c + 1) * ln, :]) for c in range(n_sgu)], axis=0))
    bsf = bsf_ref[...]
    s_mix = jnp.concatenate(s_parts, axis=1) + jnp.concatenate([bsf] * n_sgu, axis=0)
    y_a = u_a * s_mix

    ub16 = u_b.astype(BF16)
    gc = GROUPS_PER_CHUNK * SSM_GROUP
    tiles_per_chunk = 2 * HALF // LANES
    for m in range(STATE_CHUNKS):
        bu = _dot(ub16[:, m * gc:(m + 1) * gc], wb_ref[m])
        for j in range(tiles_per_chunk):
            bu_ref[m * tiles_per_chunk + j] = bu[:, j * LANES:(j + 1) * LANES]
    n_blk = t_rows // CHUNK
    if chain:
        @pl.when(new_seq)
        def _():
            h_scr[...] = jnp.zeros_like(h_scr)
    for blk in range(n_blk):
        if not chain:
            h_scr[...] = h0_ref[blk]
        _s5_scan_block(bu_ref, blk * CHUNK, h_scr, tab_ref)
        if not chain:
            hfin_ref[blk] = h_scr[...]
    if chain:
        hfin_ref[0] = h_scr[...]
    y_parts = []
    for m in range(STATE_CHUNKS):
        st = jnp.concatenate([bu_ref[m * tiles_per_chunk + j] for j in range(tiles_per_chunk)], axis=1)
        y_parts.append(_dot(st.astype(BF16), wc_ref[m]))
    y_s = jnp.concatenate(y_parts, axis=1) + dsk_ref[...] * u_b
    g_b = jax.nn.gelu(y_s)
    y_b = g_b * jax.nn.sigmoid(_dot(g_b.astype(BF16), wglu_ref[...]) + bglu_ref[...])

    na = (y_a * _rms(y_a)) * oga_ref[...]
    nb = (y_b * _rms(y_b)) * ogb_ref[...]
    mix = _dot(jnp.concatenate([na, nb], axis=1).astype(BF16), wout_ref[...])
    x1 = x + modrow(2) * mix

    h2 = (x1 * _rms(x1)) * n2g_ref[...] * (1.0 + modrow(4)) + modrow(3)
    for j in range(ROW_TILES):
        h2t_ref[_tile_rows(j, t_rows), :] = h2[:, j * LANES:(j + 1) * LANES]
    h2b = h2.astype(BF16)
    act = jax.nn.silu(_dot(h2b, wsg_ref[...])) * _dot(h2b, wsu_ref[...])
    shared_out = _dot(act.astype(BF16), wsd_ref[...])
    xs1_ref[...] = x1 + modrow(5) * shared_out

    scores = jax.nn.sigmoid(_dot(h2b, wr_ref[...]))
    lane = lax.broadcasted_iota(I32, (t_rows, N_EXPERTS), 1).astype(F32)
    work = scores + rb_ref[...]
    onehot = jnp.zeros((t_rows, N_EXPERTS), F32)
    idxs, sels = [], []
    for _ in range(TOP_K):
        top = jnp.max(work, axis=-1, keepdims=True)
        idx = jnp.min(jnp.where(work == top, lane, float(N_EXPERTS)), axis=-1, keepdims=True)
        pick = lane == idx
        sels.append(jnp.sum(jnp.where(pick, scores, 0.0), axis=-1, keepdims=True))
        idxs.append(idx)
        work = jnp.where(pick, -jnp.inf, work)
        onehot = jnp.where(pick, 1.0, onehot)
    total = sels[0]
    for k in range(1, TOP_K):
        total = total + sels[k]
    ranktab = _dot(ltri_ref[...], onehot.astype(BF16)) + cnt_scr[0:1, :]
    slab_lane = lax.broadcasted_iota(I32, (t_rows, SLAB), 1)
    slab = jnp.zeros((t_rows, SLAB), I32)
    wslab = jnp.zeros((t_rows, SLAB), F32)
    for k in range(TOP_K):
        rank = jnp.sum(jnp.where(lane == idxs[k], ranktab, 0.0), axis=-1, keepdims=True)
        slab = jnp.where(slab_lane == k, idxs[k].astype(I32), slab)
        slab = jnp.where(slab_lane == TOP_K + k, rank.astype(I32), slab)
        wslab = jnp.where(slab_lane == k, ROUTE_SCALE * sels[k] / total, wslab)
    slab_ref[...] = slab
    wslab_ref[...] = wslab
    cnt_scr[...] = cnt_scr[...] + jnp.sum(onehot, axis=0, keepdims=True)
    cnt_ref[...] = cnt_scr[...]


def _mix_body(n_prompt_tiles, tiles_per_seq, seqs_s, ln_s,
              xp_ref, xsm_ref, modp_ref, mods_ref, h0s_ref, wsp_ref, bsfp_ref, wss_ref, bsfs_ref,
              *rest):
    shared = rest[:19]
    xs1_ref, h2t_ref, slab_ref, wslab_ref, cnt_ref, hfp_ref, hfs_ref, v_ref = rest[19:27]
    scratch = rest[27:]
    outs = (xs1_ref, h2t_ref, slab_ref, wslab_ref, cnt_ref)
    s = pl.program_id(0)

    @pl.when(s == 0)
    def _():
        scratch[2][...] = jnp.zeros_like(scratch[2])

    @pl.when(s < n_prompt_tiles)
    def _():
        _mix_tile(1, SGU_LEN, True, lax.rem(s, tiles_per_seq) == 0, xp_ref, modp_ref, None,
                  wsp_ref, bsfp_ref, shared, outs, hfp_ref, None, scratch)

    @pl.when(s >= n_prompt_tiles)
    def _():
        _mix_tile(seqs_s, ln_s, False, None, xsm_ref, mods_ref, h0s_ref,
                  wss_ref, bsfs_ref, shared, outs, hfs_ref, v_ref, scratch)


def _mix(x_prompt, x_sample, mod, h0_s, ws_p, bsf_p, ws_s, bsf_s, shared):
    bp, sp, _ = x_prompt.shape
    bs, ss, _ = x_sample.shape
    assert sp % MIX_ROWS == 0 and MIX_ROWS % ss == 0 and ss == CHUNK
    seqs_s = MIX_ROWS // ss
    assert bs % seqs_s == 0 and bp % seqs_s == 0
    tiles_per_seq = sp // MIX_ROWS
    n_pt = bp * tiles_per_seq
    n_st = bs // seqs_s
    n_tok = bp * sp + bs * ss

    def p_tile(s):
        return jnp.minimum(s, n_pt - 1)

    def s_tile(s):
        return jnp.maximum(s - n_pt, 0)

    def const(shape):
        nd = len(shape)
        return pl.BlockSpec(shape, lambda s: (0,) * nd)

    in_specs = [
        pl.BlockSpec((1, MIX_ROWS, D_MODEL), lambda s: (p_tile(s) // tiles_per_seq, p_tile(s) % tiles_per_seq, 0)),
        pl.BlockSpec((seqs_s, ss, D_MODEL), lambda s: (s_tile(s), 0, 0)),
        pl.BlockSpec((1, 6, D_MODEL), lambda s: (p_tile(s) // tiles_per_seq, 0, 0)),
        pl.BlockSpec((seqs_s, 6, D_MODEL), lambda s: (bp // seqs_s + s_tile(s), 0, 0)),
        pl.BlockSpec((seqs_s, SUB, STATE_COLS), lambda s: (s_tile(s), 0, 0)),
        const(ws_p.shape), const(bsf_p.shape), const(ws_s.shape), const(bsf_s.shape),
    ] + [const(w.shape) for w in shared]
    out_shape = [
        jax.ShapeDtypeStruct((n_tok, D_MODEL), F32),
        jax.ShapeDtypeStruct((n_tok * ROW_TILES, LANES), F32),
        jax.ShapeDtypeStruct((n_tok, SLAB), I32),
        jax.ShapeDtypeStruct((n_tok, SLAB), F32),
        jax.ShapeDtypeStruct((SUB, N_EXPERTS), F32),
        jax.ShapeDtypeStruct((bp, SUB, STATE_COLS), F32),
        jax.ShapeDtypeStruct((bs, SUB, STATE_COLS), F32),
        jax.ShapeDtypeStruct((bs, ss, D_A), F32),
    ]
    out_specs = [
        pl.BlockSpec((MIX_ROWS, D_MODEL), lambda s: (s, 0)),
        pl.BlockSpec((MIX_ROWS * ROW_TILES, LANES), lambda s: (s, 0)),
        pl.BlockSpec((MIX_ROWS, SLAB), lambda s: (s, 0)),
        pl.BlockSpec((MIX_ROWS, SLAB), lambda s: (s, 0)),
        const((SUB, N_EXPERTS)),
        pl.BlockSpec((1, SUB, STATE_COLS), lambda s: (p_tile(s) // tiles_per_seq, 0, 0)),
        pl.BlockSpec((seqs_s, SUB, STATE_COLS), lambda s: (s_tile(s), 0, 0)),
        pl.BlockSpec((seqs_s, ss, D_A), lambda s: (s_tile(s), 0, 0)),
    ]
    return pl.pallas_call(
        functools.partial(_mix_body, n_pt, tiles_per_seq, seqs_s, ss),
        out_shape=out_shape,
        grid=(n_pt + n_st,),
        in_specs=in_specs,
        out_specs=out_specs,
        scratch_shapes=[pltpu.VMEM((STATE_COLS // LANES, MIX_ROWS, LANES), F32),
                        pltpu.VMEM((SUB, STATE_COLS), F32),
                        pltpu.VMEM((SUB, N_EXPERTS), F32)],
        compiler_params=pltpu.CompilerParams(dimension_semantics=("arbitrary",),
                                             vmem_limit_bytes=VMEM_LIMIT),
        name="mix",
    )(x_prompt, x_sample, mod, mod, h0_s, ws_p, bsf_p, ws_s, bsf_s, *shared)


def _token_rows(t):
    return pl.ds(pl.multiple_of(t * ROW_TILES, ROW_TILES), ROW_TILES)


def _dispatch_body(dst_ref, pstart_ref, pend_ref, h2t_ref, xs_hbm, zero_ref, sem, zsem):
    i = pl.program_id(0)
    blk_rows = MOE_BLK * ROW_TILES
    n_blocks = xs_hbm.shape[0] // blk_rows

    @pl.when(i == 0)
    def _():
        zero_ref[...] = jnp.zeros_like(zero_ref)

        def clear(blk):
            dst = xs_hbm.at[pl.ds(pl.multiple_of(blk * blk_rows, blk_rows), blk_rows)]
            return pltpu.make_async_copy(zero_ref, dst, zsem)

        def each(fn):
            def fill(e, carry):
                @pl.when(pend_ref[e] > pstart_ref[e])
                def _():
                    fn(clear(pend_ref[e] // MOE_BLK - 1))
                return carry
            lax.fori_loop(0, N_EXPERTS, fill, 0)

            def tail(b, carry):
                fn(clear(b))
                return carry
            lax.fori_loop(pend_ref[N_EXPERTS - 1] // MOE_BLK, n_blocks, tail, 0)

        each(lambda cp: cp.start())
        each(lambda cp: cp.wait())

    def issue(t, carry):
        for k in range(TOP_K):
            dst = dst_ref[0, 0, t * TOP_K + k]
            pltpu.make_async_copy(h2t_ref.at[_token_rows(t)], xs_hbm.at[_token_rows(dst)],
                                  sem).start(priority=k % 2)
        return carry
    lax.fori_loop(0, TOK_TILE, issue, 0)
    for _ in range(TOP_K):
        pltpu.make_async_copy(h2t_ref, xs_hbm.at[pl.ds(0, TOK_TILE * ROW_TILES)], sem).wait()


def _dispatch(dst, pstart, pend, h2t, n_rows):
    n_tiles = h2t.shape[0] // (TOK_TILE * ROW_TILES)
    smem_tile = pl.BlockSpec((1, 1, TOK_TILE * TOP_K), lambda i: (i, 0, 0), memory_space=pltpu.SMEM)
    smem_all = pl.BlockSpec(memory_space=pltpu.SMEM)
    return pl.pallas_call(
        _dispatch_body,
        out_shape=jax.ShapeDtypeStruct((n_rows * ROW_TILES, LANES), F32),
        grid=(n_tiles,),
        in_specs=[smem_tile, smem_all, smem_all,
                  pl.BlockSpec((TOK_TILE * ROW_TILES, LANES), lambda i: (i, 0))],
        out_specs=pl.BlockSpec(memory_space=pl.ANY),
        scratch_shapes=[pltpu.VMEM((MOE_BLK * ROW_TILES, LANES), F32),
                        pltpu.SemaphoreType.DMA, pltpu.SemaphoreType.DMA],
        compiler_params=pltpu.CompilerParams(dimension_semantics=("arbitrary",)),
        name="moe_dispatch",
    )(dst.reshape(n_tiles, 1, TOK_TILE * TOP_K), pstart, pend, h2t)


def _experts_body(bexp_ref, nused_ref, xs_ref, wg_ref, wu_ref, wd_ref, ys_ref, wgu_scr, wd_scr):
    b = pl.program_id(0)
    used = b < nused_ref[0]
    new_expert = jnp.logical_or(b == 0, bexp_ref[b] != bexp_ref[jnp.maximum(b - 1, 0)])

    @pl.when(jnp.logical_and(used, new_expert))
    def _():
        wgu_scr[:, :D_EXPERT] = wg_ref[0].astype(BF16)
        wgu_scr[:, D_EXPERT:] = wu_ref[0].astype(BF16)
        wd_scr[...] = wd_ref[0].astype(BF16)

    @pl.when(used)
    def _():
        xb = jnp.concatenate([xs_ref[_tile_rows(j, MOE_BLK), :] for j in range(ROW_TILES)],
                             axis=1).astype(BF16)
        gu = _dot(xb, wgu_scr[...])
        act = (jax.nn.silu(gu[:, :D_EXPERT]) * gu[:, D_EXPERT:]).astype(BF16)
        y = _dot(act, wd_scr[...])
        for j in range(ROW_TILES):
            ys_ref[_tile_rows(j, MOE_BLK), :] = y[:, j * LANES:(j + 1) * LANES]

    @pl.when(jnp.logical_not(used))
    def _():
        ys_ref[...] = jnp.zeros_like(ys_ref)


def _experts(blk_exp, n_used, xs, w_gate, w_up, w_down):
    blk_rows = MOE_BLK * ROW_TILES
    n_blocks = xs.shape[0] // blk_rows

    def row_map(b, bexp, nused):
        return (jnp.minimum(b, nused[0] - 1), 0)

    def out_map(b, bexp, nused):
        return (b, 0)

    def w_map(b, bexp, nused):
        return (bexp[jnp.minimum(b, nused[0] - 1)], 0, 0)

    return pl.pallas_call(
        _experts_body,
        out_shape=jax.ShapeDtypeStruct(xs.shape, F32),
        grid_spec=pltpu.PrefetchScalarGridSpec(
            num_scalar_prefetch=2,
            grid=(n_blocks,),
            in_specs=[pl.BlockSpec((blk_rows, LANES), row_map),
                      pl.BlockSpec((1, D_MODEL, D_EXPERT), w_map),
                      pl.BlockSpec((1, D_MODEL, D_EXPERT), w_map),
                      pl.BlockSpec((1, D_EXPERT, D_MODEL), w_map)],
            out_specs=pl.BlockSpec((blk_rows, LANES), out_map),
            scratch_shapes=[pltpu.VMEM((D_MODEL, 2 * D_EXPERT), BF16),
                            pltpu.VMEM((D_EXPERT, D_MODEL), BF16)]),
        compiler_params=pltpu.CompilerParams(dimension_semantics=("arbitrary",),
                                             vmem_limit_bytes=VMEM_LIMIT),
        name="moe_experts",
    )(blk_exp, n_used, xs, w_gate, w_up, w_down)


def _combine_body(n_prompt_tiles, dst_ref, xs1_ref, g2_ref, wts_ref, fg_ref,
                  ys_hbm, yp_ref, ysm_ref, buf_ref, sem):
    i = pl.program_id(0)

    def issue(t, carry):
        for k in range(TOP_K):
            src = dst_ref[0, 0, t * TOP_K + k]
            pltpu.make_async_copy(ys_hbm.at[_token_rows(src)], buf_ref.at[k, _token_rows(t)],
                                  sem).start(priority=k % 2)
        return carry
    lax.fori_loop(0, TOK_TILE, issue, 0)
    for k in range(TOP_K):
        pltpu.make_async_copy(ys_hbm.at[pl.ds(0, TOK_TILE * ROW_TILES)], buf_ref.at[k], sem).wait()
    wts = wts_ref[...]
    parts = []
    for j in range(ROW_TILES):
        acc = wts[:, 0:1] * buf_ref[0, _tile_rows(j, TOK_TILE), :]
        for k in range(1, TOP_K):
            acc = acc + wts[:, k:k + 1] * buf_ref[k, _tile_rows(j, TOK_TILE), :]
        parts.append(acc)
    seg = TOK_TILE // g2_ref.shape[0]
    g2 = jnp.concatenate([jnp.broadcast_to(g2_ref[s], (seg, D_MODEL)) for s in range(g2_ref.shape[0])],
                         axis=0)
    x = xs1_ref[...] + g2 * jnp.concatenate(parts, axis=1)
    y = (x * _rms(x)) * fg_ref[...]

    @pl.when(i < n_prompt_tiles)
    def _():
        yp_ref[...] = y

    @pl.when(i >= n_prompt_tiles)
    def _():
        ysm_ref[...] = y


def _combine(dst, xs1, g2_blocks, wts, final_g, ys, n_p):
    n_tok = xs1.shape[0]
    n_tiles = n_tok // TOK_TILE
    n_pt = n_p // TOK_TILE
    segs = TOK_TILE // CHUNK
    smem_tile = pl.BlockSpec((1, 1, TOK_TILE * TOP_K), lambda i: (i, 0, 0), memory_space=pltpu.SMEM)
    return pl.pallas_call(
        functools.partial(_combine_body, n_pt),
        out_shape=[jax.ShapeDtypeStruct((n_p, D_MODEL), F32),
                   jax.ShapeDtypeStruct((n_tok - n_p, D_MODEL), F32)],
        grid=(n_tiles,),
        in_specs=[smem_tile,
                  pl.BlockSpec((TOK_TILE, D_MODEL), lambda i: (i, 0)),
                  pl.BlockSpec((segs, 1, D_MODEL), lambda i: (i, 0, 0)),
                  pl.BlockSpec((TOK_TILE, TOP_K), lambda i: (i, 0)),
                  pl.BlockSpec((1, D_MODEL), lambda i: (0, 0)),
                  pl.BlockSpec(memory_space=pl.ANY)],
        out_specs=[pl.BlockSpec((TOK_TILE, D_MODEL), lambda i: (jnp.minimum(i, n_pt - 1), 0)),
                   pl.BlockSpec((TOK_TILE, D_MODEL), lambda i: (jnp.maximum(i - n_pt, 0), 0))],
        scratch_shapes=[pltpu.VMEM((TOP_K, TOK_TILE * ROW_TILES, LANES), F32),
                        pltpu.SemaphoreType.DMA],
        compiler_params=pltpu.CompilerParams(dimension_semantics=("arbitrary",),
                                             vmem_limit_bytes=VMEM_LIMIT),
        name="moe_combine",
    )(dst.reshape(n_tiles, 1, TOK_TILE * TOP_K), xs1, g2_blocks, wts, final_g.reshape(1, D_MODEL), ys)


def kernel(x_prompt, x_sample, c_prompt, c_sample, state_ssm_re, state_ssm_im, norm1_g, norm2_g,
           w_ada, b_ada, w_in, w_s, b_s, g_v, lam_re, lam_im, log_dt, b_re, b_im, c_re, c_im,
           d_skip, w_glu, b_glu, out_g_a, out_g_b, w_out, w_router, router_bias, w_gate, w_up,
           w_down, ws_gate, ws_up, ws_down, final_g):
    assert norm1_g.shape[0] == 1
    bp, sp, _ = x_prompt.shape
    bs, ss, _ = x_sample.shape
    n_p, n_s = bp * sp, bs * ss
    n_tok = n_p + n_s
    l = 0

    mod = _adaln(jnp.concatenate([c_prompt, c_sample], axis=0), w_ada[l], b_ada[l])
    mod = mod.reshape(bp + bs, 6, D_MODEL)

    wb, wc, tabs = _s5_tables(lam_re[l], lam_im[l], log_dt[l], b_re[l], b_im[l], c_re[l], c_im[l])
    pos = jnp.arange(SGU_LEN)
    mask = (pos[:, None] // CHUNK) >= (pos[None, :] // CHUNK)
    ws_masked = jnp.where(mask[None], w_s[l], 0.0)
    row = lax.broadcasted_iota(I32, (MIX_ROWS, MIX_ROWS), 0)
    col = lax.broadcasted_iota(I32, (MIX_ROWS, MIX_ROWS), 1)
    ltri = (col < row).astype(BF16)

    def sgu_weights(ln):
        bsf = jnp.repeat(b_s[l][:, :ln].T, HEAD_A, axis=1)
        return ws_masked[:, :ln, :ln].astype(BF16), bsf

    ws_p, bsf_p = sgu_weights(SGU_LEN)
    ws_s, bsf_s = sgu_weights(ss)
    shared = [norm1_g[l].reshape(1, D_MODEL), w_in[l].astype(BF16), g_v[l].reshape(1, D_A),
              wb, wc, tabs, d_skip[l].reshape(1, D_B), w_glu[l].astype(BF16),
              b_glu[l].reshape(1, D_B), out_g_a[l].reshape(1, D_A), out_g_b[l].reshape(1, D_B),
              w_out[l].astype(BF16), norm2_g[l].reshape(1, D_MODEL), w_router[l].astype(BF16),
              router_bias[l].reshape(1, N_EXPERTS), ws_gate[l].astype(BF16),
              ws_up[l].astype(BF16), ws_down[l].astype(BF16), ltri]
    h0_s = jnp.broadcast_to(_lay(state_ssm_re[l], state_ssm_im[l])[:, None, :], (bs, SUB, STATE_COLS))
    xs1, h2t, slab, wslab, cnt_all, hfin_p, hfin_s, v_rows = _mix(
        x_prompt, x_sample, mod, h0_s, ws_p, bsf_p, ws_s, bsf_s, shared)

    idx = slab[:, :TOP_K]
    rank = slab[:, TOP_K:2 * TOP_K]
    wts = wslab[:, :TOP_K]
    counts = cnt_all[0].astype(I32)
    padded = (counts + MOE_BLK - 1) // MOE_BLK * MOE_BLK
    pend = jnp.cumsum(padded).astype(I32)
    pstart = pend - padded
    n_blocks = -(-n_tok * TOP_K // MOE_BLK) + N_EXPERTS
    n_used = (pend[-1:] // MOE_BLK).astype(I32)
    blk_start = jnp.arange(n_blocks, dtype=I32) * MOE_BLK
    blk_exp = jnp.sum((pend[None, :] <= blk_start[:, None]).astype(I32), axis=1)
    blk_exp = jnp.minimum(blk_exp, N_EXPERTS - 1)

    dst = jnp.take(pstart, idx, axis=0) + rank
    xs = _dispatch(dst, pstart, pend, h2t, n_blocks * MOE_BLK)
    ys = _experts(blk_exp, n_used, xs, w_gate[l], w_up[l], w_down[l])

    g2 = mod[:, 5, :]
    g2_blocks = jnp.concatenate([jnp.repeat(g2[:bp], sp // CHUNK, axis=0),
                                 jnp.repeat(g2[bp:], ss // CHUNK, axis=0)], axis=0)
    y_p, y_s = _combine(dst, xs1, g2_blocks.reshape(n_tok // CHUNK, 1, D_MODEL), wts, final_g, ys, n_p)

    re_p, im_p = _unlay(hfin_p[:, 0, :])
    re_s, im_s = _unlay(hfin_s[:, 0, :])
    return (y_p.reshape(bp, sp, D_MODEL), y_s.reshape(bs, ss, D_MODEL),
            re_p[None], im_p[None], re_s[None], im_s[None], v_rows[None])
```

```python
import functools

import jax
import jax.numpy as jnp
from jax import lax
from jax.experimental import pallas as pl
from jax.experimental.pallas import tpu as pltpu

F32 = jnp.float32
BF16 = jnp.bfloat16
I32 = jnp.int32

D_MODEL = 1024
D_A = 512
D_B = 512
N_HEADS_A = 4
HEAD_A = 128
SSM_GROUP = 16
N_GROUPS_B = 32
SSM_STATE = 64
N_EXPERTS = 256
TOP_K = 8
D_EXPERT = 256
ROUTE_SCALE = 2.5
CHUNK = 64
SGU_LEN = 128
EPS = 1e-6

STATE_CHUNKS = 4
GROUPS_PER_CHUNK = N_GROUPS_B // STATE_CHUNKS
HALF = GROUPS_PER_CHUNK * SSM_STATE
STATE_COLS = STATE_CHUNKS * 2 * HALF
SCAN_W = 256
SUB = 8
LANES = 128
ROW_TILES = D_MODEL // LANES

TAB_PW = 0
TAB_MD = 8
TAB_P8 = 11

MIX_ROWS = 256
MOE_BLK = 256
TOK_TILE = 256
SLAB = 128
VMEM_LIMIT = 56 * 1024 * 1024


def _dot(a, b):
    return jnp.dot(a, b, preferred_element_type=F32)


def _rms(x):
    return lax.rsqrt(jnp.mean(x * x, axis=-1, keepdims=True) + EPS)


def _tile_rows(j, n):
    return pl.ds(j, n, stride=ROW_TILES)


def _adaln_body(c_ref, w_ref, b_ref, o_ref):
    c = c_ref[...]
    o_ref[...] = _dot(jax.nn.silu(c).astype(BF16), w_ref[...].astype(BF16)) + b_ref[...]


def _adaln(c, w_ada, b_ada):
    n = c.shape[0]
    cols = w_ada.shape[1]
    blk = 1536
    return pl.pallas_call(
        _adaln_body,
        out_shape=jax.ShapeDtypeStruct((n, cols), F32),
        grid=(cols // blk,),
        in_specs=[pl.BlockSpec((n, D_MODEL), lambda j: (0, 0)),
                  pl.BlockSpec((D_MODEL, blk), lambda j: (0, j)),
                  pl.BlockSpec((1, blk), lambda j: (0, j))],
        out_specs=pl.BlockSpec((n, blk), lambda j: (0, j)),
        compiler_params=pltpu.CompilerParams(dimension_semantics=("arbitrary",)),
        name="adaln",
    )(c, w_ada, b_ada.reshape(1, cols))


def _lay(re, im):
    lead = re.shape[:-2]
    re = re.reshape(lead + (STATE_CHUNKS, HALF))
    im = im.reshape(lead + (STATE_CHUNKS, HALF))
    return jnp.concatenate([re, im], axis=-1).reshape(lead + (STATE_COLS,))


def _unlay(v):
    lead = v.shape[:-1]
    v = v.reshape(lead + (STATE_CHUNKS, 2, HALF))
    re = v[..., 0, :].reshape(lead + (N_GROUPS_B, SSM_STATE))
    im = v[..., 1, :].reshape(lead + (N_GROUPS_B, SSM_STATE))
    return re, im


def _s5_tables(lam_re, lam_im, log_dt, b_re, b_im, c_re, c_im):
    dt = jnp.exp(log_dt.astype(F32))[:, None]
    lr, li = lam_re.astype(F32), lam_im.astype(F32)

    def apow(k):
        mag = jnp.exp(lr * dt * k)
        return mag * jnp.cos(li * dt * k), mag * jnp.sin(li * dt * k)

    ar, ai = apow(1.0)
    den = lr * lr + li * li
    nr, ni = ar - 1.0, ai
    kr, ki = (nr * lr + ni * li) / den, (ni * lr - nr * li) / den
    br, bi = b_re.astype(F32), b_im.astype(F32)
    bbr = kr[..., None] * br - ki[..., None] * bi
    bbi = kr[..., None] * bi + ki[..., None] * br
    eye = jnp.eye(GROUPS_PER_CHUNK, dtype=F32)

    def bproj(bb):
        bb = bb.reshape(STATE_CHUNKS, GROUPS_PER_CHUNK, SSM_STATE, SSM_GROUP)
        w = jnp.einsum("mgph,gk->mghkp", bb, eye)
        return w.reshape(STATE_CHUNKS, GROUPS_PER_CHUNK * SSM_GROUP, HALF)

    wb = jnp.concatenate([bproj(bbr), bproj(bbi)], axis=-1).astype(BF16)

    def cproj(cc):
        cc = cc.reshape(STATE_CHUNKS, GROUPS_PER_CHUNK, SSM_GROUP, SSM_STATE)
        w = jnp.einsum("mghp,gk->mgpkh", cc, eye)
        return w.reshape(STATE_CHUNKS, HALF, GROUPS_PER_CHUNK * SSM_GROUP)

    wc = jnp.concatenate([cproj(c_re.astype(F32)), cproj(-c_im.astype(F32))], axis=1).astype(BF16)

    rows = jnp.arange(SUB, dtype=F32)
    tabs = []
    for i in range(SUB):
        pr, pi = apow(float(i + 1))
        tabs.append(jnp.broadcast_to(_lay(pr, pi)[None], (SUB, STATE_COLS)))
    for d in (1, 2, 4):
        pr, pi = apow(float(SUB * d))
        keep = (rows >= d).astype(F32)[:, None]
        tabs.append(_lay(pr, pi)[None] * keep)
    pr, pi = apow(SUB * rows[:, None, None])
    tabs.append(_lay(pr, pi))
    return wb, wc, jnp.stack(tabs)


def _cmul(ar, ai, br, bi):
    return ar * br - ai * bi, ar * bi + ai * br


def _s5_scan_block(bu_ref, row0, h_ref, tab_ref):
    row_id = lax.broadcasted_iota(I32, (SUB, SCAN_W), 0)
    tiles = SCAN_W // LANES
    for m in range(STATE_CHUNKS):
        for hf in range(HALF // SCAN_W):
            c_re0 = m * 2 * HALF + hf * SCAN_W
            c_im0 = c_re0 + HALF
            cre = pl.ds(c_re0, SCAN_W)
            cim = pl.ds(c_im0, SCAN_W)

            def tab(slot):
                return tab_ref[slot, :, cre], tab_ref[slot, :, cim]

            def load(i, c0):
                rows = pl.ds(row0 + i, SUB, stride=SUB)
                return jnp.concatenate([bu_ref[c0 // LANES + j, rows, :] for j in range(tiles)], axis=1)

            def store(i, c0, val):
                rows = pl.ds(row0 + i, SUB, stride=SUB)
                for j in range(tiles):
                    bu_ref[c0 // LANES + j, rows, :] = val[:, j * LANES:(j + 1) * LANES]

            a_re, a_im = tab(TAB_PW)
            s_re = load(0, c_re0)
            s_im = load(0, c_im0)
            loc = [(s_re, s_im)]
            for i in range(1, SUB):
                p_re, p_im = _cmul(a_re, a_im, s_re, s_im)
                s_re = p_re + load(i, c_re0)
                s_im = p_im + load(i, c_im0)
                loc.append((s_re, s_im))
            e_re, e_im = s_re, s_im
            for n, d in enumerate((1, 2, 4)):
                m_re, m_im = tab(TAB_MD + n)
                q_re, q_im = _cmul(m_re, m_im, pltpu.roll(e_re, d, 0), pltpu.roll(e_im, d, 0))
                e_re, e_im = e_re + q_re, e_im + q_im
            p8_re, p8_im = tab(TAB_P8)
            c_re, c_im = _cmul(p8_re, p8_im, h_ref[:, cre], h_ref[:, cim])
            c_re = c_re + jnp.where(row_id >= 1, pltpu.roll(e_re, 1, 0), 0.0)
            c_im = c_im + jnp.where(row_id >= 1, pltpu.roll(e_im, 1, 0), 0.0)
            for i in range(SUB):
                w_re, w_im = tab(TAB_PW + i)
                q_re, q_im = _cmul(w_re, w_im, c_re, c_im)
                f_re, f_im = loc[i][0] + q_re, loc[i][1] + q_im
                store(i, c_re0, f_re)
                store(i, c_im0, f_im)
            h_ref[:, cre] = jnp.broadcast_to(f_re[SUB - 1:SUB, :], (SUB, SCAN_W))
            h_ref[:, cim] = jnp.broadcast_to(f_im[SUB - 1:SUB, :], (SUB, SCAN_W))


def _mix_tile(seqs, ln, chain, new_seq, x_ref, mod_ref, h0_ref, ws_ref, bsf_ref, shared, outs,
              hfin_ref, v_ref, scratch):
    (n1g_ref, win_ref, gv_ref, wb_ref, wc_ref, tab_ref, dsk_ref, wglu_ref, bglu_ref, oga_ref,
     ogb_ref, wout_ref, n2g_ref, wr_ref, rb_ref, wsg_ref, wsu_ref, wsd_ref, ltri_ref) = shared
    xs1_ref, h2t_ref, slab_ref, wslab_ref, cnt_ref = outs
    bu_ref, h_scr, cnt_scr = scratch
    t_rows = MIX_ROWS
    rows_per_seq = t_rows // seqs

    x = x_ref[...].reshape(t_rows, D_MODEL)

    def modrow(j):
        parts = [jnp.broadcast_to(mod_ref[s, j:j + 1, :], (rows_per_seq, D_MODEL)) for s in range(seqs)]
        return parts[0] if seqs == 1 else jnp.concatenate(parts, axis=0)

    h = (x * _rms(x)) * n1g_ref[...] * (1.0 + modrow(1)) + modrow(0)
    z = _dot(h.astype(BF16), win_ref[...])
    u_a = z[:, :D_A]
    v_a = z[:, D_A:2 * D_A]
    u_b = z[:, 2 * D_A:]

    vh_parts = []
    for hh in range(N_HEADS_A):
        cols = slice(hh * HEAD_A, (hh + 1) * HEAD_A)
        vv = v_a[:, cols]
        vh_parts.append((vv * _rms(vv)) * gv_ref[:, cols])
    if v_ref is not None:
        v_ref[...] = jnp.concatenate(vh_parts, axis=1).reshape(seqs, rows_per_seq, D_A)
    n_sgu = t_rows // ln
    s_parts = []
    for hh in range(N_HEADS_A):
        vb = vh_parts[hh].astype(BF16)
        w_h = ws_ref[hh]
        s_parts.append(jnp.concatenate(
            [_dot(w_h, vb[c * ln:(c + 1) * ln, :]) for c in range(n_sgu)], axis=0))
    bsf = bsf_ref[...]
    s_mix = jnp.concatenate(s_parts, axis=1) + jnp.concatenate([bsf] * n_sgu, axis=0)
    y_a = u_a * s_mix

    ub16 = u_b.astype(BF16)
    gc = GROUPS_PER_CHUNK * SSM_GROUP
    tiles_per_chunk = 2 * HALF // LANES
    for m in range(STATE_CHUNKS):
        bu = _dot(ub16[:, m * gc:(m + 1) * gc], wb_ref[m])
        for j in range(tiles_per_chunk):
            bu_ref[m * tiles_per_chunk + j] = bu[:, j * LANES:(j + 1) * LANES]
    n_blk = t_rows // CHUNK
    if chain:
        @pl.when(new_seq)
        def _():
            h_scr[...] = jnp.zeros_like(h_scr)
    for blk in range(n_blk):
        if not chain:
            h_scr[...] = h0_ref[blk]
        _s5_scan_block(bu_ref, blk * CHUNK, h_scr, tab_ref)
        if not chain:
            hfin_ref[blk] = h_scr[...]
    if chain:
        hfin_ref[0] = h_scr[...]
    y_parts = []
    for m in range(STATE_CHUNKS):
        st = jnp.concatenate([bu_ref[m * tiles_per_chunk + j] for j in range(tiles_per_chunk)], axis=1)
        y_parts.append(_dot(st.astype(BF16), wc_ref[m]))
    y_s = jnp.concatenate(y_parts, axis=1) + dsk_ref[...] * u_b
    g_b = jax.nn.gelu(y_s)
    y_b = g_b * jax.nn.sigmoid(_dot(g_b.astype(BF16), wglu_ref[...]) + bglu_ref[...])

    na = (y_a * _rms(y_a)) * oga_ref[...]
    nb = (y_b * _rms(y_b)) * ogb_ref[...]
    mix = _dot(jnp.concatenate([na, nb], axis=1).astype(BF16), wout_ref[...])
    x1 = x + modrow(2) * mix

    h2 = (x1 * _rms(x1)) * n2g_ref[...] * (1.0 + modrow(4)) + modrow(3)
    for j in range(ROW_TILES):
        h2t_ref[_tile_rows(j, t_rows), :] = h2[:, j * LANES:(j + 1) * LANES]
    h2b = h2.astype(BF16)
    act = jax.nn.silu(_dot(h2b, wsg_ref[...])) * _dot(h2b, wsu_ref[...])
    shared_out = _dot(act.astype(BF16), wsd_ref[...])
    xs1_ref[...] = x1 + modrow(5) * shared_out

    scores = jax.nn.sigmoid(_dot(h2b, wr_ref[...]))
    lane = lax.broadcasted_iota(I32, (t_rows, N_EXPERTS), 1).astype(F32)
    work = scores + rb_ref[...]
    onehot = jnp.zeros((t_rows, N_EXPERTS), F32)
    idxs, sels = [], []
    for _ in range(TOP_K):
        top = jnp.max(work, axis=-1, keepdims=True)
        idx = jnp.min(jnp.where(work == top, lane, float(N_EXPERTS)), axis=-1, keepdims=True)
        pick = lane == idx
        sels.append(jnp.sum(jnp.where(pick, scores, 0.0), axis=-1, keepdims=True))
        idxs.append(idx)
        work = jnp.where(pick, -jnp.inf, work)
        onehot = jnp.where(pick, 1.0, onehot)
    total = sels[0]
    for k in range(1, TOP_K):
        total = total + sels[k]
    ranktab = _dot(ltri_ref[...], onehot.astype(BF16)) + cnt_scr[0:1, :]
    slab_lane = lax.broadcasted_iota(I32, (t_rows, SLAB), 1)
    slab = jnp.zeros((t_rows, SLAB), I32)
    wslab = jnp.zeros((t_rows, SLAB), F32)
    for k in range(TOP_K):
        rank = jnp.sum(jnp.where(lane == idxs[k], ranktab, 0.0), axis=-1, keepdims=True)
        slab = jnp.where(slab_lane == k, idxs[k].astype(I32), slab)
        slab = jnp.where(slab_lane == TOP_K + k, rank.astype(I32), slab)
        wslab = jnp.where(slab_lane == k, ROUTE_SCALE * sels[k] / total, wslab)
    slab_ref[...] = slab
    wslab_ref[...] = wslab
    cnt_scr[...] = cnt_scr[...] + jnp.sum(onehot, axis=0, keepdims=True)
    cnt_ref[...] = cnt_scr[...]


def _mix_body(n_prompt_tiles, tiles_per_seq, seqs_s, ln_s,
              xp_ref, xsm_ref, modp_ref, mods_ref, h0s_ref, wsp_ref, bsfp_ref, wss_ref, bsfs_ref,
              *rest):
    shared = rest[:19]
    xs1_ref, h2t_ref, slab_ref, wslab_ref, cnt_ref, hfp_ref, hfs_ref, v_ref = rest[19:27]
    scratch = rest[27:]
    outs = (xs1_ref, h2t_ref, slab_ref, wslab_ref, cnt_ref)
    s = pl.program_id(0)

    @pl.when(s == 0)
    def _():
        scratch[2][...] = jnp.zeros_like(scratch[2])

    @pl.when(s < n_prompt_tiles)
    def _():
        _mix_tile(1, SGU_LEN, True, lax.rem(s, tiles_per_seq) == 0, xp_ref, modp_ref, None,
                  wsp_ref, bsfp_ref, shared, outs, hfp_ref, None, scratch)

    @pl.when(s >= n_prompt_tiles)
    def _():
        _mix_tile(seqs_s, ln_s, False, None, xsm_ref, mods_ref, h0s_ref,
                  wss_ref, bsfs_ref, shared, outs, hfs_ref, v_ref, scratch)


def _mix(x_prompt, x_sample, mod, h0_s, ws_p, bsf_p, ws_s, bsf_s, shared):
    bp, sp, _ = x_prompt.shape
    bs, ss, _ = x_sample.shape
    assert sp % MIX_ROWS == 0 and MIX_ROWS % ss == 0 and ss == CHUNK
    seqs_s = MIX_ROWS // ss
    assert bs % seqs_s == 0 and bp % seqs_s == 0
    tiles_per_seq = sp // MIX_ROWS
    n_pt = bp * tiles_per_seq
    n_st = bs // seqs_s
    n_tok = bp * sp + bs * ss

    def p_tile(s):
        return jnp.minimum(s, n_pt - 1)

    def s_tile(s):
        return jnp.maximum(s - n_pt, 0)

    def const(shape):
        nd = len(shape)
        return pl.BlockSpec(shape, lambda s: (0,) * nd)

    in_specs = [
        pl.BlockSpec((1, MIX_ROWS, D_MODEL), lambda s: (p_tile(s) // tiles_per_seq, p_tile(s) % tiles_per_seq, 0)),
        pl.BlockSpec((seqs_s, ss, D_MODEL), lambda s: (s_tile(s), 0, 0)),
        pl.BlockSpec((1, 6, D_MODEL), lambda s: (p_tile(s) // tiles_per_seq, 0, 0)),
        pl.BlockSpec((seqs_s, 6, D_MODEL), lambda s: (bp // seqs_s + s_tile(s), 0, 0)),
        pl.BlockSpec((seqs_s, SUB, STATE_COLS), lambda s: (s_tile(s), 0, 0)),
        const(ws_p.shape), const(bsf_p.shape), const(ws_s.shape), const(bsf_s.shape),
    ] + [const(w.shape) for w in shared]
    out_shape = [
        jax.ShapeDtypeStruct((n_tok, D_MODEL), F32),
        jax.ShapeDtypeStruct((n_tok * ROW_TILES, LANES), F32),
        jax.ShapeDtypeStruct((n_tok, SLAB), I32),
        jax.ShapeDtypeStruct((n_tok, SLAB), F32),
        jax.ShapeDtypeStruct((SUB, N_EXPERTS), F32),
        jax.ShapeDtypeStruct((bp, SUB, STATE_COLS), F32),
        jax.ShapeDtypeStruct((bs, SUB, STATE_COLS), F32),
        jax.ShapeDtypeStruct((bs, ss, D_A), F32),
    ]
    out_specs = [
        pl.BlockSpec((MIX_ROWS, D_MODEL), lambda s: (s, 0)),
        pl.BlockSpec((MIX_ROWS * ROW_TILES, LANES), lambda s: (s, 0)),
        pl.BlockSpec((MIX_ROWS, SLAB), lambda s: (s, 0)),
        pl.BlockSpec((MIX_ROWS, SLAB), lambda s: (s, 0)),
        const((SUB, N_EXPERTS)),
        pl.BlockSpec((1, SUB, STATE_COLS), lambda s: (p_tile(s) // tiles_per_seq, 0, 0)),
        pl.BlockSpec((seqs_s, SUB, STATE_COLS), lambda s: (s_tile(s), 0, 0)),
        pl.BlockSpec((seqs_s, ss, D_A), lambda s: (s_tile(s), 0, 0)),
    ]
    return pl.pallas_call(
        functools.partial(_mix_body, n_pt, tiles_per_seq, seqs_s, ss),
        out_shape=out_shape,
        grid=(n_pt + n_st,),
        in_specs=in_specs,
        out_specs=out_specs,
        scratch_shapes=[pltpu.VMEM((STATE_COLS // LANES, MIX_ROWS, LANES), F32),
                        pltpu.VMEM((SUB, STATE_COLS), F32),
                        pltpu.VMEM((SUB, N_EXPERTS), F32)],
        compiler_params=pltpu.CompilerParams(dimension_semantics=("arbitrary",),
                                             vmem_limit_bytes=VMEM_LIMIT),
        name="mix",
    )(x_prompt, x_sample, mod, mod, h0_s, ws_p, bsf_p, ws_s, bsf_s, *shared)


def _token_rows(t):
    return pl.ds(pl.multiple_of(t * ROW_TILES, ROW_TILES), ROW_TILES)


def _dispatch_body(dst_ref, pstart_ref, pend_ref, h2t_ref, xs_hbm, zero_ref, sem, zsem):
    i = pl.program_id(0)
    blk_rows = MOE_BLK * ROW_TILES
    n_blocks = xs_hbm.shape[0] // blk_rows

    @pl.when(i == 0)
    def _():
        zero_ref[...] = jnp.zeros_like(zero_ref)

        def clear(blk):
            dst = xs_hbm.at[pl.ds(pl.multiple_of(blk * blk_rows, blk_rows), blk_rows)]
            return pltpu.make_async_copy(zero_ref, dst, zsem)

        def each(fn):
            def fill(e, carry):
                @pl.when(pend_ref[e] > pstart_ref[e])
                def _():
                    fn(clear(pend_ref[e] // MOE_BLK - 1))
                return carry
            lax.fori_loop(0, N_EXPERTS, fill, 0)

            def tail(b, carry):
                fn(clear(b))
                return carry
            lax.fori_loop(pend_ref[N_EXPERTS - 1] // MOE_BLK, n_blocks, tail, 0)

        each(lambda cp: cp.start())
        each(lambda cp: cp.wait())

    def issue(t, carry):
        for k in range(TOP_K):
            dst = dst_ref[0, 0, t * TOP_K + k]
            pltpu.make_async_copy(h2t_ref.at[_token_rows(t)], xs_hbm.at[_token_rows(dst)],
                                  sem).start(priority=k % 2)
        return carry
    lax.fori_loop(0, TOK_TILE, issue, 0)
    for _ in range(TOP_K):
        pltpu.make_async_copy(h2t_ref, xs_hbm.at[pl.ds(0, TOK_TILE * ROW_TILES)], sem).wait()


def _dispatch(dst, pstart, pend, h2t, n_rows):
    n_tiles = h2t.shape[0] // (TOK_TILE * ROW_TILES)
    smem_tile = pl.BlockSpec((1, 1, TOK_TILE * TOP_K), lambda i: (i, 0, 0), memory_space=pltpu.SMEM)
    smem_all = pl.BlockSpec(memory_space=pltpu.SMEM)
    return pl.pallas_call(
        _dispatch_body,
        out_shape=jax.ShapeDtypeStruct((n_rows * ROW_TILES, LANES), F32),
        grid=(n_tiles,),
        in_specs=[smem_tile, smem_all, smem_all,
                  pl.BlockSpec((TOK_TILE * ROW_TILES, LANES), lambda i: (i, 0))],
        out_specs=pl.BlockSpec(memory_space=pl.ANY),
        scratch_shapes=[pltpu.VMEM((MOE_BLK * ROW_TILES, LANES), F32),
                        pltpu.SemaphoreType.DMA, pltpu.SemaphoreType.DMA],
        compiler_params=pltpu.CompilerParams(dimension_semantics=("arbitrary",)),
        name="moe_dispatch",
    )(dst.reshape(n_tiles, 1, TOK_TILE * TOP_K), pstart, pend, h2t)


def _experts_body(bexp_ref, nused_ref, xs_ref, wg_ref, wu_ref, wd_ref, ys_ref, wgu_scr, wd_scr):
    b = pl.program_id(0)
    used = b < nused_ref[0]
    new_expert = jnp.logical_or(b == 0, bexp_ref[b] != bexp_ref[jnp.maximum(b - 1, 0)])

    @pl.when(jnp.logical_and(used, new_expert))
    def _():
        wgu_scr[:, :D_EXPERT] = wg_ref[0].astype(BF16)
        wgu_scr[:, D_EXPERT:] = wu_ref[0].astype(BF16)
        wd_scr[...] = wd_ref[0].astype(BF16)

    @pl.when(used)
    def _():
        xb = jnp.concatenate([xs_ref[_tile_rows(j, MOE_BLK), :] for j in range(ROW_TILES)],
                             axis=1).astype(BF16)
        gu = _dot(xb, wgu_scr[...])
        act = (jax.nn.silu(gu[:, :D_EXPERT]) * gu[:, D_EXPERT:]).astype(BF16)
        y = _dot(act, wd_scr[...])
        for j in range(ROW_TILES):
            ys_ref[_tile_rows(j, MOE_BLK), :] = y[:, j * LANES:(j + 1) * LANES]

    @pl.when(jnp.logical_not(used))
    def _():
        ys_ref[...] = jnp.zeros_like(ys_ref)


def _experts(blk_exp, n_used, xs, w_gate, w_up, w_down):
    blk_rows = MOE_BLK * ROW_TILES
    n_blocks = xs.shape[0] // blk_rows

    def row_map(b, bexp, nused):
        return (jnp.minimum(b, nused[0] - 1), 0)

    def out_map(b, bexp, nused):
        return (b, 0)

    def w_map(b, bexp, nused):
        return (bexp[jnp.minimum(b, nused[0] - 1)], 0, 0)

    return pl.pallas_call(
        _experts_body,
        out_shape=jax.ShapeDtypeStruct(xs.shape, F32),
        grid_spec=pltpu.PrefetchScalarGridSpec(
            num_scalar_prefetch=2,
            grid=(n_blocks,),
            in_specs=[pl.BlockSpec((blk_rows, LANES), row_map),
                      pl.BlockSpec((1, D_MODEL, D_EXPERT), w_map),
                      pl.BlockSpec((1, D_MODEL, D_EXPERT), w_map),
                      pl.BlockSpec((1, D_EXPERT, D_MODEL), w_map)],
            out_specs=pl.BlockSpec((blk_rows, LANES), out_map),
            scratch_shapes=[pltpu.VMEM((D_MODEL, 2 * D_EXPERT), BF16),
                            pltpu.VMEM((D_EXPERT, D_MODEL), BF16)]),
        compiler_params=pltpu.CompilerParams(dimension_semantics=("arbitrary",),
                                             vmem_limit_bytes=VMEM_LIMIT),
        name="moe_experts",
    )(blk_exp, n_used, xs, w_gate, w_up, w_down)


def _combine_body(n_prompt_tiles, dst_ref, xs1_ref, g2_ref, wts_ref, fg_ref,
                  ys_hbm, yp_ref, ysm_ref, buf_ref, sem):
    i = pl.program_id(0)

    def issue(t, carry):
        for k in range(TOP_K):
            src = dst_ref[0, 0, t * TOP_K + k]
            pltpu.make_async_copy(ys_hbm.at[_token_rows(src)], buf_ref.at[k, _token_rows(t)],
                                  sem).start(priority=k % 2)
        return carry
    lax.fori_loop(0, TOK_TILE, issue, 0)
    for k in range(TOP_K):
        pltpu.make_async_copy(ys_hbm.at[pl.ds(0, TOK_TILE * ROW_TILES)], buf_ref.at[k], sem).wait()
    wts = wts_ref[...]
    parts = []
    for j in range(ROW_TILES):
        acc = wts[:, 0:1] * buf_ref[0, _tile_rows(j, TOK_TILE), :]
        for k in range(1, TOP_K):
            acc = acc + wts[:, k:k + 1] * buf_ref[k, _tile_rows(j, TOK_TILE), :]
        parts.append(acc)
    seg = TOK_TILE // g2_ref.shape[0]
    g2 = jnp.concatenate([jnp.broadcast_to(g2_ref[s], (seg, D_MODEL)) for s in range(g2_ref.shape[0])],
                         axis=0)
    x = xs1_ref[...] + g2 * jnp.concatenate(parts, axis=1)
    y = (x * _rms(x)) * fg_ref[...]

    @pl.when(i < n_prompt_tiles)
    def _():
        yp_ref[...] = y

    @pl.when(i >= n_prompt_tiles)
    def _():
        ysm_ref[...] = y


def _combine(dst, xs1, g2_blocks, wts, final_g, ys, n_p):
    n_tok = xs1.shape[0]
    n_tiles = n_tok // TOK_TILE
    n_pt = n_p // TOK_TILE
    segs = TOK_TILE // CHUNK
    smem_tile = pl.BlockSpec((1, 1, TOK_TILE * TOP_K), lambda i: (i, 0, 0), memory_space=pltpu.SMEM)
    return pl.pallas_call(
        functools.partial(_combine_body, n_pt),
        out_shape=[jax.ShapeDtypeStruct((n_p, D_MODEL), F32),
                   jax.ShapeDtypeStruct((n_tok - n_p, D_MODEL), F32)],
        grid=(n_tiles,),
        in_specs=[smem_tile,
                  pl.BlockSpec((TOK_TILE, D_MODEL), lambda i: (i, 0)),
                  pl.BlockSpec((segs, 1, D_MODEL), lambda i: (i, 0, 0)),
                  pl.BlockSpec((TOK_TILE, TOP_K), lambda i: (i, 0)),
                  pl.BlockSpec((1, D_MODEL), lambda i: (0, 0)),
                  pl.BlockSpec(memory_space=pl.ANY)],
        out_specs=[pl.BlockSpec((TOK_TILE, D_MODEL), lambda i: (jnp.minimum(i, n_pt - 1), 0)),
                   pl.BlockSpec((TOK_TILE, D_MODEL), lambda i: (jnp.maximum(i - n_pt, 0), 0))],
        scratch_shapes=[pltpu.VMEM((TOP_K, TOK_TILE * ROW_TILES, LANES), F32),
                        pltpu.SemaphoreType.DMA],
        compiler_params=pltpu.CompilerParams(dimension_semantics=("arbitrary",),
                                             vmem_limit_bytes=VMEM_LIMIT),
        name="moe_combine",
    )(dst.reshape(n_tiles, 1, TOK_TILE * TOP_K), xs1, g2_blocks, wts, final_g.reshape(1, D_MODEL), ys)


def kernel(x_prompt, x_sample, c_prompt, c_sample, state_ssm_re, state_ssm_im, norm1_g, norm2_g,
           w_ada, b_ada, w_in, w_s, b_s, g_v, lam_re, lam_im, log_dt, b_re, b_im, c_re, c_im,
           d_skip, w_glu, b_glu, out_g_a, out_g_b, w_out, w_router, router_bias, w_gate, w_up,
           w_down, ws_gate, ws_up, ws_down, final_g):
    assert norm1_g.shape[0] == 1
    bp, sp, _ = x_prompt.shape
    bs, ss, _ = x_sample.shape
    n_p, n_s = bp * sp, bs * ss
    n_tok = n_p + n_s
    l = 0

    mod = _adaln(jnp.concatenate([c_prompt, c_sample], axis=0), w_ada[l], b_ada[l])
    mod = mod.reshape(bp + bs, 6, D_MODEL)

    wb, wc, tabs = _s5_tables(lam_re[l], lam_im[l], log_dt[l], b_re[l], b_im[l], c_re[l], c_im[l])
    pos = jnp.arange(SGU_LEN)
    mask = (pos[:, None] // CHUNK) >= (pos[None, :] // CHUNK)
    ws_masked = jnp.where(mask[None], w_s[l], 0.0)
    row = lax.broadcasted_iota(I32, (MIX_ROWS, MIX_ROWS), 0)
    col = lax.broadcasted_iota(I32, (MIX_ROWS, MIX_ROWS), 1)
    ltri = (col < row).astype(BF16)

    def sgu_weights(ln):
        bsf = jnp.repeat(b_s[l][:, :ln].T, HEAD_A, axis=1)
        return ws_masked[:, :ln, :ln].astype(BF16), bsf

    ws_p, bsf_p = sgu_weights(SGU_LEN)
    ws_s, bsf_s = sgu_weights(ss)
    shared = [norm1_g[l].reshape(1, D_MODEL), w_in[l].astype(BF16), g_v[l].reshape(1, D_A),
              wb, wc, tabs, d_skip[l].reshape(1, D_B), w_glu[l].astype(BF16),
              b_glu[l].reshape(1, D_B), out_g_a[l].reshape(1, D_A), out_g_b[l].reshape(1, D_B),
              w_out[l].astype(BF16), norm2_g[l].reshape(1, D_MODEL), w_router[l].astype(BF16),
              router_bias[l].reshape(1, N_EXPERTS), ws_gate[l].astype(BF16),
              ws_up[l].astype(BF16), ws_down[l].astype(BF16), ltri]
    h0_s = jnp.broadcast_to(_lay(state_ssm_re[l], state_ssm_im[l])[:, None, :], (bs, SUB, STATE_COLS))
    xs1, h2t, slab, wslab, cnt_all, hfin_p, hfin_s, v_rows = _mix(
        x_prompt, x_sample, mod, h0_s, ws_p, bsf_p, ws_s, bsf_s, shared)

    idx = slab[:, :TOP_K]
    rank = slab[:, TOP_K:2 * TOP_K]
    wts = wslab[:, :TOP_K]
    counts = cnt_all[0].astype(I32)
    padded = (counts + MOE_BLK - 1) // MOE_BLK * MOE_BLK
    pend = jnp.cumsum(padded).astype(I32)
    pstart = pend - padded
    n_blocks = -(-n_tok * TOP_K // MOE_BLK) + N_EXPERTS
    n_used = (pend[-1:] // MOE_BLK).astype(I32)
    blk_start = jnp.arange(n_blocks, dtype=I32) * MOE_BLK
    blk_exp = jnp.sum((pend[None, :] <= blk_start[:, None]).astype(I32), axis=1)
    blk_exp = jnp.minimum(blk_exp, N_EXPERTS - 1)

    experts = jnp.arange(N_EXPERTS, dtype=I32)
    dst = rank + jnp.sum(jnp.where(idx[..., None] == experts, pstart, 0), axis=-1)
    xs = _dispatch(dst, pstart, pend, h2t, n_blocks * MOE_BLK)
    ys = _experts(blk_exp, n_used, xs, w_gate[l], w_up[l], w_down[l])

    g2 = mod[:, 5, :]
    g2_blocks = jnp.concatenate([jnp.repeat(g2[:bp], sp // CHUNK, axis=0),
                                 jnp.repeat(g2[bp:], ss // CHUNK, axis=0)], axis=0)
    y_p, y_s = _combine(dst, xs1, g2_blocks.reshape(n_tok // CHUNK, 1, D_MODEL), wts, final_g, ys, n_p)

    re_p, im_p = _unlay(hfin_p[:, 0, :])
    re_s, im_s = _unlay(hfin_s[:, 0, :])
    return (y_p.reshape(bp, sp, D_MODEL), y_s.reshape(bs, ss, D_MODEL),
            re_p[None], im_p[None], re_s[None], im_s[None], v_rows[None])
```

```python
import functools

import jax
import jax.numpy as jnp
from jax import lax
from jax.experimental import pallas as pl
from jax.experimental.pallas import tpu as pltpu

F32 = jnp.float32
BF16 = jnp.bfloat16
I32 = jnp.int32

D_MODEL = 1024
D_A = 512
D_B = 512
N_HEADS_A = 4
HEAD_A = 128
SSM_GROUP = 16
N_GROUPS_B = 32
SSM_STATE = 64
N_EXPERTS = 256
TOP_K = 8
D_EXPERT = 256
ROUTE_SCALE = 2.5
CHUNK = 64
SGU_LEN = 128
EPS = 1e-6

STATE_CHUNKS = 4
GROUPS_PER_CHUNK = N_GROUPS_B // STATE_CHUNKS
HALF = GROUPS_PER_CHUNK * SSM_STATE
STATE_COLS = STATE_CHUNKS * 2 * HALF
SCAN_W = 256
SUB = 8
LANES = 128
ROW_TILES = D_MODEL // LANES

TAB_PW = 0
TAB_MD = 8
TAB_P8 = 11

MIX_ROWS = 256
MOE_BLK = 256
TOK_TILE = 256
SLAB = 128
VMEM_LIMIT = 56 * 1024 * 1024


def _dot(a, b):
    return jnp.dot(a, b, preferred_element_type=F32)


def _rms(x):
    return lax.rsqrt(jnp.mean(x * x, axis=-1, keepdims=True) + EPS)


def _tile_rows(j, n):
    return pl.ds(j, n, stride=ROW_TILES)


def _adaln_body(c_ref, w_ref, b_ref, o_ref):
    c = c_ref[...]
    o_ref[...] = _dot(jax.nn.silu(c).astype(BF16), w_ref[...].astype(BF16)) + b_ref[...]


def _adaln(c, w_ada, b_ada):
    n = c.shape[0]
    cols = w_ada.shape[1]
    blk = 1536
    return pl.pallas_call(
        _adaln_body,
        out_shape=jax.ShapeDtypeStruct((n, cols), F32),
        grid=(cols // blk,),
        in_specs=[pl.BlockSpec((n, D_MODEL), lambda j: (0, 0)),
                  pl.BlockSpec((D_MODEL, blk), lambda j: (0, j)),
                  pl.BlockSpec((1, blk), lambda j: (0, j))],
        out_specs=pl.BlockSpec((n, blk), lambda j: (0, j)),
        compiler_params=pltpu.CompilerParams(dimension_semantics=("arbitrary",)),
        name="adaln",
    )(c, w_ada, b_ada.reshape(1, cols))


def _lay(re, im):
    lead = re.shape[:-2]
    re = re.reshape(lead + (STATE_CHUNKS, HALF))
    im = im.reshape(lead + (STATE_CHUNKS, HALF))
    return jnp.concatenate([re, im], axis=-1).reshape(lead + (STATE_COLS,))


def _unlay(v):
    lead = v.shape[:-1]
    v = v.reshape(lead + (STATE_CHUNKS, 2, HALF))
    re = v[..., 0, :].reshape(lead + (N_GROUPS_B, SSM_STATE))
    im = v[..., 1, :].reshape(lead + (N_GROUPS_B, SSM_STATE))
    return re, im


def _s5_tables(lam_re, lam_im, log_dt, b_re, b_im, c_re, c_im):
    dt = jnp.exp(log_dt.astype(F32))[:, None]
    lr, li = lam_re.astype(F32), lam_im.astype(F32)

    def apow(k):
        mag = jnp.exp(lr * dt * k)
        return mag * jnp.cos(li * dt * k), mag * jnp.sin(li * dt * k)

    ar, ai = apow(1.0)
    den = lr * lr + li * li
    nr, ni = ar - 1.0, ai
    kr, ki = (nr * lr + ni * li) / den, (ni * lr - nr * li) / den
    br, bi = b_re.astype(F32), b_im.astype(F32)
    bbr = kr[..., None] * br - ki[..., None] * bi
    bbi = kr[..., None] * bi + ki[..., None] * br
    eye = jnp.eye(GROUPS_PER_CHUNK, dtype=F32)

    def bproj(bb):
        bb = bb.reshape(STATE_CHUNKS, GROUPS_PER_CHUNK, SSM_STATE, SSM_GROUP)
        w = jnp.einsum("mgph,gk->mghkp", bb, eye)
        return w.reshape(STATE_CHUNKS, GROUPS_PER_CHUNK * SSM_GROUP, HALF)

    wb = jnp.concatenate([bproj(bbr), bproj(bbi)], axis=-1).astype(BF16)

    def cproj(cc):
        cc = cc.reshape(STATE_CHUNKS, GROUPS_PER_CHUNK, SSM_GROUP, SSM_STATE)
        w = jnp.einsum("mghp,gk->mgpkh", cc, eye)
        return w.reshape(STATE_CHUNKS, HALF, GROUPS_PER_CHUNK * SSM_GROUP)

    wc = jnp.concatenate([cproj(c_re.astype(F32)), cproj(-c_im.astype(F32))], axis=1).astype(BF16)

    rows = jnp.arange(SUB, dtype=F32)
    tabs = []
    for i in range(SUB):
        pr, pi = apow(float(i + 1))
        tabs.append(jnp.broadcast_to(_lay(pr, pi)[None], (SUB, STATE_COLS)))
    for d in (1, 2, 4):
        pr, pi = apow(float(SUB * d))
        keep = (rows >= d).astype(F32)[:, None]
        tabs.append(_lay(pr, pi)[None] * keep)
    pr, pi = apow(SUB * rows[:, None, None])
    tabs.append(_lay(pr, pi))
    return wb, wc, jnp.stack(tabs)


def _cmul(ar, ai, br, bi):
    return ar * br - ai * bi, ar * bi + ai * br


def _s5_scan_block(bu_ref, row0, h_ref, tab_ref):
    row_id = lax.broadcasted_iota(I32, (SUB, SCAN_W), 0)
    tiles = SCAN_W // LANES
    for m in range(STATE_CHUNKS):
        for hf in range(HALF // SCAN_W):
            c_re0 = m * 2 * HALF + hf * SCAN_W
            c_im0 = c_re0 + HALF
            cre = pl.ds(c_re0, SCAN_W)
            cim = pl.ds(c_im0, SCAN_W)

            def tab(slot):
                return tab_ref[slot, :, cre], tab_ref[slot, :, cim]

            def load(i, c0):
                rows = pl.ds(row0 + i, SUB, stride=SUB)
                return jnp.concatenate([bu_ref[c0 // LANES + j, rows, :] for j in range(tiles)], axis=1)

            def store(i, c0, val):
                rows = pl.ds(row0 + i, SUB, stride=SUB)
                for j in range(tiles):
                    bu_ref[c0 // LANES + j, rows, :] = val[:, j * LANES:(j + 1) * LANES]

            a_re, a_im = tab(TAB_PW)
            s_re = load(0, c_re0)
            s_im = load(0, c_im0)
            loc = [(s_re, s_im)]
            for i in range(1, SUB):
                p_re, p_im = _cmul(a_re, a_im, s_re, s_im)
                s_re = p_re + load(i, c_re0)
                s_im = p_im + load(i, c_im0)
                loc.append((s_re, s_im))
            e_re, e_im = s_re, s_im
            for n, d in enumerate((1, 2, 4)):
                m_re, m_im = tab(TAB_MD + n)
                q_re, q_im = _cmul(m_re, m_im, pltpu.roll(e_re, d, 0), pltpu.roll(e_im, d, 0))
                e_re, e_im = e_re + q_re, e_im + q_im
            p8_re, p8_im = tab(TAB_P8)
            c_re, c_im = _cmul(p8_re, p8_im, h_ref[:, cre], h_ref[:, cim])
            c_re = c_re + jnp.where(row_id >= 1, pltpu.roll(e_re, 1, 0), 0.0)
            c_im = c_im + jnp.where(row_id >= 1, pltpu.roll(e_im, 1, 0), 0.0)
            for i in range(SUB):
                w_re, w_im = tab(TAB_PW + i)
                q_re, q_im = _cmul(w_re, w_im, c_re, c_im)
                f_re, f_im = loc[i][0] + q_re, loc[i][1] + q_im
                store(i, c_re0, f_re)
                store(i, c_im0, f_im)
            h_ref[:, cre] = jnp.broadcast_to(f_re[SUB - 1:SUB, :], (SUB, SCAN_W))
            h_ref[:, cim] = jnp.broadcast_to(f_im[SUB - 1:SUB, :], (SUB, SCAN_W))


def _mix_tile(seqs, ln, chain, new_seq, x_ref, mod_ref, h0_ref, ws_ref, bsf_ref, shared, outs,
              hfin_ref, v_ref, scratch):
    (n1g_ref, win_ref, gv_ref, wb_ref, wc_ref, tab_ref, dsk_ref, wglu_ref, bglu_ref, oga_ref,
     ogb_ref, wout_ref, n2g_ref, wr_ref, rb_ref, wsg_ref, wsu_ref, wsd_ref, ltri_ref) = shared
    xs1_ref, h2t_ref, slab_ref, wslab_ref, cnt_ref = outs
    bu_ref, h_scr, cnt_scr = scratch
    t_rows = MIX_ROWS
    rows_per_seq = t_rows // seqs

    x = x_ref[...].reshape(t_rows, D_MODEL)

    def modrow(j):
        parts = [jnp.broadcast_to(mod_ref[s, j:j + 1, :], (rows_per_seq, D_MODEL)) for s in range(seqs)]
        return parts[0] if seqs == 1 else jnp.concatenate(parts, axis=0)

    h = (x * _rms(x)) * n1g_ref[...] * (1.0 + modrow(1)) + modrow(0)
    z = _dot(h.astype(BF16), win_ref[...])
    u_a = z[:, :D_A]
    v_a = z[:, D_A:2 * D_A]
    u_b = z[:, 2 * D_A:]

    vh_parts = []
    for hh in range(N_HEADS_A):
        cols = slice(hh * HEAD_A, (hh + 1) * HEAD_A)
        vv = v_a[:, cols]
        vh_parts.append((vv * _rms(vv)) * gv_ref[:, cols])
    if v_ref is not None:
        v_ref[...] = jnp.concatenate(vh_parts, axis=1).reshape(seqs, rows_per_seq, D_A)
    n_sgu = t_rows // ln
    s_parts = []
    for hh in range(N_HEADS_A):
        vb = vh_parts[hh].astype(BF16)
        w_h = ws_ref[hh]
        s_parts.append(jnp.concatenate(
            [_dot(w_h, vb[c * ln:(c + 1) * ln, :]) for c in range(n_sgu)], axis=0))
    bsf = bsf_ref[...]
    s_mix = jnp.concatenate(s_parts, axis=1) + jnp.concatenate([bsf] * n_sgu, axis=0)
    y_a = u_a * s_mix

    ub16 = u_b.astype(BF16)
    gc = GROUPS_PER_CHUNK * SSM_GROUP
    tiles_per_chunk = 2 * HALF // LANES
    for m in range(STATE_CHUNKS):
        bu = _dot(ub16[:, m * gc:(m + 1) * gc], wb_ref[m])
        for j in range(tiles_per_chunk):
            bu_ref[m * tiles_per_chunk + j] = bu[:, j * LANES:(j + 1) * LANES]
    n_blk = t_rows // CHUNK
    if chain:
        @pl.when(new_seq)
        def _():
            h_scr[...] = jnp.zeros_like(h_scr)
    for blk in range(n_blk):
        if not chain:
            h_scr[...] = h0_ref[blk]
        _s5_scan_block(bu_ref, blk * CHUNK, h_scr, tab_ref)
        if not chain:
            hfin_ref[blk] = h_scr[...]
    if chain:
        hfin_ref[0] = h_scr[...]
    y_parts = []
    for m in range(STATE_CHUNKS):
        st = jnp.concatenate([bu_ref[m * tiles_per_chunk + j] for j in range(tiles_per_chunk)], axis=1)
        y_parts.append(_dot(st.astype(BF16), wc_ref[m]))
    y_s = jnp.concatenate(y_parts, axis=1) + dsk_ref[...] * u_b
    g_b = jax.nn.gelu(y_s)
    y_b = g_b * jax.nn.sigmoid(_dot(g_b.astype(BF16), wglu_ref[...]) + bglu_ref[...])

    na = (y_a * _rms(y_a)) * oga_ref[...]
    nb = (y_b * _rms(y_b)) * ogb_ref[...]
    mix = _dot(jnp.concatenate([na, nb], axis=1).astype(BF16), wout_ref[...])
    x1 = x + modrow(2) * mix

    h2 = (x1 * _rms(x1)) * n2g_ref[...] * (1.0 + modrow(4)) + modrow(3)
    for j in range(ROW_TILES):
        h2t_ref[_tile_rows(j, t_rows), :] = h2[:, j * LANES:(j + 1) * LANES]
    h2b = h2.astype(BF16)
    act = jax.nn.silu(_dot(h2b, wsg_ref[...])) * _dot(h2b, wsu_ref[...])
    shared_out = _dot(act.astype(BF16), wsd_ref[...])
    xs1_ref[...] = x1 + modrow(5) * shared_out

    scores = jax.nn.sigmoid(_dot(h2b, wr_ref[...]))
    lane = lax.broadcasted_iota(I32, (t_rows, N_EXPERTS), 1).astype(F32)
    work = scores + rb_ref[...]
    onehot = jnp.zeros((t_rows, N_EXPERTS), F32)
    idxs, sels = [], []
    for _ in range(TOP_K):
        top = jnp.max(work, axis=-1, keepdims=True)
        idx = jnp.min(jnp.where(work == top, lane, float(N_EXPERTS)), axis=-1, keepdims=True)
        pick = lane == idx
        sels.append(jnp.sum(jnp.where(pick, scores, 0.0), axis=-1, keepdims=True))
        idxs.append(idx)
        work = jnp.where(pick, -jnp.inf, work)
        onehot = jnp.where(pick, 1.0, onehot)
    total = sels[0]
    for k in range(1, TOP_K):
        total = total + sels[k]
    ranktab = _dot(ltri_ref[...], onehot.astype(BF16)) + cnt_scr[0:1, :]
    slab_lane = lax.broadcasted_iota(I32, (t_rows, SLAB), 1)
    slab = jnp.zeros((t_rows, SLAB), I32)
    wslab = jnp.zeros((t_rows, SLAB), F32)
    for k in range(TOP_K):
        rank = jnp.sum(jnp.where(lane == idxs[k], ranktab, 0.0), axis=-1, keepdims=True)
        slab = jnp.where(slab_lane == k, idxs[k].astype(I32), slab)
        slab = jnp.where(slab_lane == TOP_K + k, rank.astype(I32), slab)
        wslab = jnp.where(slab_lane == k, ROUTE_SCALE * sels[k] / total, wslab)
    slab_ref[...] = slab
    wslab_ref[...] = wslab
    cnt_scr[...] = cnt_scr[...] + jnp.sum(onehot, axis=0, keepdims=True)
    cnt_ref[...] = cnt_scr[...]


def _mix_body(n_prompt_tiles, tiles_per_seq, seqs_s, ln_s,
              xp_ref, xsm_ref, modp_ref, mods_ref, h0s_ref, wsp_ref, bsfp_ref, wss_ref, bsfs_ref,
              *rest):
    shared = rest[:19]
    xs1_ref, h2t_ref, slab_ref, wslab_ref, cnt_ref, hfp_ref, hfs_ref, v_ref = rest[19:27]
    scratch = rest[27:]
    outs = (xs1_ref, h2t_ref, slab_ref, wslab_ref, cnt_ref)
    s = pl.program_id(0)

    @pl.when(s == 0)
    def _():
        scratch[2][...] = jnp.zeros_like(scratch[2])

    @pl.when(s < n_prompt_tiles)
    def _():
        _mix_tile(1, SGU_LEN, True, lax.rem(s, tiles_per_seq) == 0, xp_ref, modp_ref, None,
                  wsp_ref, bsfp_ref, shared, outs, hfp_ref, None, scratch)

    @pl.when(s >= n_prompt_tiles)
    def _():
        _mix_tile(seqs_s, ln_s, False, None, xsm_ref, mods_ref, h0s_ref,
                  wss_ref, bsfs_ref, shared, outs, hfs_ref, v_ref, scratch)


def _mix(x_prompt, x_sample, mod, h0_s, ws_p, bsf_p, ws_s, bsf_s, shared):
    bp, sp, _ = x_prompt.shape
    bs, ss, _ = x_sample.shape
    assert sp % MIX_ROWS == 0 and MIX_ROWS % ss == 0 and ss == CHUNK
    seqs_s = MIX_ROWS // ss
    assert bs % seqs_s == 0 and bp % seqs_s == 0
    tiles_per_seq = sp // MIX_ROWS
    n_pt = bp * tiles_per_seq
    n_st = bs // seqs_s
    n_tok = bp * sp + bs * ss

    def p_tile(s):
        return jnp.minimum(s, n_pt - 1)

    def s_tile(s):
        return jnp.maximum(s - n_pt, 0)

    def const(shape):
        nd = len(shape)
        return pl.BlockSpec(shape, lambda s: (0,) * nd)

    in_specs = [
        pl.BlockSpec((1, MIX_ROWS, D_MODEL), lambda s: (p_tile(s) // tiles_per_seq, p_tile(s) % tiles_per_seq, 0)),
        pl.BlockSpec((seqs_s, ss, D_MODEL), lambda s: (s_tile(s), 0, 0)),
        pl.BlockSpec((1, 6, D_MODEL), lambda s: (p_tile(s) // tiles_per_seq, 0, 0)),
        pl.BlockSpec((seqs_s, 6, D_MODEL), lambda s: (bp // seqs_s + s_tile(s), 0, 0)),
        pl.BlockSpec((seqs_s, SUB, STATE_COLS), lambda s: (s_tile(s), 0, 0)),
        const(ws_p.shape), const(bsf_p.shape), const(ws_s.shape), const(bsf_s.shape),
    ] + [const(w.shape) for w in shared]
    out_shape = [
        jax.ShapeDtypeStruct((n_tok, D_MODEL), F32),
        jax.ShapeDtypeStruct((n_tok * ROW_TILES, LANES), F32),
        jax.ShapeDtypeStruct((n_tok, SLAB), I32),
        jax.ShapeDtypeStruct((n_tok, SLAB), F32),
        jax.ShapeDtypeStruct((SUB, N_EXPERTS), F32),
        jax.ShapeDtypeStruct((bp, SUB, STATE_COLS), F32),
        jax.ShapeDtypeStruct((bs, SUB, STATE_COLS), F32),
        jax.ShapeDtypeStruct((bs, ss, D_A), F32),
    ]
    out_specs = [
        pl.BlockSpec((MIX_ROWS, D_MODEL), lambda s: (s, 0)),
        pl.BlockSpec((MIX_ROWS * ROW_TILES, LANES), lambda s: (s, 0)),
        pl.BlockSpec((MIX_ROWS, SLAB), lambda s: (s, 0)),
        pl.BlockSpec((MIX_ROWS, SLAB), lambda s: (s, 0)),
        const((SUB, N_EXPERTS)),
        pl.BlockSpec((1, SUB, STATE_COLS), lambda s: (p_tile(s) // tiles_per_seq, 0, 0)),
        pl.BlockSpec((seqs_s, SUB, STATE_COLS), lambda s: (s_tile(s), 0, 0)),
        pl.BlockSpec((seqs_s, ss, D_A), lambda s: (s_tile(s), 0, 0)),
    ]
    return pl.pallas_call(
        functools.partial(_mix_body, n_pt, tiles_per_seq, seqs_s, ss),
        out_shape=out_shape,
        grid=(n_pt + n_st,),
        in_specs=in_specs,
        out_specs=out_specs,
        scratch_shapes=[pltpu.VMEM((STATE_COLS // LANES, MIX_ROWS, LANES), F32),
                        pltpu.VMEM((SUB, STATE_COLS), F32),
                        pltpu.VMEM((SUB, N_EXPERTS), F32)],
        compiler_params=pltpu.CompilerParams(dimension_semantics=("arbitrary",),
                                             vmem_limit_bytes=VMEM_LIMIT),
        name="mix",
    )(x_prompt, x_sample, mod, mod, h0_s, ws_p, bsf_p, ws_s, bsf_s, *shared)


SLOT_TILE = 1024


def _slots_body(slab_ref, pstart_ref, dst_ref):
    slab = slab_ref[...]
    pstart = pstart_ref[...]
    lane = lax.broadcasted_iota(I32, (SLOT_TILE, N_EXPERTS), 1)
    out_lane = lax.broadcasted_iota(I32, (SLOT_TILE, SLAB), 1)
    out = jnp.zeros((SLOT_TILE, SLAB), I32)
    for k in range(TOP_K):
        start = jnp.sum(jnp.where(lane == slab[:, k:k + 1], pstart, 0.0), axis=-1, keepdims=True)
        out = jnp.where(out_lane == k, start.astype(I32) + slab[:, TOP_K + k:TOP_K + k + 1], out)
    dst_ref[...] = out


def _slots(slab, pstart):
    n_tok = slab.shape[0]
    assert n_tok % SLOT_TILE == 0
    return pl.pallas_call(
        _slots_body,
        out_shape=jax.ShapeDtypeStruct((n_tok, SLAB), I32),
        grid=(n_tok // SLOT_TILE,),
        in_specs=[pl.BlockSpec((SLOT_TILE, SLAB), lambda i: (i, 0)),
                  pl.BlockSpec((1, N_EXPERTS), lambda i: (0, 0))],
        out_specs=pl.BlockSpec((SLOT_TILE, SLAB), lambda i: (i, 0)),
        compiler_params=pltpu.CompilerParams(dimension_semantics=("arbitrary",)),
        name="moe_slots",
    )(slab, pstart.astype(F32).reshape(1, N_EXPERTS))


def _token_rows(t):
    return pl.ds(pl.multiple_of(t * ROW_TILES, ROW_TILES), ROW_TILES)


def _dispatch_body(dst_ref, pstart_ref, pend_ref, h2t_ref, xs_hbm, zero_ref, sem, zsem):
    i = pl.program_id(0)
    blk_rows = MOE_BLK * ROW_TILES
    n_blocks = xs_hbm.shape[0] // blk_rows

    @pl.when(i == 0)
    def _():
        zero_ref[...] = jnp.zeros_like(zero_ref)

        def clear(blk):
            dst = xs_hbm.at[pl.ds(pl.multiple_of(blk * blk_rows, blk_rows), blk_rows)]
            return pltpu.make_async_copy(zero_ref, dst, zsem)

        def each(fn):
            def fill(e, carry):
                @pl.when(pend_ref[e] > pstart_ref[e])
                def _():
                    fn(clear(pend_ref[e] // MOE_BLK - 1))
                return carry
            lax.fori_loop(0, N_EXPERTS, fill, 0)

            def tail(b, carry):
                fn(clear(b))
                return carry
            lax.fori_loop(pend_ref[N_EXPERTS - 1] // MOE_BLK, n_blocks, tail, 0)

        each(lambda cp: cp.start())
        each(lambda cp: cp.wait())

    def issue(t, carry):
        for k in range(TOP_K):
            dst = dst_ref[0, 0, t * TOP_K + k]
            pltpu.make_async_copy(h2t_ref.at[_token_rows(t)], xs_hbm.at[_token_rows(dst)],
                                  sem).start(priority=k % 2)
        return carry
    lax.fori_loop(0, TOK_TILE, issue, 0)
    for _ in range(TOP_K):
        pltpu.make_async_copy(h2t_ref, xs_hbm.at[pl.ds(0, TOK_TILE * ROW_TILES)], sem).wait()


def _dispatch(dst, pstart, pend, h2t, n_rows):
    n_tiles = h2t.shape[0] // (TOK_TILE * ROW_TILES)
    smem_tile = pl.BlockSpec((1, 1, TOK_TILE * TOP_K), lambda i: (i, 0, 0), memory_space=pltpu.SMEM)
    smem_all = pl.BlockSpec(memory_space=pltpu.SMEM)
    return pl.pallas_call(
        _dispatch_body,
        out_shape=jax.ShapeDtypeStruct((n_rows * ROW_TILES, LANES), F32),
        grid=(n_tiles,),
        in_specs=[smem_tile, smem_all, smem_all,
                  pl.BlockSpec((TOK_TILE * ROW_TILES, LANES), lambda i: (i, 0))],
        out_specs=pl.BlockSpec(memory_space=pl.ANY),
        scratch_shapes=[pltpu.VMEM((MOE_BLK * ROW_TILES, LANES), F32),
                        pltpu.SemaphoreType.DMA, pltpu.SemaphoreType.DMA],
        compiler_params=pltpu.CompilerParams(dimension_semantics=("arbitrary",)),
        name="moe_dispatch",
    )(dst.reshape(n_tiles, 1, TOK_TILE * TOP_K), pstart, pend, h2t)


def _experts_body(bexp_ref, nused_ref, xs_ref, wg_ref, wu_ref, wd_ref, ys_ref, wgu_scr, wd_scr):
    b = pl.program_id(0)
    used = b < nused_ref[0]
    new_expert = jnp.logical_or(b == 0, bexp_ref[b] != bexp_ref[jnp.maximum(b - 1, 0)])

    @pl.when(jnp.logical_and(used, new_expert))
    def _():
        wgu_scr[:, :D_EXPERT] = wg_ref[0].astype(BF16)
        wgu_scr[:, D_EXPERT:] = wu_ref[0].astype(BF16)
        wd_scr[...] = wd_ref[0].astype(BF16)

    @pl.when(used)
    def _():
        xb = jnp.concatenate([xs_ref[_tile_rows(j, MOE_BLK), :] for j in range(ROW_TILES)],
                             axis=1).astype(BF16)
        gu = _dot(xb, wgu_scr[...])
        act = (jax.nn.silu(gu[:, :D_EXPERT]) * gu[:, D_EXPERT:]).astype(BF16)
        y = _dot(act, wd_scr[...])
        for j in range(ROW_TILES):
            ys_ref[_tile_rows(j, MOE_BLK), :] = y[:, j * LANES:(j + 1) * LANES]

    @pl.when(jnp.logical_not(used))
    def _():
        ys_ref[...] = jnp.zeros_like(ys_ref)


def _experts(blk_exp, n_used, xs, w_gate, w_up, w_down):
    blk_rows = MOE_BLK * ROW_TILES
    n_blocks = xs.shape[0] // blk_rows

    def row_map(b, bexp, nused):
        return (jnp.minimum(b, nused[0] - 1), 0)

    def out_map(b, bexp, nused):
        return (b, 0)

    def w_map(b, bexp, nused):
        return (bexp[jnp.minimum(b, nused[0] - 1)], 0, 0)

    return pl.pallas_call(
        _experts_body,
        out_shape=jax.ShapeDtypeStruct(xs.shape, F32),
        grid_spec=pltpu.PrefetchScalarGridSpec(
            num_scalar_prefetch=2,
            grid=(n_blocks,),
            in_specs=[pl.BlockSpec((blk_rows, LANES), row_map),
                      pl.BlockSpec((1, D_MODEL, D_EXPERT), w_map),
                      pl.BlockSpec((1, D_MODEL, D_EXPERT), w_map),
                      pl.BlockSpec((1, D_EXPERT, D_MODEL), w_map)],
            out_specs=pl.BlockSpec((blk_rows, LANES), out_map),
            scratch_shapes=[pltpu.VMEM((D_MODEL, 2 * D_EXPERT), BF16),
                            pltpu.VMEM((D_EXPERT, D_MODEL), BF16)]),
        compiler_params=pltpu.CompilerParams(dimension_semantics=("arbitrary",),
                                             vmem_limit_bytes=VMEM_LIMIT),
        name="moe_experts",
    )(blk_exp, n_used, xs, w_gate, w_up, w_down)


def _combine_body(n_prompt_tiles, dst_ref, xs1_ref, g2_ref, wts_ref, fg_ref,
                  ys_hbm, yp_ref, ysm_ref, buf_ref, sem):
    i = pl.program_id(0)

    def issue(t, carry):
        for k in range(TOP_K):
            src = dst_ref[0, 0, t * TOP_K + k]
            pltpu.make_async_copy(ys_hbm.at[_token_rows(src)], buf_ref.at[k, _token_rows(t)],
                                  sem).start(priority=k % 2)
        return carry
    lax.fori_loop(0, TOK_TILE, issue, 0)
    for k in range(TOP_K):
        pltpu.make_async_copy(ys_hbm.at[pl.ds(0, TOK_TILE * ROW_TILES)], buf_ref.at[k], sem).wait()
    wts = wts_ref[...]
    parts = []
    for j in range(ROW_TILES):
        acc = wts[:, 0:1] * buf_ref[0, _tile_rows(j, TOK_TILE), :]
        for k in range(1, TOP_K):
            acc = acc + wts[:, k:k + 1] * buf_ref[k, _tile_rows(j, TOK_TILE), :]
        parts.append(acc)
    seg = TOK_TILE // g2_ref.shape[0]
    g2 = jnp.concatenate([jnp.broadcast_to(g2_ref[s], (seg, D_MODEL)) for s in range(g2_ref.shape[0])],
                         axis=0)
    x = xs1_ref[...] + g2 * jnp.concatenate(parts, axis=1)
    y = (x * _rms(x)) * fg_ref[...]

    @pl.when(i < n_prompt_tiles)
    def _():
        yp_ref[...] = y

    @pl.when(i >= n_prompt_tiles)
    def _():
        ysm_ref[...] = y


def _combine(dst, xs1, g2_blocks, wts, final_g, ys, n_p):
    n_tok = xs1.shape[0]
    n_tiles = n_tok // TOK_TILE
    n_pt = n_p // TOK_TILE
    segs = TOK_TILE // CHUNK
    smem_tile = pl.BlockSpec((1, 1, TOK_TILE * TOP_K), lambda i: (i, 0, 0), memory_space=pltpu.SMEM)
    return pl.pallas_call(
        functools.partial(_combine_body, n_pt),
        out_shape=[jax.ShapeDtypeStruct((n_p, D_MODEL), F32),
                   jax.ShapeDtypeStruct((n_tok - n_p, D_MODEL), F32)],
        grid=(n_tiles,),
        in_specs=[smem_tile,
                  pl.BlockSpec((TOK_TILE, D_MODEL), lambda i: (i, 0)),
                  pl.BlockSpec((segs, 1, D_MODEL), lambda i: (i, 0, 0)),
                  pl.BlockSpec((TOK_TILE, TOP_K), lambda i: (i, 0)),
                  pl.BlockSpec((1, D_MODEL), lambda i: (0, 0)),
                  pl.BlockSpec(memory_space=pl.ANY)],
        out_specs=[pl.BlockSpec((TOK_TILE, D_MODEL), lambda i: (jnp.minimum(i, n_pt - 1), 0)),
                   pl.BlockSpec((TOK_TILE, D_MODEL), lambda i: (jnp.maximum(i - n_pt, 0), 0))],
        scratch_shapes=[pltpu.VMEM((TOP_K, TOK_TILE * ROW_TILES, LANES), F32),
                        pltpu.SemaphoreType.DMA],
        compiler_params=pltpu.CompilerParams(dimension_semantics=("arbitrary",),
                                             vmem_limit_bytes=VMEM_LIMIT),
        name="moe_combine",
    )(dst.reshape(n_tiles, 1, TOK_TILE * TOP_K), xs1, g2_blocks, wts, final_g.reshape(1, D_MODEL), ys)


def kernel(x_prompt, x_sample, c_prompt, c_sample, state_ssm_re, state_ssm_im, norm1_g, norm2_g,
           w_ada, b_ada, w_in, w_s, b_s, g_v, lam_re, lam_im, log_dt, b_re, b_im, c_re, c_im,
           d_skip, w_glu, b_glu, out_g_a, out_g_b, w_out, w_router, router_bias, w_gate, w_up,
           w_down, ws_gate, ws_up, ws_down, final_g):
    assert norm1_g.shape[0] == 1
    bp, sp, _ = x_prompt.shape
    bs, ss, _ = x_sample.shape
    n_p, n_s = bp * sp, bs * ss
    n_tok = n_p + n_s
    l = 0

    mod = _adaln(jnp.concatenate([c_prompt, c_sample], axis=0), w_ada[l], b_ada[l])
    mod = mod.reshape(bp + bs, 6, D_MODEL)

    wb, wc, tabs = _s5_tables(lam_re[l], lam_im[l], log_dt[l], b_re[l], b_im[l], c_re[l], c_im[l])
    pos = jnp.arange(SGU_LEN)
    mask = (pos[:, None] // CHUNK) >= (pos[None, :] // CHUNK)
    ws_masked = jnp.where(mask[None], w_s[l], 0.0)
    row = lax.broadcasted_iota(I32, (MIX_ROWS, MIX_ROWS), 0)
    col = lax.broadcasted_iota(I32, (MIX_ROWS, MIX_ROWS), 1)
    ltri = (col < row).astype(BF16)

    def sgu_weights(ln):
        bsf = jnp.repeat(b_s[l][:, :ln].T, HEAD_A, axis=1)
        return ws_masked[:, :ln, :ln].astype(BF16), bsf

    ws_p, bsf_p = sgu_weights(SGU_LEN)
    ws_s, bsf_s = sgu_weights(ss)
    shared = [norm1_g[l].reshape(1, D_MODEL), w_in[l].astype(BF16), g_v[l].reshape(1, D_A),
              wb, wc, tabs, d_skip[l].reshape(1, D_B), w_glu[l].astype(BF16),
              b_glu[l].reshape(1, D_B), out_g_a[l].reshape(1, D_A), out_g_b[l].reshape(1, D_B),
              w_out[l].astype(BF16), norm2_g[l].reshape(1, D_MODEL), w_router[l].astype(BF16),
              router_bias[l].reshape(1, N_EXPERTS), ws_gate[l].astype(BF16),
              ws_up[l].astype(BF16), ws_down[l].astype(BF16), ltri]
    h0_s = jnp.broadcast_to(_lay(state_ssm_re[l], state_ssm_im[l])[:, None, :], (bs, SUB, STATE_COLS))
    xs1, h2t, slab, wslab, cnt_all, hfin_p, hfin_s, v_rows = _mix(
        x_prompt, x_sample, mod, h0_s, ws_p, bsf_p, ws_s, bsf_s, shared)

    wts = wslab[:, :TOP_K]
    counts = cnt_all[0].astype(I32)
    padded = (counts + MOE_BLK - 1) // MOE_BLK * MOE_BLK
    pend = jnp.cumsum(padded).astype(I32)
    pstart = pend - padded
    n_blocks = -(-n_tok * TOP_K // MOE_BLK) + N_EXPERTS
    n_used = (pend[-1:] // MOE_BLK).astype(I32)
    blk_start = jnp.arange(n_blocks, dtype=I32) * MOE_BLK
    blk_exp = jnp.sum((pend[None, :] <= blk_start[:, None]).astype(I32), axis=1)
    blk_exp = jnp.minimum(blk_exp, N_EXPERTS - 1)

    dst = _slots(slab, pstart)[:, :TOP_K]
    xs = _dispatch(dst, pstart, pend, h2t, n_blocks * MOE_BLK)
    ys = _experts(blk_exp, n_used, xs, w_gate[l], w_up[l], w_down[l])

    g2 = mod[:, 5, :]
    g2_blocks = jnp.concatenate([jnp.repeat(g2[:bp], sp // CHUNK, axis=0),
                                 jnp.repeat(g2[bp:], ss // CHUNK, axis=0)], axis=0)
    y_p, y_s = _combine(dst, xs1, g2_blocks.reshape(n_tok // CHUNK, 1, D_MODEL), wts, final_g, ys, n_p)

    re_p, im_p = _unlay(hfin_p[:, 0, :])
    re_s, im_s = _unlay(hfin_s[:, 0, :])
    return (y_p.reshape(bp, sp, D_MODEL), y_s.reshape(bs, ss, D_MODEL),
            re_p[None], im_p[None], re_s[None], im_s[None], v_rows[None])
```

```python
import functools

import jax
import jax.numpy as jnp
from jax import lax
from jax.experimental import pallas as pl
from jax.experimental.pallas import tpu as pltpu

F32 = jnp.float32
BF16 = jnp.bfloat16
I32 = jnp.int32

D_MODEL = 1024
D_A = 512
D_B = 512
N_HEADS_A = 4
HEAD_A = 128
SSM_GROUP = 16
N_GROUPS_B = 32
SSM_STATE = 64
N_EXPERTS = 256
TOP_K = 8
D_EXPERT = 256
ROUTE_SCALE = 2.5
CHUNK = 64
SGU_LEN = 128
EPS = 1e-6

STATE_CHUNKS = 4
GROUPS_PER_CHUNK = N_GROUPS_B // STATE_CHUNKS
HALF = GROUPS_PER_CHUNK * SSM_STATE
STATE_COLS = STATE_CHUNKS * 2 * HALF
SCAN_W = 256
SUB = 8
LANES = 128
ROW_TILES = D_MODEL // LANES

TAB_PW = 0
TAB_MD = 8
TAB_P8 = 11

MIX_ROWS = 256
MOE_BLK = 256
TOK_TILE = 256
SLAB = 128
VMEM_LIMIT = 56 * 1024 * 1024


def _dot(a, b):
    return jnp.dot(a, b, preferred_element_type=F32)


def _rms(x):
    return lax.rsqrt(jnp.mean(x * x, axis=-1, keepdims=True) + EPS)


def _tile_rows(j, n):
    return pl.ds(j, n, stride=ROW_TILES)


def _adaln_body(c_ref, w_ref, b_ref, o_ref):
    c = c_ref[...]
    o_ref[...] = _dot(jax.nn.silu(c).astype(BF16), w_ref[...].astype(BF16)) + b_ref[...]


def _adaln(c, w_ada, b_ada):
    n = c.shape[0]
    cols = w_ada.shape[1]
    blk = 1536
    return pl.pallas_call(
        _adaln_body,
        out_shape=jax.ShapeDtypeStruct((n, cols), F32),
        grid=(cols // blk,),
        in_specs=[pl.BlockSpec((n, D_MODEL), lambda j: (0, 0)),
                  pl.BlockSpec((D_MODEL, blk), lambda j: (0, j)),
                  pl.BlockSpec((1, blk), lambda j: (0, j))],
        out_specs=pl.BlockSpec((n, blk), lambda j: (0, j)),
        compiler_params=pltpu.CompilerParams(dimension_semantics=("arbitrary",)),
        name="adaln",
    )(c, w_ada, b_ada.reshape(1, cols))


def _lay(re, im):
    lead = re.shape[:-2]
    re = re.reshape(lead + (STATE_CHUNKS, HALF))
    im = im.reshape(lead + (STATE_CHUNKS, HALF))
    return jnp.concatenate([re, im], axis=-1).reshape(lead + (STATE_COLS,))


def _unlay(v):
    lead = v.shape[:-1]
    v = v.reshape(lead + (STATE_CHUNKS, 2, HALF))
    re = v[..., 0, :].reshape(lead + (N_GROUPS_B, SSM_STATE))
    im = v[..., 1, :].reshape(lead + (N_GROUPS_B, SSM_STATE))
    return re, im


def _s5_tables(lam_re, lam_im, log_dt, b_re, b_im, c_re, c_im):
    dt = jnp.exp(log_dt.astype(F32))[:, None]
    lr, li = lam_re.astype(F32), lam_im.astype(F32)

    def apow(k):
        mag = jnp.exp(lr * dt * k)
        return mag * jnp.cos(li * dt * k), mag * jnp.sin(li * dt * k)

    ar, ai = apow(1.0)
    den = lr * lr + li * li
    nr, ni = ar - 1.0, ai
    kr, ki = (nr * lr + ni * li) / den, (ni * lr - nr * li) / den
    br, bi = b_re.astype(F32), b_im.astype(F32)
    bbr = kr[..., None] * br - ki[..., None] * bi
    bbi = kr[..., None] * bi + ki[..., None] * br
    eye = jnp.eye(GROUPS_PER_CHUNK, dtype=F32)

    def bproj(bb):
        bb = bb.reshape(STATE_CHUNKS, GROUPS_PER_CHUNK, SSM_STATE, SSM_GROUP)
        w = jnp.einsum("mgph,gk->mghkp", bb, eye)
        return w.reshape(STATE_CHUNKS, GROUPS_PER_CHUNK * SSM_GROUP, HALF)

    wb = jnp.concatenate([bproj(bbr), bproj(bbi)], axis=-1).astype(BF16)

    def cproj(cc):
        cc = cc.reshape(STATE_CHUNKS, GROUPS_PER_CHUNK, SSM_GROUP, SSM_STATE)
        w = jnp.einsum("mghp,gk->mgpkh", cc, eye)
        return w.reshape(STATE_CHUNKS, HALF, GROUPS_PER_CHUNK * SSM_GROUP)

    wc = jnp.concatenate([cproj(c_re.astype(F32)), cproj(-c_im.astype(F32))], axis=1).astype(BF16)

    rows = jnp.arange(SUB, dtype=F32)
    tabs = []
    for i in range(SUB):
        pr, pi = apow(float(i + 1))
        tabs.append(jnp.broadcast_to(_lay(pr, pi)[None], (SUB, STATE_COLS)))
    for d in (1, 2, 4):
        pr, pi = apow(float(SUB * d))
        keep = (rows >= d).astype(F32)[:, None]
        tabs.append(_lay(pr, pi)[None] * keep)
    pr, pi = apow(SUB * rows[:, None, None])
    tabs.append(_lay(pr, pi))
    return wb, wc, jnp.stack(tabs)


def _cmul(ar, ai, br, bi):
    return ar * br - ai * bi, ar * bi + ai * br


def _s5_scan_block(bu_ref, row0, h_ref, tab_ref):
    row_id = lax.broadcasted_iota(I32, (SUB, SCAN_W), 0)
    tiles = SCAN_W // LANES
    for m in range(STATE_CHUNKS):
        for hf in range(HALF // SCAN_W):
            c_re0 = m * 2 * HALF + hf * SCAN_W
            c_im0 = c_re0 + HALF
            cre = pl.ds(c_re0, SCAN_W)
            cim = pl.ds(c_im0, SCAN_W)

            def tab(slot):
                return tab_ref[slot, :, cre], tab_ref[slot, :, cim]

            def load(i, c0):
                rows = pl.ds(row0 + i, SUB, stride=SUB)
                return jnp.concatenate([bu_ref[c0 // LANES + j, rows, :] for j in range(tiles)], axis=1)

            def store(i, c0, val):
                rows = pl.ds(row0 + i, SUB, stride=SUB)
                for j in range(tiles):
                    bu_ref[c0 // LANES + j, rows, :] = val[:, j * LANES:(j + 1) * LANES]

            a_re, a_im = tab(TAB_PW)
            s_re = load(0, c_re0)
            s_im = load(0, c_im0)
            loc = [(s_re, s_im)]
            for i in range(1, SUB):
                p_re, p_im = _cmul(a_re, a_im, s_re, s_im)
                s_re = p_re + load(i, c_re0)
                s_im = p_im + load(i, c_im0)
                loc.append((s_re, s_im))
            e_re, e_im = s_re, s_im
            for n, d in enumerate((1, 2, 4)):
                m_re, m_im = tab(TAB_MD + n)
                q_re, q_im = _cmul(m_re, m_im, pltpu.roll(e_re, d, 0), pltpu.roll(e_im, d, 0))
                e_re, e_im = e_re + q_re, e_im + q_im
            p8_re, p8_im = tab(TAB_P8)
            c_re, c_im = _cmul(p8_re, p8_im, h_ref[:, cre], h_ref[:, cim])
            c_re = c_re + jnp.where(row_id >= 1, pltpu.roll(e_re, 1, 0), 0.0)
            c_im = c_im + jnp.where(row_id >= 1, pltpu.roll(e_im, 1, 0), 0.0)
            for i in range(SUB):
                w_re, w_im = tab(TAB_PW + i)
                q_re, q_im = _cmul(w_re, w_im, c_re, c_im)
                f_re, f_im = loc[i][0] + q_re, loc[i][1] + q_im
                store(i, c_re0, f_re)
                store(i, c_im0, f_im)
            h_ref[:, cre] = jnp.broadcast_to(f_re[SUB - 1:SUB, :], (SUB, SCAN_W))
            h_ref[:, cim] = jnp.broadcast_to(f_im[SUB - 1:SUB, :], (SUB, SCAN_W))


def _mix_tile(seqs, ln, chain, new_seq, x_ref, mod_ref, h0_ref, ws_ref, bsf_ref, shared, outs,
              hfin_ref, v_ref, scratch):
    (n1g_ref, win_ref, gv_ref, wb_ref, wc_ref, tab_ref, dsk_ref, wglu_ref, bglu_ref, oga_ref,
     ogb_ref, wout_ref, n2g_ref, wr_ref, rb_ref, wsg_ref, wsu_ref, wsd_ref, ltri_ref) = shared
    xs1_ref, h2t_ref, slab_ref, wslab_ref, cnt_ref = outs
    bu_ref, h_scr, cnt_scr = scratch
    t_rows = MIX_ROWS
    rows_per_seq = t_rows // seqs

    x = x_ref[...].reshape(t_rows, D_MODEL)

    def modrow(j):
        parts = [jnp.broadcast_to(mod_ref[s, j:j + 1, :], (rows_per_seq, D_MODEL)) for s in range(seqs)]
        return parts[0] if seqs == 1 else jnp.concatenate(parts, axis=0)

    h = (x * _rms(x)) * n1g_ref[...] * (1.0 + modrow(1)) + modrow(0)
    z = _dot(h.astype(BF16), win_ref[...])
    u_a = z[:, :D_A]
    v_a = z[:, D_A:2 * D_A]
    u_b = z[:, 2 * D_A:]

    vh_parts = []
    for hh in range(N_HEADS_A):
        cols = slice(hh * HEAD_A, (hh + 1) * HEAD_A)
        vv = v_a[:, cols]
        vh_parts.append((vv * _rms(vv)) * gv_ref[:, cols])
    if v_ref is not None:
        v_ref[...] = jnp.concatenate(vh_parts, axis=1).reshape(seqs, rows_per_seq, D_A)
    n_sgu = t_rows // ln
    s_parts = []
    for hh in range(N_HEADS_A):
        vb = vh_parts[hh].astype(BF16)
        w_h = ws_ref[hh]
        s_parts.append(jnp.concatenate(
            [_dot(w_h, vb[c * ln:(c + 1) * ln, :]) for c in range(n_sgu)], axis=0))
    bsf = bsf_ref[...]
    s_mix = jnp.concatenate(s_parts, axis=1) + jnp.concatenate([bsf] * n_sgu, axis=0)
    y_a = u_a * s_mix

    ub16 = u_b.astype(BF16)
    gc = GROUPS_PER_CHUNK * SSM_GROUP
    tiles_per_chunk = 2 * HALF // LANES
    for m in range(STATE_CHUNKS):
        bu = _dot(ub16[:, m * gc:(m + 1) * gc], wb_ref[m])
        for j in range(tiles_per_chunk):
            bu_ref[m * tiles_per_chunk + j] = bu[:, j * LANES:(j + 1) * LANES]
    n_blk = t_rows // CHUNK
    if chain:
        @pl.when(new_seq)
        def _():
            h_scr[...] = jnp.zeros_like(h_scr)
    for blk in range(n_blk):
        if not chain:
            h_scr[...] = h0_ref[blk]
        _s5_scan_block(bu_ref, blk * CHUNK, h_scr, tab_ref)
        if not chain:
            hfin_ref[blk] = h_scr[...]
    if chain:
        hfin_ref[0] = h_scr[...]
    y_parts = []
    for m in range(STATE_CHUNKS):
        st = jnp.concatenate([bu_ref[m * tiles_per_chunk + j] for j in range(tiles_per_chunk)], axis=1)
        y_parts.append(_dot(st.astype(BF16), wc_ref[m]))
    y_s = jnp.concatenate(y_parts, axis=1) + dsk_ref[...] * u_b
    g_b = jax.nn.gelu(y_s)
    y_b = g_b * jax.nn.sigmoid(_dot(g_b.astype(BF16), wglu_ref[...]) + bglu_ref[...])

    na = (y_a * _rms(y_a)) * oga_ref[...]
    nb = (y_b * _rms(y_b)) * ogb_ref[...]
    mix = _dot(jnp.concatenate([na, nb], axis=1).astype(BF16), wout_ref[...])
    x1 = x + modrow(2) * mix

    h2 = (x1 * _rms(x1)) * n2g_ref[...] * (1.0 + modrow(4)) + modrow(3)
    for j in range(ROW_TILES):
        h2t_ref[_tile_rows(j, t_rows), :] = h2[:, j * LANES:(j + 1) * LANES]
    h2b = h2.astype(BF16)
    act = jax.nn.silu(_dot(h2b, wsg_ref[...])) * _dot(h2b, wsu_ref[...])
    shared_out = _dot(act.astype(BF16), wsd_ref[...])
    xs1_ref[...] = x1 + modrow(5) * shared_out

    scores = jax.nn.sigmoid(_dot(h2b, wr_ref[...]))
    lane = lax.broadcasted_iota(I32, (t_rows, N_EXPERTS), 1).astype(F32)
    work = scores + rb_ref[...]
    onehot = jnp.zeros((t_rows, N_EXPERTS), F32)
    idxs, sels = [], []
    for _ in range(TOP_K):
        top = jnp.max(work, axis=-1, keepdims=True)
        idx = jnp.min(jnp.where(work == top, lane, float(N_EXPERTS)), axis=-1, keepdims=True)
        pick = lane == idx
        sels.append(jnp.sum(jnp.where(pick, scores, 0.0), axis=-1, keepdims=True))
        idxs.append(idx)
        work = jnp.where(pick, -jnp.inf, work)
        onehot = jnp.where(pick, 1.0, onehot)
    total = sels[0]
    for k in range(1, TOP_K):
        total = total + sels[k]
    ranktab = _dot(ltri_ref[...], onehot.astype(BF16)) + cnt_scr[0:1, :]
    slab_lane = lax.broadcasted_iota(I32, (t_rows, SLAB), 1)
    slab = jnp.zeros((t_rows, SLAB), I32)
    wslab = jnp.zeros((t_rows, SLAB), F32)
    for k in range(TOP_K):
        rank = jnp.sum(jnp.where(lane == idxs[k], ranktab, 0.0), axis=-1, keepdims=True)
        slab = jnp.where(slab_lane == k, idxs[k].astype(I32), slab)
        slab = jnp.where(slab_lane == TOP_K + k, rank.astype(I32), slab)
        wslab = jnp.where(slab_lane == k, ROUTE_SCALE * sels[k] / total, wslab)
    slab_ref[...] = slab
    wslab_ref[...] = wslab
    cnt_scr[...] = cnt_scr[...] + jnp.sum(onehot, axis=0, keepdims=True)
    cnt_ref[...] = cnt_scr[...]


def _mix_body(n_prompt_tiles, tiles_per_seq, seqs_s, ln_s,
              xp_ref, xsm_ref, modp_ref, mods_ref, h0s_ref, wsp_ref, bsfp_ref, wss_ref, bsfs_ref,
              *rest):
    shared = rest[:19]
    xs1_ref, h2t_ref, slab_ref, wslab_ref, cnt_ref, hfp_ref, hfs_ref, v_ref = rest[19:27]
    scratch = rest[27:]
    outs = (xs1_ref, h2t_ref, slab_ref, wslab_ref, cnt_ref)
    s = pl.program_id(0)

    @pl.when(s == 0)
    def _():
        scratch[2][...] = jnp.zeros_like(scratch[2])

    @pl.when(s < n_prompt_tiles)
    def _():
        _mix_tile(1, SGU_LEN, True, lax.rem(s, tiles_per_seq) == 0, xp_ref, modp_ref, None,
                  wsp_ref, bsfp_ref, shared, outs, hfp_ref, None, scratch)

    @pl.when(s >= n_prompt_tiles)
    def _():
        _mix_tile(seqs_s, ln_s, False, None, xsm_ref, mods_ref, h0s_ref,
                  wss_ref, bsfs_ref, shared, outs, hfs_ref, v_ref, scratch)


def _mix(x_prompt, x_sample, mod, h0_s, ws_p, bsf_p, ws_s, bsf_s, shared):
    bp, sp, _ = x_prompt.shape
    bs, ss, _ = x_sample.shape
    assert sp % MIX_ROWS == 0 and MIX_ROWS % ss == 0 and ss == CHUNK
    seqs_s = MIX_ROWS // ss
    assert bs % seqs_s == 0 and bp % seqs_s == 0
    tiles_per_seq = sp // MIX_ROWS
    n_pt = bp * tiles_per_seq
    n_st = bs // seqs_s
    n_tok = bp * sp + bs * ss

    def p_tile(s):
        return jnp.minimum(s, n_pt - 1)

    def s_tile(s):
        return jnp.maximum(s - n_pt, 0)

    def const(shape):
        nd = len(shape)
        return pl.BlockSpec(shape, lambda s: (0,) * nd)

    in_specs = [
        pl.BlockSpec((1, MIX_ROWS, D_MODEL), lambda s: (p_tile(s) // tiles_per_seq, p_tile(s) % tiles_per_seq, 0)),
        pl.BlockSpec((seqs_s, ss, D_MODEL), lambda s: (s_tile(s), 0, 0)),
        pl.BlockSpec((1, 6, D_MODEL), lambda s: (p_tile(s) // tiles_per_seq, 0, 0)),
        pl.BlockSpec((seqs_s, 6, D_MODEL), lambda s: (bp // seqs_s + s_tile(s), 0, 0)),
        pl.BlockSpec((seqs_s, SUB, STATE_COLS), lambda s: (s_tile(s), 0, 0)),
        const(ws_p.shape), const(bsf_p.shape), const(ws_s.shape), const(bsf_s.shape),
    ] + [const(w.shape) for w in shared]
    out_shape = [
        jax.ShapeDtypeStruct((n_tok, D_MODEL), F32),
        jax.ShapeDtypeStruct((n_tok * ROW_TILES, LANES), F32),
        jax.ShapeDtypeStruct((n_tok, SLAB), I32),
        jax.ShapeDtypeStruct((n_tok, SLAB), F32),
        jax.ShapeDtypeStruct((SUB, N_EXPERTS), F32),
        jax.ShapeDtypeStruct((bp, SUB, STATE_COLS), F32),
        jax.ShapeDtypeStruct((bs, SUB, STATE_COLS), F32),
        jax.ShapeDtypeStruct((bs, ss, D_A), F32),
    ]
    out_specs = [
        pl.BlockSpec((MIX_ROWS, D_MODEL), lambda s: (s, 0)),
        pl.BlockSpec((MIX_ROWS * ROW_TILES, LANES), lambda s: (s, 0)),
        pl.BlockSpec((MIX_ROWS, SLAB), lambda s: (s, 0)),
        pl.BlockSpec((MIX_ROWS, SLAB), lambda s: (s, 0)),
        const((SUB, N_EXPERTS)),
        pl.BlockSpec((1, SUB, STATE_COLS), lambda s: (p_tile(s) // tiles_per_seq, 0, 0)),
        pl.BlockSpec((seqs_s, SUB, STATE_COLS), lambda s: (s_tile(s), 0, 0)),
        pl.BlockSpec((seqs_s, ss, D_A), lambda s: (s_tile(s), 0, 0)),
    ]
    return pl.pallas_call(
        functools.partial(_mix_body, n_pt, tiles_per_seq, seqs_s, ss),
        out_shape=out_shape,
        grid=(n_pt + n_st,),
        in_specs=in_specs,
        out_specs=out_specs,
        scratch_shapes=[pltpu.VMEM((STATE_COLS // LANES, MIX_ROWS, LANES), F32),
                        pltpu.VMEM((SUB, STATE_COLS), F32),
                        pltpu.VMEM((SUB, N_EXPERTS), F32)],
        compiler_params=pltpu.CompilerParams(dimension_semantics=("arbitrary",),
                                             vmem_limit_bytes=VMEM_LIMIT),
        name="mix",
    )(x_prompt, x_sample, mod, mod, h0_s, ws_p, bsf_p, ws_s, bsf_s, *shared)


SLOT_TILE = 1024


def _slots_body(slab_ref, pstart_ref, dst_ref):
    slab = slab_ref[...]
    pstart = pstart_ref[...]
    lane = lax.broadcasted_iota(I32, (SLOT_TILE, N_EXPERTS), 1)
    out_lane = lax.broadcasted_iota(I32, (SLOT_TILE, SLAB), 1)
    out = jnp.zeros((SLOT_TILE, SLAB), I32)
    for k in range(TOP_K):
        start = jnp.sum(jnp.where(lane == slab[:, k:k + 1], pstart, 0.0), axis=-1, keepdims=True)
        out = jnp.where(out_lane == k, start.astype(I32) + slab[:, TOP_K + k:TOP_K + k + 1], out)
    dst_ref[...] = out


def _slots(slab, pstart):
    n_tok = slab.shape[0]
    assert n_tok % SLOT_TILE == 0
    return pl.pallas_call(
        _slots_body,
        out_shape=jax.ShapeDtypeStruct((n_tok, SLAB), I32),
        grid=(n_tok // SLOT_TILE,),
        in_specs=[pl.BlockSpec((SLOT_TILE, SLAB), lambda i: (i, 0)),
                  pl.BlockSpec((1, N_EXPERTS), lambda i: (0, 0))],
        out_specs=pl.BlockSpec((SLOT_TILE, SLAB), lambda i: (i, 0)),
        compiler_params=pltpu.CompilerParams(dimension_semantics=("arbitrary",)),
        name="moe_slots",
    )(slab, pstart.astype(F32).reshape(1, N_EXPERTS))


def _token_rows(t):
    return pl.ds(pl.multiple_of(t * ROW_TILES, ROW_TILES), ROW_TILES)


def _dispatch_body(dst_ref, pstart_ref, pend_ref, h2t_ref, xs_hbm, zero_ref, sem, zsem):
    i = pl.program_id(0)
    blk_rows = MOE_BLK * ROW_TILES
    n_blocks = xs_hbm.shape[0] // blk_rows

    @pl.when(i == 0)
    def _():
        zero_ref[...] = jnp.zeros_like(zero_ref)

        def clear(blk):
            dst = xs_hbm.at[pl.ds(pl.multiple_of(blk * blk_rows, blk_rows), blk_rows)]
            return pltpu.make_async_copy(zero_ref, dst, zsem)

        def each(fn):
            def fill(e, carry):
                @pl.when(pend_ref[e] > pstart_ref[e])
                def _():
                    fn(clear(pend_ref[e] // MOE_BLK - 1))
                return carry
            lax.fori_loop(0, N_EXPERTS, fill, 0)

            def tail(b, carry):
                fn(clear(b))
                return carry
            lax.fori_loop(pend_ref[N_EXPERTS - 1] // MOE_BLK, n_blocks, tail, 0)

        each(lambda cp: cp.start())
        each(lambda cp: cp.wait())

    def issue(t, carry):
        for k in range(TOP_K):
            dst = dst_ref[0, 0, t * TOP_K + k]
            pltpu.make_async_copy(h2t_ref.at[_token_rows(t)], xs_hbm.at[_token_rows(dst)],
                                  sem).start(priority=k % 2)
        return carry
    lax.fori_loop(0, TOK_TILE, issue, 0)
    for _ in range(TOP_K):
        pltpu.make_async_copy(h2t_ref, xs_hbm.at[pl.ds(0, TOK_TILE * ROW_TILES)], sem).wait()


def _dispatch(dst, pstart, pend, h2t, n_rows):
    n_tiles = h2t.shape[0] // (TOK_TILE * ROW_TILES)
    smem_tile = pl.BlockSpec((1, 1, TOK_TILE * TOP_K), lambda i: (i, 0, 0), memory_space=pltpu.SMEM)
    smem_all = pl.BlockSpec(memory_space=pltpu.SMEM)
    return pl.pallas_call(
        _dispatch_body,
        out_shape=jax.ShapeDtypeStruct((n_rows * ROW_TILES, LANES), F32),
        grid=(n_tiles,),
        in_specs=[smem_tile, smem_all, smem_all,
                  pl.BlockSpec((TOK_TILE * ROW_TILES, LANES), lambda i: (i, 0))],
        out_specs=pl.BlockSpec(memory_space=pl.ANY),
        scratch_shapes=[pltpu.VMEM((MOE_BLK * ROW_TILES, LANES), F32),
                        pltpu.SemaphoreType.DMA, pltpu.SemaphoreType.DMA],
        compiler_params=pltpu.CompilerParams(dimension_semantics=("arbitrary",)),
        name="moe_dispatch",
    )(dst.reshape(n_tiles, 1, TOK_TILE * TOP_K), pstart, pend, h2t)


def _experts_body(bexp_ref, nused_ref, xs_ref, wg_ref, wu_ref, wd_ref, ys_ref, wgu_scr, wd_scr):
    b = pl.program_id(0)
    used = b < nused_ref[0]
    new_expert = jnp.logical_or(b == 0, bexp_ref[b] != bexp_ref[jnp.maximum(b - 1, 0)])

    @pl.when(jnp.logical_and(used, new_expert))
    def _():
        wgu_scr[:, :D_EXPERT] = wg_ref[0].astype(BF16)
        wgu_scr[:, D_EXPERT:] = wu_ref[0].astype(BF16)
        wd_scr[...] = wd_ref[0].astype(BF16)

    @pl.when(used)
    def _():
        xb = jnp.concatenate([xs_ref[_tile_rows(j, MOE_BLK), :] for j in range(ROW_TILES)],
                             axis=1).astype(BF16)
        gu = _dot(xb, wgu_scr[...])
        act = (jax.nn.silu(gu[:, :D_EXPERT]) * gu[:, D_EXPERT:]).astype(BF16)
        y = _dot(act, wd_scr[...])
        for j in range(ROW_TILES):
            ys_ref[_tile_rows(j, MOE_BLK), :] = y[:, j * LANES:(j + 1) * LANES]

    @pl.when(jnp.logical_not(used))
    def _():
        ys_ref[...] = jnp.zeros_like(ys_ref)


def _experts(blk_exp, n_used, xs, w_gate, w_up, w_down):
    blk_rows = MOE_BLK * ROW_TILES
    n_blocks = xs.shape[0] // blk_rows

    def row_map(b, bexp, nused):
        return (jnp.minimum(b, nused[0] - 1), 0)

    def out_map(b, bexp, nused):
        return (b, 0)

    def w_map(b, bexp, nused):
        return (bexp[jnp.minimum(b, nused[0] - 1)], 0, 0)

    return pl.pallas_call(
        _experts_body,
        out_shape=jax.ShapeDtypeStruct(xs.shape, F32),
        grid_spec=pltpu.PrefetchScalarGridSpec(
            num_scalar_prefetch=2,
            grid=(n_blocks,),
            in_specs=[pl.BlockSpec((blk_rows, LANES), row_map),
                      pl.BlockSpec((1, D_MODEL, D_EXPERT), w_map),
                      pl.BlockSpec((1, D_MODEL, D_EXPERT), w_map),
                      pl.BlockSpec((1, D_EXPERT, D_MODEL), w_map)],
            out_specs=pl.BlockSpec((blk_rows, LANES), out_map),
            scratch_shapes=[pltpu.VMEM((D_MODEL, 2 * D_EXPERT), BF16),
                            pltpu.VMEM((D_EXPERT, D_MODEL), BF16)]),
        compiler_params=pltpu.CompilerParams(dimension_semantics=("arbitrary",),
                                             vmem_limit_bytes=VMEM_LIMIT),
        name="moe_experts",
    )(blk_exp, n_used, xs, w_gate, w_up, w_down)


def _combine_body(n_prompt_tiles, n_tiles, dst_ref, dnext_ref, xs1_ref, g2_ref, wts_ref, fg_ref,
                  ys_hbm, yp_ref, ysm_ref, buf_ref, sems):
    i = pl.program_id(0)

    def issue(d_ref, slot):
        def body(t, carry):
            for k in range(TOP_K):
                src = d_ref[0, 0, t * TOP_K + k]
                pltpu.make_async_copy(ys_hbm.at[_token_rows(src)], buf_ref.at[slot, k, _token_rows(t)],
                                      sems.at[slot]).start(priority=k % 2)
            return carry
        lax.fori_loop(0, TOK_TILE, body, 0)

    def drain(slot):
        for k in range(TOP_K):
            pltpu.make_async_copy(ys_hbm.at[pl.ds(0, TOK_TILE * ROW_TILES)], buf_ref.at[slot, k],
                                  sems.at[slot]).wait()

    @pl.when(i == 0)
    def _():
        issue(dst_ref, 0)

    def step(slot):
        @pl.when(i + 1 < n_tiles)
        def _():
            issue(dnext_ref, 1 - slot)
        drain(slot)
        wts = wts_ref[...]
        parts = []
        for j in range(ROW_TILES):
            acc = wts[:, 0:1] * buf_ref[slot, 0, _tile_rows(j, TOK_TILE), :]
            for k in range(1, TOP_K):
                acc = acc + wts[:, k:k + 1] * buf_ref[slot, k, _tile_rows(j, TOK_TILE), :]
            parts.append(acc)
        seg = TOK_TILE // g2_ref.shape[0]
        g2 = jnp.concatenate(
            [jnp.broadcast_to(g2_ref[s], (seg, D_MODEL)) for s in range(g2_ref.shape[0])], axis=0)
        x = xs1_ref[...] + g2 * jnp.concatenate(parts, axis=1)
        y = (x * _rms(x)) * fg_ref[...]

        @pl.when(i < n_prompt_tiles)
        def _():
            yp_ref[...] = y

        @pl.when(i >= n_prompt_tiles)
        def _():
            ysm_ref[...] = y

    for slot in range(2):
        pl.when(lax.rem(i, 2) == slot)(functools.partial(step, slot))


def _combine(dst, xs1, g2_blocks, wts, final_g, ys, n_p):
    n_tok = xs1.shape[0]
    n_tiles = n_tok // TOK_TILE
    n_pt = n_p // TOK_TILE
    segs = TOK_TILE // CHUNK
    smem_tile = pl.BlockSpec((1, 1, TOK_TILE * TOP_K), lambda i: (i, 0, 0), memory_space=pltpu.SMEM)
    smem_next = pl.BlockSpec((1, 1, TOK_TILE * TOP_K), lambda i: (jnp.minimum(i + 1, n_tiles - 1), 0, 0),
                             memory_space=pltpu.SMEM)
    dst = dst.reshape(n_tiles, 1, TOK_TILE * TOP_K)
    return pl.pallas_call(
        functools.partial(_combine_body, n_pt, n_tiles),
        out_shape=[jax.ShapeDtypeStruct((n_p, D_MODEL), F32),
                   jax.ShapeDtypeStruct((n_tok - n_p, D_MODEL), F32)],
        grid=(n_tiles,),
        in_specs=[smem_tile, smem_next,
                  pl.BlockSpec((TOK_TILE, D_MODEL), lambda i: (i, 0)),
                  pl.BlockSpec((segs, 1, D_MODEL), lambda i: (i, 0, 0)),
                  pl.BlockSpec((TOK_TILE, TOP_K), lambda i: (i, 0)),
                  pl.BlockSpec((1, D_MODEL), lambda i: (0, 0)),
                  pl.BlockSpec(memory_space=pl.ANY)],
        out_specs=[pl.BlockSpec((TOK_TILE, D_MODEL), lambda i: (jnp.minimum(i, n_pt - 1), 0)),
                   pl.BlockSpec((TOK_TILE, D_MODEL), lambda i: (jnp.maximum(i - n_pt, 0), 0))],
        scratch_shapes=[pltpu.VMEM((2, TOP_K, TOK_TILE * ROW_TILES, LANES), F32),
                        pltpu.SemaphoreType.DMA((2,))],
        compiler_params=pltpu.CompilerParams(dimension_semantics=("arbitrary",),
                                             vmem_limit_bytes=VMEM_LIMIT),
        name="moe_combine",
    )(dst, dst, xs1, g2_blocks, wts, final_g.reshape(1, D_MODEL), ys)


def kernel(x_prompt, x_sample, c_prompt, c_sample, state_ssm_re, state_ssm_im, norm1_g, norm2_g,
           w_ada, b_ada, w_in, w_s, b_s, g_v, lam_re, lam_im, log_dt, b_re, b_im, c_re, c_im,
           d_skip, w_glu, b_glu, out_g_a, out_g_b, w_out, w_router, router_bias, w_gate, w_up,
           w_down, ws_gate, ws_up, ws_down, final_g):
    assert norm1_g.shape[0] == 1
    bp, sp, _ = x_prompt.shape
    bs, ss, _ = x_sample.shape
    n_p, n_s = bp * sp, bs * ss
    n_tok = n_p + n_s
    l = 0

    mod = _adaln(jnp.concatenate([c_prompt, c_sample], axis=0), w_ada[l], b_ada[l])
    mod = mod.reshape(bp + bs, 6, D_MODEL)

    wb, wc, tabs = _s5_tables(lam_re[l], lam_im[l], log_dt[l], b_re[l], b_im[l], c_re[l], c_im[l])
    pos = jnp.arange(SGU_LEN)
    mask = (pos[:, None] // CHUNK) >= (pos[None, :] // CHUNK)
    ws_masked = jnp.where(mask[None], w_s[l], 0.0)
    row = lax.broadcasted_iota(I32, (MIX_ROWS, MIX_ROWS), 0)
    col = lax.broadcasted_iota(I32, (MIX_ROWS, MIX_ROWS), 1)
    ltri = (col < row).astype(BF16)

    def sgu_weights(ln):
        bsf = jnp.repeat(b_s[l][:, :ln].T, HEAD_A, axis=1)
        return ws_masked[:, :ln, :ln].astype(BF16), bsf

    ws_p, bsf_p = sgu_weights(SGU_LEN)
    ws_s, bsf_s = sgu_weights(ss)
    shared = [norm1_g[l].reshape(1, D_MODEL), w_in[l].astype(BF16), g_v[l].reshape(1, D_A),
              wb, wc, tabs, d_skip[l].reshape(1, D_B), w_glu[l].astype(BF16),
              b_glu[l].reshape(1, D_B), out_g_a[l].reshape(1, D_A), out_g_b[l].reshape(1, D_B),
              w_out[l].astype(BF16), norm2_g[l].reshape(1, D_MODEL), w_router[l].astype(BF16),
              router_bias[l].reshape(1, N_EXPERTS), ws_gate[l].astype(BF16),
              ws_up[l].astype(BF16), ws_down[l].astype(BF16), ltri]
    h0_s = jnp.broadcast_to(_lay(state_ssm_re[l], state_ssm_im[l])[:, None, :], (bs, SUB, STATE_COLS))
    xs1, h2t, slab, wslab, cnt_all, hfin_p, hfin_s, v_rows = _mix(
        x_prompt, x_sample, mod, h0_s, ws_p, bsf_p, ws_s, bsf_s, shared)

    wts = wslab[:, :TOP_K]
    counts = cnt_all[0].astype(I32)
    padded = (counts + MOE_BLK - 1) // MOE_BLK * MOE_BLK
    pend = jnp.cumsum(padded).astype(I32)
    pstart = pend - padded
    n_blocks = -(-n_tok * TOP_K // MOE_BLK) + N_EXPERTS
    n_used = (pend[-1:] // MOE_BLK).astype(I32)
    blk_start = jnp.arange(n_blocks, dtype=I32) * MOE_BLK
    blk_exp = jnp.sum((pend[None, :] <= blk_start[:, None]).astype(I32), axis=1)
    blk_exp = jnp.minimum(blk_exp, N_EXPERTS - 1)

    dst = _slots(slab, pstart)[:, :TOP_K]
    xs = _dispatch(dst, pstart, pend, h2t, n_blocks * MOE_BLK)
    ys = _experts(blk_exp, n_used, xs, w_gate[l], w_up[l], w_down[l])

    g2 = mod[:, 5, :]
    g2_blocks = jnp.concatenate([jnp.repeat(g2[:bp], sp // CHUNK, axis=0),
                                 jnp.repeat(g2[bp:], ss // CHUNK, axis=0)], axis=0)
    y_p, y_s = _combine(dst, xs1, g2_blocks.reshape(n_tok // CHUNK, 1, D_MODEL), wts, final_g, ys, n_p)

    re_p, im_p = _unlay(hfin_p[:, 0, :])
    re_s, im_s = _unlay(hfin_s[:, 0, :])
    return (y_p.reshape(bp, sp, D_MODEL), y_s.reshape(bs, ss, D_MODEL),
            re_p[None], im_p[None], re_s[None], im_s[None], v_rows[None])
```

```python
import functools

import jax
import jax.numpy as jnp
from jax import lax
from jax.experimental import pallas as pl
from jax.experimental.pallas import tpu as pltpu

F32 = jnp.float32
BF16 = jnp.bfloat16
I32 = jnp.int32

D_MODEL = 1024
D_A = 512
D_B = 512
N_HEADS_A = 4
HEAD_A = 128
SSM_GROUP = 16
N_GROUPS_B = 32
SSM_STATE = 64
N_EXPERTS = 256
TOP_K = 8
D_EXPERT = 256
ROUTE_SCALE = 2.5
CHUNK = 64
SGU_LEN = 128
EPS = 1e-6

STATE_CHUNKS = 4
GROUPS_PER_CHUNK = N_GROUPS_B // STATE_CHUNKS
HALF = GROUPS_PER_CHUNK * SSM_STATE
STATE_COLS = STATE_CHUNKS * 2 * HALF
SCAN_W = 256
SUB = 8
LANES = 128
ROW_TILES = D_MODEL // LANES

TAB_PW = 0
TAB_MD = 8
TAB_P8 = 11

MIX_ROWS = 256
MOE_BLK = 256
IN_BUFS = 3
OUT_BUFS = 2
TOK_TILE = 256
SLAB = 128
VMEM_LIMIT = 56 * 1024 * 1024


def _dot(a, b):
    return jnp.dot(a, b, preferred_element_type=F32)


def _rms(x):
    return lax.rsqrt(jnp.mean(x * x, axis=-1, keepdims=True) + EPS)


def _tile_rows(j, n):
    return pl.ds(j, n, stride=ROW_TILES)


def _adaln_body(c_ref, w_ref, b_ref, o_ref):
    c = c_ref[...]
    o_ref[...] = _dot(jax.nn.silu(c).astype(BF16), w_ref[...].astype(BF16)) + b_ref[...]


def _adaln(c, w_ada, b_ada):
    n = c.shape[0]
    cols = w_ada.shape[1]
    blk = 1536
    return pl.pallas_call(
        _adaln_body,
        out_shape=jax.ShapeDtypeStruct((n, cols), F32),
        grid=(cols // blk,),
        in_specs=[pl.BlockSpec((n, D_MODEL), lambda j: (0, 0)),
                  pl.BlockSpec((D_MODEL, blk), lambda j: (0, j)),
                  pl.BlockSpec((1, blk), lambda j: (0, j))],
        out_specs=pl.BlockSpec((n, blk), lambda j: (0, j)),
        compiler_params=pltpu.CompilerParams(dimension_semantics=("arbitrary",)),
        name="adaln",
    )(c, w_ada, b_ada.reshape(1, cols))


def _lay(re, im):
    lead = re.shape[:-2]
    re = re.reshape(lead + (STATE_CHUNKS, HALF))
    im = im.reshape(lead + (STATE_CHUNKS, HALF))
    return jnp.concatenate([re, im], axis=-1).reshape(lead + (STATE_COLS,))


def _unlay(v):
    lead = v.shape[:-1]
    v = v.reshape(lead + (STATE_CHUNKS, 2, HALF))
    re = v[..., 0, :].reshape(lead + (N_GROUPS_B, SSM_STATE))
    im = v[..., 1, :].reshape(lead + (N_GROUPS_B, SSM_STATE))
    return re, im


def _s5_tables(lam_re, lam_im, log_dt, b_re, b_im, c_re, c_im):
    dt = jnp.exp(log_dt.astype(F32))[:, None]
    lr, li = lam_re.astype(F32), lam_im.astype(F32)

    def apow(k):
        mag = jnp.exp(lr * dt * k)
        return mag * jnp.cos(li * dt * k), mag * jnp.sin(li * dt * k)

    ar, ai = apow(1.0)
    den = lr * lr + li * li
    nr, ni = ar - 1.0, ai
    kr, ki = (nr * lr + ni * li) / den, (ni * lr - nr * li) / den
    br, bi = b_re.astype(F32), b_im.astype(F32)
    bbr = kr[..., None] * br - ki[..., None] * bi
    bbi = kr[..., None] * bi + ki[..., None] * br
    eye = jnp.eye(GROUPS_PER_CHUNK, dtype=F32)

    def bproj(bb):
        bb = bb.reshape(STATE_CHUNKS, GROUPS_PER_CHUNK, SSM_STATE, SSM_GROUP)
        w = jnp.einsum("mgph,gk->mghkp", bb, eye)
        return w.reshape(STATE_CHUNKS, GROUPS_PER_CHUNK * SSM_GROUP, HALF)

    wb = jnp.concatenate([bproj(bbr), bproj(bbi)], axis=-1).astype(BF16)

    def cproj(cc):
        cc = cc.reshape(STATE_CHUNKS, GROUPS_PER_CHUNK, SSM_GROUP, SSM_STATE)
        w = jnp.einsum("mghp,gk->mgpkh", cc, eye)
        return w.reshape(STATE_CHUNKS, HALF, GROUPS_PER_CHUNK * SSM_GROUP)

    wc = jnp.concatenate([cproj(c_re.astype(F32)), cproj(-c_im.astype(F32))], axis=1).astype(BF16)

    rows = jnp.arange(SUB, dtype=F32)
    tabs = []
    for i in range(SUB):
        pr, pi = apow(float(i + 1))
        tabs.append(jnp.broadcast_to(_lay(pr, pi)[None], (SUB, STATE_COLS)))
    for d in (1, 2, 4):
        pr, pi = apow(float(SUB * d))
        keep = (rows >= d).astype(F32)[:, None]
        tabs.append(_lay(pr, pi)[None] * keep)
    pr, pi = apow(SUB * rows[:, None, None])
    tabs.append(_lay(pr, pi))
    return wb, wc, jnp.stack(tabs)


def _cmul(ar, ai, br, bi):
    return ar * br - ai * bi, ar * bi + ai * br


def _s5_scan_block(bu_ref, row0, h_ref, tab_ref):
    row_id = lax.broadcasted_iota(I32, (SUB, SCAN_W), 0)
    tiles = SCAN_W // LANES
    for m in range(STATE_CHUNKS):
        for hf in range(HALF // SCAN_W):
            c_re0 = m * 2 * HALF + hf * SCAN_W
            c_im0 = c_re0 + HALF
            cre = pl.ds(c_re0, SCAN_W)
            cim = pl.ds(c_im0, SCAN_W)

            def tab(slot):
                return tab_ref[slot, :, cre], tab_ref[slot, :, cim]

            def load(i, c0):
                rows = pl.ds(row0 + i, SUB, stride=SUB)
                return jnp.concatenate([bu_ref[c0 // LANES + j, rows, :] for j in range(tiles)], axis=1)

            def store(i, c0, val):
                rows = pl.ds(row0 + i, SUB, stride=SUB)
                for j in range(tiles):
                    bu_ref[c0 // LANES + j, rows, :] = val[:, j * LANES:(j + 1) * LANES]

            a_re, a_im = tab(TAB_PW)
            s_re = load(0, c_re0)
            s_im = load(0, c_im0)
            loc = [(s_re, s_im)]
            for i in range(1, SUB):
                p_re, p_im = _cmul(a_re, a_im, s_re, s_im)
                s_re = p_re + load(i, c_re0)
                s_im = p_im + load(i, c_im0)
                loc.append((s_re, s_im))
            e_re, e_im = s_re, s_im
            for n, d in enumerate((1, 2, 4)):
                m_re, m_im = tab(TAB_MD + n)
                q_re, q_im = _cmul(m_re, m_im, pltpu.roll(e_re, d, 0), pltpu.roll(e_im, d, 0))
                e_re, e_im = e_re + q_re, e_im + q_im
            p8_re, p8_im = tab(TAB_P8)
            c_re, c_im = _cmul(p8_re, p8_im, h_ref[:, cre], h_ref[:, cim])
            c_re = c_re + jnp.where(row_id >= 1, pltpu.roll(e_re, 1, 0), 0.0)
            c_im = c_im + jnp.where(row_id >= 1, pltpu.roll(e_im, 1, 0), 0.0)
            for i in range(SUB):
                w_re, w_im = tab(TAB_PW + i)
                q_re, q_im = _cmul(w_re, w_im, c_re, c_im)
                f_re, f_im = loc[i][0] + q_re, loc[i][1] + q_im
                store(i, c_re0, f_re)
                store(i, c_im0, f_im)
            h_ref[:, cre] = jnp.broadcast_to(f_re[SUB - 1:SUB, :], (SUB, SCAN_W))
            h_ref[:, cim] = jnp.broadcast_to(f_im[SUB - 1:SUB, :], (SUB, SCAN_W))


def _mix_tile(seqs, ln, chain, new_seq, x_ref, mod_ref, h0_ref, ws_ref, bsf_ref, shared, outs,
              hfin_ref, v_ref, scratch):
    (n1g_ref, win_ref, gv_ref, wb_ref, wc_ref, tab_ref, dsk_ref, wglu_ref, bglu_ref, oga_ref,
     ogb_ref, wout_ref, n2g_ref, wr_ref, rb_ref, wsg_ref, wsu_ref, wsd_ref, ltri_ref) = shared
    xs1_ref, h2t_ref, slab_ref, wslab_ref, cnt_ref = outs
    bu_ref, h_scr, cnt_scr = scratch
    t_rows = MIX_ROWS
    rows_per_seq = t_rows // seqs

    x = x_ref[...].reshape(t_rows, D_MODEL)

    def modrow(j):
        parts = [jnp.broadcast_to(mod_ref[s, j:j + 1, :], (rows_per_seq, D_MODEL)) for s in range(seqs)]
        return parts[0] if seqs == 1 else jnp.concatenate(parts, axis=0)

    h = (x * _rms(x)) * n1g_ref[...] * (1.0 + modrow(1)) + modrow(0)
    z = _dot(h.astype(BF16), win_ref[...])
    u_a = z[:, :D_A]
    v_a = z[:, D_A:2 * D_A]
    u_b = z[:, 2 * D_A:]

    vh_parts = []
    for hh in range(N_HEADS_A):
        cols = slice(hh * HEAD_A, (hh + 1) * HEAD_A)
        vv = v_a[:, cols]
        vh_parts.append((vv * _rms(vv)) * gv_ref[:, cols])
    if v_ref is not None:
        v_ref[...] = jnp.concatenate(vh_parts, axis=1).reshape(seqs, rows_per_seq, D_A)
    n_sgu = t_rows // ln
    s_parts = []
    for hh in range(N_HEADS_A):
        vb = vh_parts[hh].astype(BF16)
        w_h = ws_ref[hh]
        s_parts.append(jnp.concatenate(
            [_dot(w_h, vb[c * ln:(c + 1) * ln, :]) for c in range(n_sgu)], axis=0))
    bsf = bsf_ref[...]
    s_mix = jnp.concatenate(s_parts, axis=1) + jnp.concatenate([bsf] * n_sgu, axis=0)
    y_a = u_a * s_mix

    ub16 = u_b.astype(BF16)
    gc = GROUPS_PER_CHUNK * SSM_GROUP
    tiles_per_chunk = 2 * HALF // LANES
    for m in range(STATE_CHUNKS):
        bu = _dot(ub16[:, m * gc:(m + 1) * gc], wb_ref[m])
        for j in range(tiles_per_chunk):
            bu_ref[m * tiles_per_chunk + j] = bu[:, j * LANES:(j + 1) * LANES]
    n_blk = t_rows // CHUNK
    if chain:
        @pl.when(new_seq)
        def _():
            h_scr[...] = jnp.zeros_like(h_scr)
    for blk in range(n_blk):
        if not chain:
            h_scr[...] = h0_ref[blk]
        _s5_scan_block(bu_ref, blk * CHUNK, h_scr, tab_ref)
        if not chain:
            hfin_ref[blk] = h_scr[...]
    if chain:
        hfin_ref[0] = h_scr[...]
    y_parts = []
    for m in range(STATE_CHUNKS):
        st = jnp.concatenate([bu_ref[m * tiles_per_chunk + j] for j in range(tiles_per_chunk)], axis=1)
        y_parts.append(_dot(st.astype(BF16), wc_ref[m]))
    y_s = jnp.concatenate(y_parts, axis=1) + dsk_ref[...] * u_b
    g_b = jax.nn.gelu(y_s)
    y_b = g_b * jax.nn.sigmoid(_dot(g_b.astype(BF16), wglu_ref[...]) + bglu_ref[...])

    na = (y_a * _rms(y_a)) * oga_ref[...]
    nb = (y_b * _rms(y_b)) * ogb_ref[...]
    mix = _dot(jnp.concatenate([na, nb], axis=1).astype(BF16), wout_ref[...])
    x1 = x + modrow(2) * mix

    h2 = (x1 * _rms(x1)) * n2g_ref[...] * (1.0 + modrow(4)) + modrow(3)
    for j in range(ROW_TILES):
        h2t_ref[_tile_rows(j, t_rows), :] = h2[:, j * LANES:(j + 1) * LANES]
    h2b = h2.astype(BF16)
    act = jax.nn.silu(_dot(h2b, wsg_ref[...])) * _dot(h2b, wsu_ref[...])
    shared_out = _dot(act.astype(BF16), wsd_ref[...])
    xs1_ref[...] = x1 + modrow(5) * shared_out

    scores = jax.nn.sigmoid(_dot(h2b, wr_ref[...]))
    lane = lax.broadcasted_iota(I32, (t_rows, N_EXPERTS), 1).astype(F32)
    work = scores + rb_ref[...]
    onehot = jnp.zeros((t_rows, N_EXPERTS), F32)
    idxs, sels = [], []
    for _ in range(TOP_K):
        top = jnp.max(work, axis=-1, keepdims=True)
        idx = jnp.min(jnp.where(work == top, lane, float(N_EXPERTS)), axis=-1, keepdims=True)
        pick = lane == idx
        sels.append(jnp.sum(jnp.where(pick, scores, 0.0), axis=-1, keepdims=True))
        idxs.append(idx)
        work = jnp.where(pick, -jnp.inf, work)
        onehot = jnp.where(pick, 1.0, onehot)
    total = sels[0]
    for k in range(1, TOP_K):
        total = total + sels[k]
    ranktab = _dot(ltri_ref[...], onehot.astype(BF16)) + cnt_scr[0:1, :]
    slab_lane = lax.broadcasted_iota(I32, (t_rows, SLAB), 1)
    slab = jnp.zeros((t_rows, SLAB), I32)
    wslab = jnp.zeros((t_rows, SLAB), F32)
    for k in range(TOP_K):
        rank = jnp.sum(jnp.where(lane == idxs[k], ranktab, 0.0), axis=-1, keepdims=True)
        slab = jnp.where(slab_lane == k, idxs[k].astype(I32), slab)
        slab = jnp.where(slab_lane == TOP_K + k, rank.astype(I32), slab)
        wslab = jnp.where(slab_lane == k, ROUTE_SCALE * sels[k] / total, wslab)
    slab_ref[...] = slab
    wslab_ref[...] = wslab
    cnt_scr[...] = cnt_scr[...] + jnp.sum(onehot, axis=0, keepdims=True)
    cnt_ref[...] = cnt_scr[...]


def _mix_body(n_prompt_tiles, tiles_per_seq, seqs_s, ln_s,
              xp_ref, xsm_ref, modp_ref, mods_ref, h0s_ref, wsp_ref, bsfp_ref, wss_ref, bsfs_ref,
              *rest):
    shared = rest[:19]
    xs1_ref, h2t_ref, slab_ref, wslab_ref, cnt_ref, hfp_ref, hfs_ref, v_ref = rest[19:27]
    scratch = rest[27:]
    outs = (xs1_ref, h2t_ref, slab_ref, wslab_ref, cnt_ref)
    s = pl.program_id(0)

    @pl.when(s == 0)
    def _():
        scratch[2][...] = jnp.zeros_like(scratch[2])

    @pl.when(s < n_prompt_tiles)
    def _():
        _mix_tile(1, SGU_LEN, True, lax.rem(s, tiles_per_seq) == 0, xp_ref, modp_ref, None,
                  wsp_ref, bsfp_ref, shared, outs, hfp_ref, None, scratch)

    @pl.when(s >= n_prompt_tiles)
    def _():
        _mix_tile(seqs_s, ln_s, False, None, xsm_ref, mods_ref, h0s_ref,
                  wss_ref, bsfs_ref, shared, outs, hfs_ref, v_ref, scratch)


def _mix(x_prompt, x_sample, mod, h0_s, ws_p, bsf_p, ws_s, bsf_s, shared):
    bp, sp, _ = x_prompt.shape
    bs, ss, _ = x_sample.shape
    assert sp % MIX_ROWS == 0 and MIX_ROWS % ss == 0 and ss == CHUNK
    seqs_s = MIX_ROWS // ss
    assert bs % seqs_s == 0 and bp % seqs_s == 0
    tiles_per_seq = sp // MIX_ROWS
    n_pt = bp * tiles_per_seq
    n_st = bs // seqs_s
    n_tok = bp * sp + bs * ss

    def p_tile(s):
        return jnp.minimum(s, n_pt - 1)

    def s_tile(s):
        return jnp.maximum(s - n_pt, 0)

    def const(shape):
        nd = len(shape)
        return pl.BlockSpec(shape, lambda s: (0,) * nd)

    in_specs = [
        pl.BlockSpec((1, MIX_ROWS, D_MODEL), lambda s: (p_tile(s) // tiles_per_seq, p_tile(s) % tiles_per_seq, 0)),
        pl.BlockSpec((seqs_s, ss, D_MODEL), lambda s: (s_tile(s), 0, 0)),
        pl.BlockSpec((1, 6, D_MODEL), lambda s: (p_tile(s) // tiles_per_seq, 0, 0)),
        pl.BlockSpec((seqs_s, 6, D_MODEL), lambda s: (bp // seqs_s + s_tile(s), 0, 0)),
        pl.BlockSpec((seqs_s, SUB, STATE_COLS), lambda s: (s_tile(s), 0, 0)),
        const(ws_p.shape), const(bsf_p.shape), const(ws_s.shape), const(bsf_s.shape),
    ] + [const(w.shape) for w in shared]
    out_shape = [
        jax.ShapeDtypeStruct((n_tok, D_MODEL), F32),
        jax.ShapeDtypeStruct((n_tok * ROW_TILES, LANES), F32),
        jax.ShapeDtypeStruct((n_tok, SLAB), I32),
        jax.ShapeDtypeStruct((n_tok, SLAB), F32),
        jax.ShapeDtypeStruct((SUB, N_EXPERTS), F32),
        jax.ShapeDtypeStruct((bp, SUB, STATE_COLS), F32),
        jax.ShapeDtypeStruct((bs, SUB, STATE_COLS), F32),
        jax.ShapeDtypeStruct((bs, ss, D_A), F32),
    ]
    out_specs = [
        pl.BlockSpec((MIX_ROWS, D_MODEL), lambda s: (s, 0)),
        pl.BlockSpec((MIX_ROWS * ROW_TILES, LANES), lambda s: (s, 0)),
        pl.BlockSpec((MIX_ROWS, SLAB), lambda s: (s, 0)),
        pl.BlockSpec((MIX_ROWS, SLAB), lambda s: (s, 0)),
        const((SUB, N_EXPERTS)),
        pl.BlockSpec((1, SUB, STATE_COLS), lambda s: (p_tile(s) // tiles_per_seq, 0, 0)),
        pl.BlockSpec((seqs_s, SUB, STATE_COLS), lambda s: (s_tile(s), 0, 0)),
        pl.BlockSpec((seqs_s, ss, D_A), lambda s: (s_tile(s), 0, 0)),
    ]
    return pl.pallas_call(
        functools.partial(_mix_body, n_pt, tiles_per_seq, seqs_s, ss),
        out_shape=out_shape,
        grid=(n_pt + n_st,),
        in_specs=in_specs,
        out_specs=out_specs,
        scratch_shapes=[pltpu.VMEM((STATE_COLS // LANES, MIX_ROWS, LANES), F32),
                        pltpu.VMEM((SUB, STATE_COLS), F32),
                        pltpu.VMEM((SUB, N_EXPERTS), F32)],
        compiler_params=pltpu.CompilerParams(dimension_semantics=("arbitrary",),
                                             vmem_limit_bytes=VMEM_LIMIT),
        name="mix",
    )(x_prompt, x_sample, mod, mod, h0_s, ws_p, bsf_p, ws_s, bsf_s, *shared)


SLOT_TILE = 1024


def _slots_body(slab_ref, pstart_ref, dst_ref):
    slab = slab_ref[...]
    pstart = pstart_ref[...]
    lane = lax.broadcasted_iota(I32, (SLOT_TILE, N_EXPERTS), 1)
    out_lane = lax.broadcasted_iota(I32, (SLOT_TILE, SLAB), 1)
    out = jnp.zeros((SLOT_TILE, SLAB), I32)
    for k in range(TOP_K):
        start = jnp.sum(jnp.where(lane == slab[:, k:k + 1], pstart, 0.0), axis=-1, keepdims=True)
        out = jnp.where(out_lane == k, start.astype(I32) + slab[:, TOP_K + k:TOP_K + k + 1], out)
    dst_ref[...] = out


def _slots(slab, pstart):
    n_tok = slab.shape[0]
    assert n_tok % SLOT_TILE == 0
    return pl.pallas_call(
        _slots_body,
        out_shape=jax.ShapeDtypeStruct((n_tok, SLAB), I32),
        grid=(n_tok // SLOT_TILE,),
        in_specs=[pl.BlockSpec((SLOT_TILE, SLAB), lambda i: (i, 0)),
                  pl.BlockSpec((1, N_EXPERTS), lambda i: (0, 0))],
        out_specs=pl.BlockSpec((SLOT_TILE, SLAB), lambda i: (i, 0)),
        compiler_params=pltpu.CompilerParams(dimension_semantics=("arbitrary",)),
        name="moe_slots",
    )(slab, pstart.astype(F32).reshape(1, N_EXPERTS))


def _token_rows(t):
    return pl.ds(pl.multiple_of(t * ROW_TILES, ROW_TILES), ROW_TILES)


def _dispatch_body(dst_ref, pstart_ref, pend_ref, h2t_ref, xs_hbm, zero_ref, sem, zsem):
    i = pl.program_id(0)
    blk_rows = MOE_BLK * ROW_TILES
    n_blocks = xs_hbm.shape[0] // blk_rows

    @pl.when(i == 0)
    def _():
        zero_ref[...] = jnp.zeros_like(zero_ref)

        def clear(blk):
            dst = xs_hbm.at[pl.ds(pl.multiple_of(blk * blk_rows, blk_rows), blk_rows)]
            return pltpu.make_async_copy(zero_ref, dst, zsem)

        def each(fn):
            def fill(e, carry):
                @pl.when(pend_ref[e] > pstart_ref[e])
                def _():
                    fn(clear(pend_ref[e] // MOE_BLK - 1))
                return carry
            lax.fori_loop(0, N_EXPERTS, fill, 0)

            def tail(b, carry):
                fn(clear(b))
                return carry
            lax.fori_loop(pend_ref[N_EXPERTS - 1] // MOE_BLK, n_blocks, tail, 0)

        each(lambda cp: cp.start())
        each(lambda cp: cp.wait())

    def issue(t, carry):
        for k in range(TOP_K):
            dst = dst_ref[0, 0, t * TOP_K + k]
            pltpu.make_async_copy(h2t_ref.at[_token_rows(t)], xs_hbm.at[_token_rows(dst)],
                                  sem).start(priority=k % 2)
        return carry
    lax.fori_loop(0, TOK_TILE, issue, 0)
    for _ in range(TOP_K):
        pltpu.make_async_copy(h2t_ref, xs_hbm.at[pl.ds(0, TOK_TILE * ROW_TILES)], sem).wait()


def _dispatch(dst, pstart, pend, h2t, n_rows):
    n_tiles = h2t.shape[0] // (TOK_TILE * ROW_TILES)
    smem_tile = pl.BlockSpec((1, 1, TOK_TILE * TOP_K), lambda i: (i, 0, 0), memory_space=pltpu.SMEM)
    smem_all = pl.BlockSpec(memory_space=pltpu.SMEM)
    return pl.pallas_call(
        _dispatch_body,
        out_shape=jax.ShapeDtypeStruct((n_rows * ROW_TILES, LANES), F32),
        grid=(n_tiles,),
        in_specs=[smem_tile, smem_all, smem_all,
                  pl.BlockSpec((TOK_TILE * ROW_TILES, LANES), lambda i: (i, 0))],
        out_specs=pl.BlockSpec(memory_space=pl.ANY),
        scratch_shapes=[pltpu.VMEM((MOE_BLK * ROW_TILES, LANES), F32),
                        pltpu.SemaphoreType.DMA, pltpu.SemaphoreType.DMA],
        compiler_params=pltpu.CompilerParams(dimension_semantics=("arbitrary",)),
        name="moe_dispatch",
    )(dst.reshape(n_tiles, 1, TOK_TILE * TOP_K), pstart, pend, h2t)


def _experts_body(bexp_ref, nused_ref, xs_hbm, wg_ref, wu_ref, wd_ref, ys_hbm,
                  xbuf, obuf, wgu_scr, wd_scr, in_sems, out_sems):
    b = pl.program_id(0)
    n_blocks = pl.num_programs(0)
    n_used = nused_ref[0]
    used = b < n_used
    blk_rows = MOE_BLK * ROW_TILES

    def block(ref, blk):
        return ref.at[pl.ds(pl.multiple_of(blk * blk_rows, blk_rows), blk_rows)]

    def fetch(blk):
        slot = lax.rem(blk, IN_BUFS)
        return pltpu.make_async_copy(block(xs_hbm, blk), xbuf.at[slot], in_sems.at[slot])

    def put(blk):
        slot = lax.rem(blk, OUT_BUFS)
        return pltpu.make_async_copy(obuf.at[slot], block(ys_hbm, blk), out_sems.at[slot])

    @pl.when(b == 0)
    def _():
        for first in range(IN_BUFS - 1):
            @pl.when(first < n_used)
            def _():
                fetch(first).start()

    @pl.when(b >= OUT_BUFS)
    def _():
        put(b - OUT_BUFS).wait()

    new_expert = jnp.logical_or(b == 0, bexp_ref[b] != bexp_ref[jnp.maximum(b - 1, 0)])

    @pl.when(jnp.logical_and(used, new_expert))
    def _():
        wgu_scr[:, :D_EXPERT] = wg_ref[0].astype(BF16)
        wgu_scr[:, D_EXPERT:] = wu_ref[0].astype(BF16)
        wd_scr[...] = wd_ref[0].astype(BF16)

    out = obuf.at[lax.rem(b, OUT_BUFS)]

    @pl.when(used)
    def _():
        ahead = b + IN_BUFS - 1

        @pl.when(ahead < n_used)
        def _():
            fetch(ahead).start()
        fetch(b).wait()
        rows_in = xbuf.at[lax.rem(b, IN_BUFS)]
        xb = jnp.concatenate([rows_in[_tile_rows(j, MOE_BLK), :] for j in range(ROW_TILES)],
                             axis=1).astype(BF16)
        gu = _dot(xb, wgu_scr[...])
        act = (jax.nn.silu(gu[:, :D_EXPERT]) * gu[:, D_EXPERT:]).astype(BF16)
        y = _dot(act, wd_scr[...])
        for j in range(ROW_TILES):
            out[_tile_rows(j, MOE_BLK), :] = y[:, j * LANES:(j + 1) * LANES]

    @pl.when(jnp.logical_not(used))
    def _():
        out[...] = jnp.zeros((blk_rows, LANES), F32)

    put(b).start()

    @pl.when(b == n_blocks - 1)
    def _():
        for back in range(OUT_BUFS):
            put(b - back).wait()


def _experts(blk_exp, n_used, xs, w_gate, w_up, w_down):
    blk_rows = MOE_BLK * ROW_TILES
    n_blocks = xs.shape[0] // blk_rows
    assert n_blocks >= OUT_BUFS

    def w_map(b, bexp, nused):
        return (bexp[jnp.minimum(b, nused[0] - 1)], 0, 0)

    return pl.pallas_call(
        _experts_body,
        out_shape=jax.ShapeDtypeStruct(xs.shape, F32),
        grid_spec=pltpu.PrefetchScalarGridSpec(
            num_scalar_prefetch=2,
            grid=(n_blocks,),
            in_specs=[pl.BlockSpec(memory_space=pl.ANY),
                      pl.BlockSpec((1, D_MODEL, D_EXPERT), w_map),
                      pl.BlockSpec((1, D_MODEL, D_EXPERT), w_map),
                      pl.BlockSpec((1, D_EXPERT, D_MODEL), w_map)],
            out_specs=pl.BlockSpec(memory_space=pl.ANY),
            scratch_shapes=[pltpu.VMEM((IN_BUFS, blk_rows, LANES), F32),
                            pltpu.VMEM((OUT_BUFS, blk_rows, LANES), F32),
                            pltpu.VMEM((D_MODEL, 2 * D_EXPERT), BF16),
                            pltpu.VMEM((D_EXPERT, D_MODEL), BF16),
                            pltpu.SemaphoreType.DMA((IN_BUFS,)),
                            pltpu.SemaphoreType.DMA((OUT_BUFS,))]),
        compiler_params=pltpu.CompilerParams(dimension_semantics=("arbitrary",),
                                             vmem_limit_bytes=VMEM_LIMIT),
        name="moe_experts",
    )(blk_exp, n_used, xs, w_gate, w_up, w_down)


def _combine_body(n_prompt_tiles, n_tiles, dst_ref, dnext_ref, xs1_ref, g2_ref, wts_ref, fg_ref,
                  ys_hbm, yp_ref, ysm_ref, buf_ref, sems):
    i = pl.program_id(0)

    def issue(d_ref, slot):
        def body(t, carry):
            for k in range(TOP_K):
                src = d_ref[0, 0, t * TOP_K + k]
                pltpu.make_async_copy(ys_hbm.at[_token_rows(src)], buf_ref.at[slot, k, _token_rows(t)],
                                      sems.at[slot]).start(priority=k % 2)
            return carry
        lax.fori_loop(0, TOK_TILE, body, 0)

    def drain(slot):
        for k in range(TOP_K):
            pltpu.make_async_copy(ys_hbm.at[pl.ds(0, TOK_TILE * ROW_TILES)], buf_ref.at[slot, k],
                                  sems.at[slot]).wait()

    @pl.when(i == 0)
    def _():
        issue(dst_ref, 0)

    def step(slot):
        @pl.when(i + 1 < n_tiles)
        def _():
            issue(dnext_ref, 1 - slot)
        drain(slot)
        wts = wts_ref[...]
        parts = []
        for j in range(ROW_TILES):
            acc = wts[:, 0:1] * buf_ref[slot, 0, _tile_rows(j, TOK_TILE), :]
            for k in range(1, TOP_K):
                acc = acc + wts[:, k:k + 1] * buf_ref[slot, k, _tile_rows(j, TOK_TILE), :]
            parts.append(acc)
        seg = TOK_TILE // g2_ref.shape[0]
        g2 = jnp.concatenate(
            [jnp.broadcast_to(g2_ref[s], (seg, D_MODEL)) for s in range(g2_ref.shape[0])], axis=0)
        x = xs1_ref[...] + g2 * jnp.concatenate(parts, axis=1)
        y = (x * _rms(x)) * fg_ref[...]

        @pl.when(i < n_prompt_tiles)
        def _():
            yp_ref[...] = y

        @pl.when(i >= n_prompt_tiles)
        def _():
            ysm_ref[...] = y

    for slot in range(2):
        pl.when(lax.rem(i, 2) == slot)(functools.partial(step, slot))


def _combine(dst, xs1, g2_blocks, wts, final_g, ys, n_p):
    n_tok = xs1.shape[0]
    n_tiles = n_tok // TOK_TILE
    n_pt = n_p // TOK_TILE
    segs = TOK_TILE // CHUNK
    smem_tile = pl.BlockSpec((1, 1, TOK_TILE * TOP_K), lambda i: (i, 0, 0), memory_space=pltpu.SMEM)
    smem_next = pl.BlockSpec((1, 1, TOK_TILE * TOP_K), lambda i: (jnp.minimum(i + 1, n_tiles - 1), 0, 0),
                             memory_space=pltpu.SMEM)
    dst = dst.reshape(n_tiles, 1, TOK_TILE * TOP_K)
    return pl.pallas_call(
        functools.partial(_combine_body, n_pt, n_tiles),
        out_shape=[jax.ShapeDtypeStruct((n_p, D_MODEL), F32),
                   jax.ShapeDtypeStruct((n_tok - n_p, D_MODEL), F32)],
        grid=(n_tiles,),
        in_specs=[smem_tile, smem_next,
                  pl.BlockSpec((TOK_TILE, D_MODEL), lambda i: (i, 0)),
                  pl.BlockSpec((segs, 1, D_MODEL), lambda i: (i, 0, 0)),
                  pl.BlockSpec((TOK_TILE, TOP_K), lambda i: (i, 0)),
                  pl.BlockSpec((1, D_MODEL), lambda i: (0, 0)),
                  pl.BlockSpec(memory_space=pl.ANY)],
        out_specs=[pl.BlockSpec((TOK_TILE, D_MODEL), lambda i: (jnp.minimum(i, n_pt - 1), 0)),
                   pl.BlockSpec((TOK_TILE, D_MODEL), lambda i: (jnp.maximum(i - n_pt, 0), 0))],
        scratch_shapes=[pltpu.VMEM((2, TOP_K, TOK_TILE * ROW_TILES, LANES), F32),
                        pltpu.SemaphoreType.DMA((2,))],
        compiler_params=pltpu.CompilerParams(dimension_semantics=("arbitrary",),
                                             vmem_limit_bytes=VMEM_LIMIT),
        name="moe_combine",
    )(dst, dst, xs1, g2_blocks, wts, final_g.reshape(1, D_MODEL), ys)


def kernel(x_prompt, x_sample, c_prompt, c_sample, state_ssm_re, state_ssm_im, norm1_g, norm2_g,
           w_ada, b_ada, w_in, w_s, b_s, g_v, lam_re, lam_im, log_dt, b_re, b_im, c_re, c_im,
           d_skip, w_glu, b_glu, out_g_a, out_g_b, w_out, w_router, router_bias, w_gate, w_up,
           w_down, ws_gate, ws_up, ws_down, final_g):
    assert norm1_g.shape[0] == 1
    bp, sp, _ = x_prompt.shape
    bs, ss, _ = x_sample.shape
    n_p, n_s = bp * sp, bs * ss
    n_tok = n_p + n_s
    l = 0

    mod = _adaln(jnp.concatenate([c_prompt, c_sample], axis=0), w_ada[l], b_ada[l])
    mod = mod.reshape(bp + bs, 6, D_MODEL)

    wb, wc, tabs = _s5_tables(lam_re[l], lam_im[l], log_dt[l], b_re[l], b_im[l], c_re[l], c_im[l])
    pos = jnp.arange(SGU_LEN)
    mask = (pos[:, None] // CHUNK) >= (pos[None, :] // CHUNK)
    ws_masked = jnp.where(mask[None], w_s[l], 0.0)
    row = lax.broadcasted_iota(I32, (MIX_ROWS, MIX_ROWS), 0)
    col = lax.broadcasted_iota(I32, (MIX_ROWS, MIX_ROWS), 1)
    ltri = (col < row).astype(BF16)

    def sgu_weights(ln):
        bsf = jnp.repeat(b_s[l][:, :ln].T, HEAD_A, axis=1)
        return ws_masked[:, :ln, :ln].astype(BF16), bsf

    ws_p, bsf_p = sgu_weights(SGU_LEN)
    ws_s, bsf_s = sgu_weights(ss)
    shared = [norm1_g[l].reshape(1, D_MODEL), w_in[l].astype(BF16), g_v[l].reshape(1, D_A),
              wb, wc, tabs, d_skip[l].reshape(1, D_B), w_glu[l].astype(BF16),
              b_glu[l].reshape(1, D_B), out_g_a[l].reshape(1, D_A), out_g_b[l].reshape(1, D_B),
              w_out[l].astype(BF16), norm2_g[l].reshape(1, D_MODEL), w_router[l].astype(BF16),
              router_bias[l].reshape(1, N_EXPERTS), ws_gate[l].astype(BF16),
              ws_up[l].astype(BF16), ws_down[l].astype(BF16), ltri]
    h0_s = jnp.broadcast_to(_lay(state_ssm_re[l], state_ssm_im[l])[:, None, :], (bs, SUB, STATE_COLS))
    xs1, h2t, slab, wslab, cnt_all, hfin_p, hfin_s, v_rows = _mix(
        x_prompt, x_sample, mod, h0_s, ws_p, bsf_p, ws_s, bsf_s, shared)

    wts = wslab[:, :TOP_K]
    counts = cnt_all[0].astype(I32)
    padded = (counts + MOE_BLK - 1) // MOE_BLK * MOE_BLK
    pend = jnp.cumsum(padded).astype(I32)
    pstart = pend - padded
    n_blocks = -(-n_tok * TOP_K // MOE_BLK) + N_EXPERTS
    n_used = (pend[-1:] // MOE_BLK).astype(I32)
    blk_start = jnp.arange(n_blocks, dtype=I32) * MOE_BLK
    blk_exp = jnp.sum((pend[None, :] <= blk_start[:, None]).astype(I32), axis=1)
    blk_exp = jnp.minimum(blk_exp, N_EXPERTS - 1)

    dst = _slots(slab, pstart)[:, :TOP_K]
    xs = _dispatch(dst, pstart, pend, h2t, n_blocks * MOE_BLK)
    ys = _experts(blk_exp, n_used, xs, w_gate[l], w_up[l], w_down[l])

    g2 = mod[:, 5, :]
    g2_blocks = jnp.concatenate([jnp.repeat(g2[:bp], sp // CHUNK, axis=0),
                                 jnp.repeat(g2[bp:], ss // CHUNK, axis=0)], axis=0)
    y_p, y_s = _combine(dst, xs1, g2_blocks.reshape(n_tok // CHUNK, 1, D_MODEL), wts, final_g, ys, n_p)

    re_p, im_p = _unlay(hfin_p[:, 0, :])
    re_s, im_s = _unlay(hfin_s[:, 0, :])
    return (y_p.reshape(bp, sp, D_MODEL), y_s.reshape(bs, ss, D_MODEL),
            re_p[None], im_p[None], re_s[None], im_s[None], v_rows[None])
```

```python
import functools

import jax
import jax.numpy as jnp
from jax import lax
from jax.experimental import pallas as pl
from jax.experimental.pallas import tpu as pltpu

F32 = jnp.float32
BF16 = jnp.bfloat16
I32 = jnp.int32

D_MODEL = 1024
D_A = 512
D_B = 512
N_HEADS_A = 4
HEAD_A = 128
SSM_GROUP = 16
N_GROUPS_B = 32
SSM_STATE = 64
N_EXPERTS = 256
TOP_K = 8
D_EXPERT = 256
ROUTE_SCALE = 2.5
CHUNK = 64
SGU_LEN = 128
EPS = 1e-6

STATE_CHUNKS = 4
GROUPS_PER_CHUNK = N_GROUPS_B // STATE_CHUNKS
HALF = GROUPS_PER_CHUNK * SSM_STATE
STATE_COLS = STATE_CHUNKS * 2 * HALF
SCAN_W = 256
SUB = 8
LANES = 128
ROW_TILES = D_MODEL // LANES

TAB_PW = 0
TAB_MD = 8
TAB_P8 = 11

MIX_ROWS = 256
MOE_BLK = 256
DISPATCH_BUFS = 3
IN_BUFS = 3
OUT_BUFS = 2
TOK_TILE = 256
SLAB = 128
VMEM_LIMIT = 56 * 1024 * 1024


def _dot(a, b):
    return jnp.dot(a, b, preferred_element_type=F32)


def _rms(x):
    return lax.rsqrt(jnp.mean(x * x, axis=-1, keepdims=True) + EPS)


def _tile_rows(j, n):
    return pl.ds(j, n, stride=ROW_TILES)


def _adaln_body(c_ref, w_ref, b_ref, o_ref):
    c = c_ref[...]
    o_ref[...] = _dot(jax.nn.silu(c).astype(BF16), w_ref[...].astype(BF16)) + b_ref[...]


def _adaln(c, w_ada, b_ada):
    n = c.shape[0]
    cols = w_ada.shape[1]
    blk = 1536
    return pl.pallas_call(
        _adaln_body,
        out_shape=jax.ShapeDtypeStruct((n, cols), F32),
        grid=(cols // blk,),
        in_specs=[pl.BlockSpec((n, D_MODEL), lambda j: (0, 0)),
                  pl.BlockSpec((D_MODEL, blk), lambda j: (0, j)),
                  pl.BlockSpec((1, blk), lambda j: (0, j))],
        out_specs=pl.BlockSpec((n, blk), lambda j: (0, j)),
        compiler_params=pltpu.CompilerParams(dimension_semantics=("arbitrary",)),
        name="adaln",
    )(c, w_ada, b_ada.reshape(1, cols))


def _lay(re, im):
    lead = re.shape[:-2]
    re = re.reshape(lead + (STATE_CHUNKS, HALF))
    im = im.reshape(lead + (STATE_CHUNKS, HALF))
    return jnp.concatenate([re, im], axis=-1).reshape(lead + (STATE_COLS,))


def _unlay(v):
    lead = v.shape[:-1]
    v = v.reshape(lead + (STATE_CHUNKS, 2, HALF))
    re = v[..., 0, :].reshape(lead + (N_GROUPS_B, SSM_STATE))
    im = v[..., 1, :].reshape(lead + (N_GROUPS_B, SSM_STATE))
    return re, im


def _s5_tables(lam_re, lam_im, log_dt, b_re, b_im, c_re, c_im):
    dt = jnp.exp(log_dt.astype(F32))[:, None]
    lr, li = lam_re.astype(F32), lam_im.astype(F32)

    def apow(k):
        mag = jnp.exp(lr * dt * k)
        return mag * jnp.cos(li * dt * k), mag * jnp.sin(li * dt * k)

    ar, ai = apow(1.0)
    den = lr * lr + li * li
    nr, ni = ar - 1.0, ai
    kr, ki = (nr * lr + ni * li) / den, (ni * lr - nr * li) / den
    br, bi = b_re.astype(F32), b_im.astype(F32)
    bbr = kr[..., None] * br - ki[..., None] * bi
    bbi = kr[..., None] * bi + ki[..., None] * br
    eye = jnp.eye(GROUPS_PER_CHUNK, dtype=F32)

    def bproj(bb):
        bb = bb.reshape(STATE_CHUNKS, GROUPS_PER_CHUNK, SSM_STATE, SSM_GROUP)
        w = jnp.einsum("mgph,gk->mghkp", bb, eye)
        return w.reshape(STATE_CHUNKS, GROUPS_PER_CHUNK * SSM_GROUP, HALF)

    wb = jnp.concatenate([bproj(bbr), bproj(bbi)], axis=-1).astype(BF16)

    def cproj(cc):
        cc = cc.reshape(STATE_CHUNKS, GROUPS_PER_CHUNK, SSM_GROUP, SSM_STATE)
        w = jnp.einsum("mghp,gk->mgpkh", cc, eye)
        return w.reshape(STATE_CHUNKS, HALF, GROUPS_PER_CHUNK * SSM_GROUP)

    wc = jnp.concatenate([cproj(c_re.astype(F32)), cproj(-c_im.astype(F32))], axis=1).astype(BF16)

    rows = jnp.arange(SUB, dtype=F32)
    tabs = []
    for i in range(SUB):
        pr, pi = apow(float(i + 1))
        tabs.append(jnp.broadcast_to(_lay(pr, pi)[None], (SUB, STATE_COLS)))
    for d in (1, 2, 4):
        pr, pi = apow(float(SUB * d))
        keep = (rows >= d).astype(F32)[:, None]
        tabs.append(_lay(pr, pi)[None] * keep)
    pr, pi = apow(SUB * rows[:, None, None])
    tabs.append(_lay(pr, pi))
    return wb, wc, jnp.stack(tabs)


def _cmul(ar, ai, br, bi):
    return ar * br - ai * bi, ar * bi + ai * br


def _s5_scan_block(bu_ref, row0, h_ref, tab_ref):
    row_id = lax.broadcasted_iota(I32, (SUB, SCAN_W), 0)
    tiles = SCAN_W // LANES
    for m in range(STATE_CHUNKS):
        for hf in range(HALF // SCAN_W):
            c_re0 = m * 2 * HALF + hf * SCAN_W
            c_im0 = c_re0 + HALF
            cre = pl.ds(c_re0, SCAN_W)
            cim = pl.ds(c_im0, SCAN_W)

            def tab(slot):
                return tab_ref[slot, :, cre], tab_ref[slot, :, cim]

            def load(i, c0):
                rows = pl.ds(row0 + i, SUB, stride=SUB)
                return jnp.concatenate([bu_ref[c0 // LANES + j, rows, :] for j in range(tiles)], axis=1)

            def store(i, c0, val):
                rows = pl.ds(row0 + i, SUB, stride=SUB)
                for j in range(tiles):
                    bu_ref[c0 // LANES + j, rows, :] = val[:, j * LANES:(j + 1) * LANES]

            a_re, a_im = tab(TAB_PW)
            s_re = load(0, c_re0)
            s_im = load(0, c_im0)
            loc = [(s_re, s_im)]
            for i in range(1, SUB):
                p_re, p_im = _cmul(a_re, a_im, s_re, s_im)
                s_re = p_re + load(i, c_re0)
                s_im = p_im + load(i, c_im0)
                loc.append((s_re, s_im))
            e_re, e_im = s_re, s_im
            for n, d in enumerate((1, 2, 4)):
                m_re, m_im = tab(TAB_MD + n)
                q_re, q_im = _cmul(m_re, m_im, pltpu.roll(e_re, d, 0), pltpu.roll(e_im, d, 0))
                e_re, e_im = e_re + q_re, e_im + q_im
            p8_re, p8_im = tab(TAB_P8)
            c_re, c_im = _cmul(p8_re, p8_im, h_ref[:, cre], h_ref[:, cim])
            c_re = c_re + jnp.where(row_id >= 1, pltpu.roll(e_re, 1, 0), 0.0)
            c_im = c_im + jnp.where(row_id >= 1, pltpu.roll(e_im, 1, 0), 0.0)
            for i in range(SUB):
                w_re, w_im = tab(TAB_PW + i)
                q_re, q_im = _cmul(w_re, w_im, c_re, c_im)
                f_re, f_im = loc[i][0] + q_re, loc[i][1] + q_im
                store(i, c_re0, f_re)
                store(i, c_im0, f_im)
            h_ref[:, cre] = jnp.broadcast_to(f_re[SUB - 1:SUB, :], (SUB, SCAN_W))
            h_ref[:, cim] = jnp.broadcast_to(f_im[SUB - 1:SUB, :], (SUB, SCAN_W))


def _mix_tile(seqs, ln, chain, new_seq, x_ref, mod_ref, h0_ref, ws_ref, bsf_ref, shared, outs,
              hfin_ref, v_ref, scratch):
    (n1g_ref, win_ref, gv_ref, wb_ref, wc_ref, tab_ref, dsk_ref, wglu_ref, bglu_ref, oga_ref,
     ogb_ref, wout_ref, n2g_ref, wr_ref, rb_ref, wsg_ref, wsu_ref, wsd_ref, ltri_ref) = shared
    xs1_ref, h2t_ref, slab_ref, wslab_ref, cnt_ref = outs
    bu_ref, h_scr, cnt_scr = scratch
    t_rows = MIX_ROWS
    rows_per_seq = t_rows // seqs

    x = x_ref[...].reshape(t_rows, D_MODEL)

    def modrow(j):
        parts = [jnp.broadcast_to(mod_ref[s, j:j + 1, :], (rows_per_seq, D_MODEL)) for s in range(seqs)]
        return parts[0] if seqs == 1 else jnp.concatenate(parts, axis=0)

    h = (x * _rms(x)) * n1g_ref[...] * (1.0 + modrow(1)) + modrow(0)
    z = _dot(h.astype(BF16), win_ref[...])
    u_a = z[:, :D_A]
    v_a = z[:, D_A:2 * D_A]
    u_b = z[:, 2 * D_A:]

    vh_parts = []
    for hh in range(N_HEADS_A):
        cols = slice(hh * HEAD_A, (hh + 1) * HEAD_A)
        vv = v_a[:, cols]
        vh_parts.append((vv * _rms(vv)) * gv_ref[:, cols])
    if v_ref is not None:
        v_ref[...] = jnp.concatenate(vh_parts, axis=1).reshape(seqs, rows_per_seq, D_A)
    n_sgu = t_rows // ln
    s_parts = []
    for hh in range(N_HEADS_A):
        vb = vh_parts[hh].astype(BF16)
        w_h = ws_ref[hh]
        s_parts.append(jnp.concatenate(
            [_dot(w_h, vb[c * ln:(c + 1) * ln, :]) for c in range(n_sgu)], axis=0))
    bsf = bsf_ref[...]
    s_mix = jnp.concatenate(s_parts, axis=1) + jnp.concatenate([bsf] * n_sgu, axis=0)
    y_a = u_a * s_mix

    ub16 = u_b.astype(BF16)
    gc = GROUPS_PER_CHUNK * SSM_GROUP
    tiles_per_chunk = 2 * HALF // LANES
    for m in range(STATE_CHUNKS):
        bu = _dot(ub16[:, m * gc:(m + 1) * gc], wb_ref[m])
        for j in range(tiles_per_chunk):
            bu_ref[m * tiles_per_chunk + j] = bu[:, j * LANES:(j + 1) * LANES]
    n_blk = t_rows // CHUNK
    if chain:
        @pl.when(new_seq)
        def _():
            h_scr[...] = jnp.zeros_like(h_scr)
    for blk in range(n_blk):
        if not chain:
            h_scr[...] = h0_ref[blk]
        _s5_scan_block(bu_ref, blk * CHUNK, h_scr, tab_ref)
        if not chain:
            hfin_ref[blk] = h_scr[...]
    if chain:
        hfin_ref[0] = h_scr[...]
    y_parts = []
    for m in range(STATE_CHUNKS):
        st = jnp.concatenate([bu_ref[m * tiles_per_chunk + j] for j in range(tiles_per_chunk)], axis=1)
        y_parts.append(_dot(st.astype(BF16), wc_ref[m]))
    y_s = jnp.concatenate(y_parts, axis=1) + dsk_ref[...] * u_b
    g_b = jax.nn.gelu(y_s)
    y_b = g_b * jax.nn.sigmoid(_dot(g_b.astype(BF16), wglu_ref[...]) + bglu_ref[...])

    na = (y_a * _rms(y_a)) * oga_ref[...]
    nb = (y_b * _rms(y_b)) * ogb_ref[...]
    mix = _dot(jnp.concatenate([na, nb], axis=1).astype(BF16), wout_ref[...])
    x1 = x + modrow(2) * mix

    h2 = (x1 * _rms(x1)) * n2g_ref[...] * (1.0 + modrow(4)) + modrow(3)
    for j in range(ROW_TILES):
        h2t_ref[_tile_rows(j, t_rows), :] = h2[:, j * LANES:(j + 1) * LANES]
    h2b = h2.astype(BF16)
    act = jax.nn.silu(_dot(h2b, wsg_ref[...])) * _dot(h2b, wsu_ref[...])
    shared_out = _dot(act.astype(BF16), wsd_ref[...])
    xs1_ref[...] = x1 + modrow(5) * shared_out

    scores = jax.nn.sigmoid(_dot(h2b, wr_ref[...]))
    lane = lax.broadcasted_iota(I32, (t_rows, N_EXPERTS), 1).astype(F32)
    work = scores + rb_ref[...]
    onehot = jnp.zeros((t_rows, N_EXPERTS), F32)
    idxs, sels = [], []
    for _ in range(TOP_K):
        top = jnp.max(work, axis=-1, keepdims=True)
        idx = jnp.min(jnp.where(work == top, lane, float(N_EXPERTS)), axis=-1, keepdims=True)
        pick = lane == idx
        sels.append(jnp.sum(jnp.where(pick, scores, 0.0), axis=-1, keepdims=True))
        idxs.append(idx)
        work = jnp.where(pick, -jnp.inf, work)
        onehot = jnp.where(pick, 1.0, onehot)
    total = sels[0]
    for k in range(1, TOP_K):
        total = total + sels[k]
    ranktab = _dot(ltri_ref[...], onehot.astype(BF16)) + cnt_scr[0:1, :]
    slab_lane = lax.broadcasted_iota(I32, (t_rows, SLAB), 1)
    slab = jnp.zeros((t_rows, SLAB), I32)
    wslab = jnp.zeros((t_rows, SLAB), F32)
    for k in range(TOP_K):
        rank = jnp.sum(jnp.where(lane == idxs[k], ranktab, 0.0), axis=-1, keepdims=True)
        slab = jnp.where(slab_lane == k, idxs[k].astype(I32), slab)
        slab = jnp.where(slab_lane == TOP_K + k, rank.astype(I32), slab)
        wslab = jnp.where(slab_lane == k, ROUTE_SCALE * sels[k] / total, wslab)
    slab_ref[...] = slab
    wslab_ref[...] = wslab
    cnt_scr[...] = cnt_scr[...] + jnp.sum(onehot, axis=0, keepdims=True)
    cnt_ref[...] = cnt_scr[...]


def _mix_body(n_prompt_tiles, tiles_per_seq, seqs_s, ln_s,
              xp_ref, xsm_ref, modp_ref, mods_ref, h0s_ref, wsp_ref, bsfp_ref, wss_ref, bsfs_ref,
              *rest):
    shared = rest[:19]
    xs1_ref, h2t_ref, slab_ref, wslab_ref, cnt_ref, hfp_ref, hfs_ref, v_ref = rest[19:27]
    scratch = rest[27:]
    outs = (xs1_ref, h2t_ref, slab_ref, wslab_ref, cnt_ref)
    s = pl.program_id(0)

    @pl.when(s == 0)
    def _():
        scratch[2][...] = jnp.zeros_like(scratch[2])

    @pl.when(s < n_prompt_tiles)
    def _():
        _mix_tile(1, SGU_LEN, True, lax.rem(s, tiles_per_seq) == 0, xp_ref, modp_ref, None,
                  wsp_ref, bsfp_ref, shared, outs, hfp_ref, None, scratch)

    @pl.when(s >= n_prompt_tiles)
    def _():
        _mix_tile(seqs_s, ln_s, False, None, xsm_ref, mods_ref, h0s_ref,
                  wss_ref, bsfs_ref, shared, outs, hfs_ref, v_ref, scratch)


def _mix(x_prompt, x_sample, mod, h0_s, ws_p, bsf_p, ws_s, bsf_s, shared):
    bp, sp, _ = x_prompt.shape
    bs, ss, _ = x_sample.shape
    assert sp % MIX_ROWS == 0 and MIX_ROWS % ss == 0 and ss == CHUNK
    seqs_s = MIX_ROWS // ss
    assert bs % seqs_s == 0 and bp % seqs_s == 0
    tiles_per_seq = sp // MIX_ROWS
    n_pt = bp * tiles_per_seq
    n_st = bs // seqs_s
    n_tok = bp * sp + bs * ss

    def p_tile(s):
        return jnp.minimum(s, n_pt - 1)

    def s_tile(s):
        return jnp.maximum(s - n_pt, 0)

    def const(shape):
        nd = len(shape)
        return pl.BlockSpec(shape, lambda s: (0,) * nd)

    in_specs = [
        pl.BlockSpec((1, MIX_ROWS, D_MODEL), lambda s: (p_tile(s) // tiles_per_seq, p_tile(s) % tiles_per_seq, 0)),
        pl.BlockSpec((seqs_s, ss, D_MODEL), lambda s: (s_tile(s), 0, 0)),
        pl.BlockSpec((1, 6, D_MODEL), lambda s: (p_tile(s) // tiles_per_seq, 0, 0)),
        pl.BlockSpec((seqs_s, 6, D_MODEL), lambda s: (bp // seqs_s + s_tile(s), 0, 0)),
        pl.BlockSpec((seqs_s, SUB, STATE_COLS), lambda s: (s_tile(s), 0, 0)),
        const(ws_p.shape), const(bsf_p.shape), const(ws_s.shape), const(bsf_s.shape),
    ] + [const(w.shape) for w in shared]
    out_shape = [
        jax.ShapeDtypeStruct((n_tok, D_MODEL), F32),
        jax.ShapeDtypeStruct((n_tok * ROW_TILES, LANES), F32),
        jax.ShapeDtypeStruct((n_tok, SLAB), I32),
        jax.ShapeDtypeStruct((n_tok, SLAB), F32),
        jax.ShapeDtypeStruct((SUB, N_EXPERTS), F32),
        jax.ShapeDtypeStruct((bp, SUB, STATE_COLS), F32),
        jax.ShapeDtypeStruct((bs, SUB, STATE_COLS), F32),
        jax.ShapeDtypeStruct((bs, ss, D_A), F32),
    ]
    out_specs = [
        pl.BlockSpec((MIX_ROWS, D_MODEL), lambda s: (s, 0)),
        pl.BlockSpec((MIX_ROWS * ROW_TILES, LANES), lambda s: (s, 0)),
        pl.BlockSpec((MIX_ROWS, SLAB), lambda s: (s, 0)),
        pl.BlockSpec((MIX_ROWS, SLAB), lambda s: (s, 0)),
        const((SUB, N_EXPERTS)),
        pl.BlockSpec((1, SUB, STATE_COLS), lambda s: (p_tile(s) // tiles_per_seq, 0, 0)),
        pl.BlockSpec((seqs_s, SUB, STATE_COLS), lambda s: (s_tile(s), 0, 0)),
        pl.BlockSpec((seqs_s, ss, D_A), lambda s: (s_tile(s), 0, 0)),
    ]
    return pl.pallas_call(
        functools.partial(_mix_body, n_pt, tiles_per_seq, seqs_s, ss),
        out_shape=out_shape,
        grid=(n_pt + n_st,),
        in_specs=in_specs,
        out_specs=out_specs,
        scratch_shapes=[pltpu.VMEM((STATE_COLS // LANES, MIX_ROWS, LANES), F32),
                        pltpu.VMEM((SUB, STATE_COLS), F32),
                        pltpu.VMEM((SUB, N_EXPERTS), F32)],
        compiler_params=pltpu.CompilerParams(dimension_semantics=("arbitrary",),
                                             vmem_limit_bytes=VMEM_LIMIT),
        name="mix",
    )(x_prompt, x_sample, mod, mod, h0_s, ws_p, bsf_p, ws_s, bsf_s, *shared)


SLOT_TILE = 1024


def _slots_body(slab_ref, pstart_ref, dst_ref):
    slab = slab_ref[...]
    pstart = pstart_ref[...]
    lane = lax.broadcasted_iota(I32, (SLOT_TILE, N_EXPERTS), 1)
    out_lane = lax.broadcasted_iota(I32, (SLOT_TILE, SLAB), 1)
    out = jnp.zeros((SLOT_TILE, SLAB), I32)
    for k in range(TOP_K):
        start = jnp.sum(jnp.where(lane == slab[:, k:k + 1], pstart, 0.0), axis=-1, keepdims=True)
        out = jnp.where(out_lane == k, start.astype(I32) + slab[:, TOP_K + k:TOP_K + k + 1], out)
    dst_ref[...] = out


def _slots(slab, pstart):
    n_tok = slab.shape[0]
    assert n_tok % SLOT_TILE == 0
    return pl.pallas_call(
        _slots_body,
        out_shape=jax.ShapeDtypeStruct((n_tok, SLAB), I32),
        grid=(n_tok // SLOT_TILE,),
        in_specs=[pl.BlockSpec((SLOT_TILE, SLAB), lambda i: (i, 0)),
                  pl.BlockSpec((1, N_EXPERTS), lambda i: (0, 0))],
        out_specs=pl.BlockSpec((SLOT_TILE, SLAB), lambda i: (i, 0)),
        compiler_params=pltpu.CompilerParams(dimension_semantics=("arbitrary",)),
        name="moe_slots",
    )(slab, pstart.astype(F32).reshape(1, N_EXPERTS))


def _token_rows(t):
    return pl.ds(pl.multiple_of(t * ROW_TILES, ROW_TILES), ROW_TILES)


def _dispatch_body(dst_ref, pstart_ref, pend_ref, h2t_hbm, xs_hbm, tile_ref, zero_ref, in_sems, row_sems,
                   zsem):
    i = pl.program_id(0)
    n_tiles = pl.num_programs(0)
    blk_rows = MOE_BLK * ROW_TILES
    tile_rows = TOK_TILE * ROW_TILES
    n_blocks = xs_hbm.shape[0] // blk_rows

    def fetch(tile):
        slot = lax.rem(tile, DISPATCH_BUFS)
        src = h2t_hbm.at[pl.ds(pl.multiple_of(tile * tile_rows, tile_rows), tile_rows)]
        return pltpu.make_async_copy(src, tile_ref.at[slot], in_sems.at[slot])

    def drain(tile):
        for _ in range(TOP_K):
            pltpu.make_async_copy(tile_ref.at[0], xs_hbm.at[pl.ds(0, tile_rows)],
                                  row_sems.at[lax.rem(tile, 2)]).wait()

    @pl.when(i == 0)
    def _():
        fetch(0).start()

    @pl.when(i + 1 < n_tiles)
    def _():
        fetch(i + 1).start()

    @pl.when(i == 0)
    def _():
        zero_ref[...] = jnp.zeros_like(zero_ref)

        def clear(blk):
            dst = xs_hbm.at[pl.ds(pl.multiple_of(blk * blk_rows, blk_rows), blk_rows)]
            return pltpu.make_async_copy(zero_ref, dst, zsem)

        def each(fn):
            def fill(e, carry):
                @pl.when(pend_ref[e] > pstart_ref[e])
                def _():
                    fn(clear(pend_ref[e] // MOE_BLK - 1))
                return carry
            lax.fori_loop(0, N_EXPERTS, fill, 0)

            def tail(b, carry):
                fn(clear(b))
                return carry
            lax.fori_loop(pend_ref[N_EXPERTS - 1] // MOE_BLK, n_blocks, tail, 0)

        each(lambda cp: cp.start())
        each(lambda cp: cp.wait())

    fetch(i).wait()
    src_tile = tile_ref.at[lax.rem(i, DISPATCH_BUFS)]
    row_sem = row_sems.at[lax.rem(i, 2)]

    def issue(t, carry):
        for k in range(TOP_K):
            dst = dst_ref[0, 0, t * TOP_K + k]
            pltpu.make_async_copy(src_tile.at[_token_rows(t)], xs_hbm.at[_token_rows(dst)],
                                  row_sem).start(priority=k % 2)
        return carry
    lax.fori_loop(0, TOK_TILE, issue, 0)

    @pl.when(i > 0)
    def _():
        drain(i - 1)

    @pl.when(i == n_tiles - 1)
    def _():
        drain(i)


def _dispatch(dst, pstart, pend, h2t, n_rows):
    n_tiles = h2t.shape[0] // (TOK_TILE * ROW_TILES)
    smem_tile = pl.BlockSpec((1, 1, TOK_TILE * TOP_K), lambda i: (i, 0, 0), memory_space=pltpu.SMEM)
    smem_all = pl.BlockSpec(memory_space=pltpu.SMEM)
    return pl.pallas_call(
        _dispatch_body,
        out_shape=jax.ShapeDtypeStruct((n_rows * ROW_TILES, LANES), F32),
        grid=(n_tiles,),
        in_specs=[smem_tile, smem_all, smem_all, pl.BlockSpec(memory_space=pl.ANY)],
        out_specs=pl.BlockSpec(memory_space=pl.ANY),
        scratch_shapes=[pltpu.VMEM((DISPATCH_BUFS, TOK_TILE * ROW_TILES, LANES), F32),
                        pltpu.VMEM((MOE_BLK * ROW_TILES, LANES), F32),
                        pltpu.SemaphoreType.DMA((DISPATCH_BUFS,)), pltpu.SemaphoreType.DMA((2,)),
                        pltpu.SemaphoreType.DMA],
        compiler_params=pltpu.CompilerParams(dimension_semantics=("arbitrary",)),
        name="moe_dispatch",
    )(dst.reshape(n_tiles, 1, TOK_TILE * TOP_K), pstart, pend, h2t)


def _experts_body(bexp_ref, nused_ref, xs_hbm, wg_ref, wu_ref, wd_ref, ys_hbm,
                  xbuf, obuf, wgu_scr, wd_scr, in_sems, out_sems):
    b = pl.program_id(0)
    n_blocks = pl.num_programs(0)
    n_used = nused_ref[0]
    used = b < n_used
    blk_rows = MOE_BLK * ROW_TILES

    def block(ref, blk):
        return ref.at[pl.ds(pl.multiple_of(blk * blk_rows, blk_rows), blk_rows)]

    def fetch(blk):
        slot = lax.rem(blk, IN_BUFS)
        return pltpu.make_async_copy(block(xs_hbm, blk), xbuf.at[slot], in_sems.at[slot])

    def put(blk):
        slot = lax.rem(blk, OUT_BUFS)
        return pltpu.make_async_copy(obuf.at[slot], block(ys_hbm, blk), out_sems.at[slot])

    @pl.when(b == 0)
    def _():
        for first in range(IN_BUFS - 1):
            @pl.when(first < n_used)
            def _():
                fetch(first).start()

    @pl.when(b >= OUT_BUFS)
    def _():
        put(b - OUT_BUFS).wait()

    new_expert = jnp.logical_or(b == 0, bexp_ref[b] != bexp_ref[jnp.maximum(b - 1, 0)])

    @pl.when(jnp.logical_and(used, new_expert))
    def _():
        wgu_scr[:, :D_EXPERT] = wg_ref[0].astype(BF16)
        wgu_scr[:, D_EXPERT:] = wu_ref[0].astype(BF16)
        wd_scr[...] = wd_ref[0].astype(BF16)

    out = obuf.at[lax.rem(b, OUT_BUFS)]

    @pl.when(used)
    def _():
        ahead = b + IN_BUFS - 1

        @pl.when(ahead < n_used)
        def _():
            fetch(ahead).start()
        fetch(b).wait()
        rows_in = xbuf.at[lax.rem(b, IN_BUFS)]
        xb = jnp.concatenate([rows_in[_tile_rows(j, MOE_BLK), :] for j in range(ROW_TILES)],
                             axis=1).astype(BF16)
        gu = _dot(xb, wgu_scr[...])
        act = (jax.nn.silu(gu[:, :D_EXPERT]) * gu[:, D_EXPERT:]).astype(BF16)
        y = _dot(act, wd_scr[...])
        for j in range(ROW_TILES):
            out[_tile_rows(j, MOE_BLK), :] = y[:, j * LANES:(j + 1) * LANES]

    @pl.when(jnp.logical_not(used))
    def _():
        out[...] = jnp.zeros((blk_rows, LANES), F32)

    put(b).start()

    @pl.when(b == n_blocks - 1)
    def _():
        for back in range(OUT_BUFS):
            put(b - back).wait()


def _experts(blk_exp, n_used, xs, w_gate, w_up, w_down):
    blk_rows = MOE_BLK * ROW_TILES
    n_blocks = xs.shape[0] // blk_rows
    assert n_blocks >= OUT_BUFS

    def w_map(b, bexp, nused):
        return (bexp[jnp.minimum(b, nused[0] - 1)], 0, 0)

    return pl.pallas_call(
        _experts_body,
        out_shape=jax.ShapeDtypeStruct(xs.shape, F32),
        grid_spec=pltpu.PrefetchScalarGridSpec(
            num_scalar_prefetch=2,
            grid=(n_blocks,),
            in_specs=[pl.BlockSpec(memory_space=pl.ANY),
                      pl.BlockSpec((1, D_MODEL, D_EXPERT), w_map),
                      pl.BlockSpec((1, D_MODEL, D_EXPERT), w_map),
                      pl.BlockSpec((1, D_EXPERT, D_MODEL), w_map)],
            out_specs=pl.BlockSpec(memory_space=pl.ANY),
            scratch_shapes=[pltpu.VMEM((IN_BUFS, blk_rows, LANES), F32),
                            pltpu.VMEM((OUT_BUFS, blk_rows, LANES), F32),
                            pltpu.VMEM((D_MODEL, 2 * D_EXPERT), BF16),
                            pltpu.VMEM((D_EXPERT, D_MODEL), BF16),
                            pltpu.SemaphoreType.DMA((IN_BUFS,)),
                            pltpu.SemaphoreType.DMA((OUT_BUFS,))]),
        compiler_params=pltpu.CompilerParams(dimension_semantics=("arbitrary",),
                                             vmem_limit_bytes=VMEM_LIMIT),
        name="moe_experts",
    )(blk_exp, n_used, xs, w_gate, w_up, w_down)


def _combine_body(n_prompt_tiles, n_tiles, dst_ref, dnext_ref, xs1_ref, g2_ref, wts_ref, fg_ref,
                  ys_hbm, yp_ref, ysm_ref, buf_ref, wb_ref, sems):
    i = pl.program_id(0)

    def issue(d_ref, slot):
        def body(t, carry):
            for k in range(TOP_K):
                src = d_ref[0, 0, t * TOP_K + k]
                pltpu.make_async_copy(ys_hbm.at[_token_rows(src)], buf_ref.at[slot, k, _token_rows(t)],
                                      sems.at[slot]).start(priority=k % 2)
            return carry
        lax.fori_loop(0, TOK_TILE, body, 0)

    def drain(slot):
        for k in range(TOP_K):
            pltpu.make_async_copy(ys_hbm.at[pl.ds(0, TOK_TILE * ROW_TILES)], buf_ref.at[slot, k],
                                  sems.at[slot]).wait()

    @pl.when(i == 0)
    def _():
        issue(dst_ref, 0)

    def step(slot):
        @pl.when(i + 1 < n_tiles)
        def _():
            issue(dnext_ref, 1 - slot)
        wts = wts_ref[...]
        for k in range(TOP_K):
            wb_ref[k] = jnp.broadcast_to(wts[:, k:k + 1], (TOK_TILE, LANES))
        drain(slot)
        parts = []
        for j in range(ROW_TILES):
            acc = wb_ref[0] * buf_ref[slot, 0, _tile_rows(j, TOK_TILE), :]
            for k in range(1, TOP_K):
                acc = acc + wb_ref[k] * buf_ref[slot, k, _tile_rows(j, TOK_TILE), :]
            parts.append(acc)
        seg = TOK_TILE // g2_ref.shape[0]
        g2 = jnp.concatenate(
            [jnp.broadcast_to(g2_ref[s], (seg, D_MODEL)) for s in range(g2_ref.shape[0])], axis=0)
        x = xs1_ref[...] + g2 * jnp.concatenate(parts, axis=1)
        y = (x * _rms(x)) * fg_ref[...]

        @pl.when(i < n_prompt_tiles)
        def _():
            yp_ref[...] = y

        @pl.when(i >= n_prompt_tiles)
        def _():
            ysm_ref[...] = y

    for slot in range(2):
        pl.when(lax.rem(i, 2) == slot)(functools.partial(step, slot))


def _combine(dst, xs1, g2_blocks, wts, final_g, ys, n_p):
    n_tok = xs1.shape[0]
    n_tiles = n_tok // TOK_TILE
    n_pt = n_p // TOK_TILE
    segs = TOK_TILE // CHUNK
    smem_tile = pl.BlockSpec((1, 1, TOK_TILE * TOP_K), lambda i: (i, 0, 0), memory_space=pltpu.SMEM)
    smem_next = pl.BlockSpec((1, 1, TOK_TILE * TOP_K), lambda i: (jnp.minimum(i + 1, n_tiles - 1), 0, 0),
                             memory_space=pltpu.SMEM)
    dst = dst.reshape(n_tiles, 1, TOK_TILE * TOP_K)
    return pl.pallas_call(
        functools.partial(_combine_body, n_pt, n_tiles),
        out_shape=[jax.ShapeDtypeStruct((n_p, D_MODEL), F32),
                   jax.ShapeDtypeStruct((n_tok - n_p, D_MODEL), F32)],
        grid=(n_tiles,),
        in_specs=[smem_tile, smem_next,
                  pl.BlockSpec((TOK_TILE, D_MODEL), lambda i: (i, 0)),
                  pl.BlockSpec((segs, 1, D_MODEL), lambda i: (i, 0, 0)),
                  pl.BlockSpec((TOK_TILE, TOP_K), lambda i: (i, 0)),
                  pl.BlockSpec((1, D_MODEL), lambda i: (0, 0)),
                  pl.BlockSpec(memory_space=pl.ANY)],
        out_specs=[pl.BlockSpec((TOK_TILE, D_MODEL), lambda i: (jnp.minimum(i, n_pt - 1), 0)),
                   pl.BlockSpec((TOK_TILE, D_MODEL), lambda i: (jnp.maximum(i - n_pt, 0), 0))],
        scratch_shapes=[pltpu.VMEM((2, TOP_K, TOK_TILE * ROW_TILES, LANES), F32),
                        pltpu.VMEM((TOP_K, TOK_TILE, LANES), F32),
                        pltpu.SemaphoreType.DMA((2,))],
        compiler_params=pltpu.CompilerParams(dimension_semantics=("arbitrary",),
                                             vmem_limit_bytes=VMEM_LIMIT),
        name="moe_combine",
    )(dst, dst, xs1, g2_blocks, wts, final_g.reshape(1, D_MODEL), ys)


def kernel(x_prompt, x_sample, c_prompt, c_sample, state_ssm_re, state_ssm_im, norm1_g, norm2_g,
           w_ada, b_ada, w_in, w_s, b_s, g_v, lam_re, lam_im, log_dt, b_re, b_im, c_re, c_im,
           d_skip, w_glu, b_glu, out_g_a, out_g_b, w_out, w_router, router_bias, w_gate, w_up,
           w_down, ws_gate, ws_up, ws_down, final_g):
    assert norm1_g.shape[0] == 1
    bp, sp, _ = x_prompt.shape
    bs, ss, _ = x_sample.shape
    n_p, n_s = bp * sp, bs * ss
    n_tok = n_p + n_s
    l = 0

    mod = _adaln(jnp.concatenate([c_prompt, c_sample], axis=0), w_ada[l], b_ada[l])
    mod = mod.reshape(bp + bs, 6, D_MODEL)

    wb, wc, tabs = _s5_tables(lam_re[l], lam_im[l], log_dt[l], b_re[l], b_im[l], c_re[l], c_im[l])
    pos = jnp.arange(SGU_LEN)
    mask = (pos[:, None] // CHUNK) >= (pos[None, :] // CHUNK)
    ws_masked = jnp.where(mask[None], w_s[l], 0.0)
    row = lax.broadcasted_iota(I32, (MIX_ROWS, MIX_ROWS), 0)
    col = lax.broadcasted_iota(I32, (MIX_ROWS, MIX_ROWS), 1)
    ltri = (col < row).astype(BF16)

    def sgu_weights(ln):
        bsf = jnp.repeat(b_s[l][:, :ln].T, HEAD_A, axis=1)
        return ws_masked[:, :ln, :ln].astype(BF16), bsf

    ws_p, bsf_p = sgu_weights(SGU_LEN)
    ws_s, bsf_s = sgu_weights(ss)
    shared = [norm1_g[l].reshape(1, D_MODEL), w_in[l].astype(BF16), g_v[l].reshape(1, D_A),
              wb, wc, tabs, d_skip[l].reshape(1, D_B), w_glu[l].astype(BF16),
              b_glu[l].reshape(1, D_B), out_g_a[l].reshape(1, D_A), out_g_b[l].reshape(1, D_B),
              w_out[l].astype(BF16), norm2_g[l].reshape(1, D_MODEL), w_router[l].astype(BF16),
              router_bias[l].reshape(1, N_EXPERTS), ws_gate[l].astype(BF16),
              ws_up[l].astype(BF16), ws_down[l].astype(BF16), ltri]
    h0_s = jnp.broadcast_to(_lay(state_ssm_re[l], state_ssm_im[l])[:, None, :], (bs, SUB, STATE_COLS))
    xs1, h2t, slab, wslab, cnt_all, hfin_p, hfin_s, v_rows = _mix(
        x_prompt, x_sample, mod, h0_s, ws_p, bsf_p, ws_s, bsf_s, shared)

    wts = wslab[:, :TOP_K]
    counts = cnt_all[0].astype(I32)
    padded = (counts + MOE_BLK - 1) // MOE_BLK * MOE_BLK
    pend = jnp.cumsum(padded).astype(I32)
    pstart = pend - padded
    n_blocks = -(-n_tok * TOP_K // MOE_BLK) + N_EXPERTS
    n_used = (pend[-1:] // MOE_BLK).astype(I32)
    blk_start = jnp.arange(n_blocks, dtype=I32) * MOE_BLK
    blk_exp = jnp.sum((pend[None, :] <= blk_start[:, None]).astype(I32), axis=1)
    blk_exp = jnp.minimum(blk_exp, N_EXPERTS - 1)

    dst = _slots(slab, pstart)[:, :TOP_K]
    xs = _dispatch(dst, pstart, pend, h2t, n_blocks * MOE_BLK)
    ys = _experts(blk_exp, n_used, xs, w_gate[l], w_up[l], w_down[l])

    g2 = mod[:, 5, :]
    g2_blocks = jnp.concatenate([jnp.repeat(g2[:bp], sp // CHUNK, axis=0),
                                 jnp.repeat(g2[bp:], ss // CHUNK, axis=0)], axis=0)
    y_p, y_s = _combine(dst, xs1, g2_blocks.reshape(n_tok // CHUNK, 1, D_MODEL), wts, final_g, ys, n_p)

    re_p, im_p = _unlay(hfin_p[:, 0, :])
    re_s, im_s = _unlay(hfin_s[:, 0, :])
    return (y_p.reshape(bp, sp, D_MODEL), y_s.reshape(bs, ss, D_MODEL),
            re_p[None], im_p[None], re_s[None], im_s[None], v_rows[None])
```

```python
import functools

import jax
import jax.numpy as jnp
from jax import lax
from jax.experimental import pallas as pl
from jax.experimental.pallas import tpu as pltpu

F32 = jnp.float32
BF16 = jnp.bfloat16
I32 = jnp.int32

D_MODEL = 1024
D_A = 512
D_B = 512
N_HEADS_A = 4
HEAD_A = 128
SSM_GROUP = 16
N_GROUPS_B = 32
SSM_STATE = 64
N_EXPERTS = 256
TOP_K = 8
D_EXPERT = 256
ROUTE_SCALE = 2.5
CHUNK = 64
SGU_LEN = 128
EPS = 1e-6

STATE_CHUNKS = 4
GROUPS_PER_CHUNK = N_GROUPS_B // STATE_CHUNKS
HALF = GROUPS_PER_CHUNK * SSM_STATE
STATE_COLS = STATE_CHUNKS * 2 * HALF
SCAN_W = 256
SUB = 8
LANES = 128
ROW_TILES = D_MODEL // LANES

TAB_PW = 0
TAB_MD = 8
TAB_P8 = 11

MIX_ROWS = 256
MOE_BLK = 256
DISPATCH_BUFS = 3
IN_BUFS = 3
OUT_BUFS = 2
TOK_TILE = 256
SLAB = 128
VMEM_LIMIT = 56 * 1024 * 1024


def _dot(a, b):
    return jnp.dot(a, b, preferred_element_type=F32)


def _rms(x):
    return lax.rsqrt(jnp.mean(x * x, axis=-1, keepdims=True) + EPS)


def _tile_rows(j, n):
    return pl.ds(j, n, stride=ROW_TILES)


def _adaln_body(c_ref, w_ref, b_ref, o_ref):
    c = c_ref[...]
    o_ref[...] = _dot(jax.nn.silu(c).astype(BF16), w_ref[...].astype(BF16)) + b_ref[...]


def _adaln(c, w_ada, b_ada):
    n = c.shape[0]
    cols = w_ada.shape[1]
    blk = 1536
    return pl.pallas_call(
        _adaln_body,
        out_shape=jax.ShapeDtypeStruct((n, cols), F32),
        grid=(cols // blk,),
        in_specs=[pl.BlockSpec((n, D_MODEL), lambda j: (0, 0)),
                  pl.BlockSpec((D_MODEL, blk), lambda j: (0, j)),
                  pl.BlockSpec((1, blk), lambda j: (0, j))],
        out_specs=pl.BlockSpec((n, blk), lambda j: (0, j)),
        compiler_params=pltpu.CompilerParams(dimension_semantics=("arbitrary",)),
        name="adaln",
    )(c, w_ada, b_ada.reshape(1, cols))


def _lay(re, im):
    lead = re.shape[:-2]
    re = re.reshape(lead + (STATE_CHUNKS, HALF))
    im = im.reshape(lead + (STATE_CHUNKS, HALF))
    return jnp.concatenate([re, im], axis=-1).reshape(lead + (STATE_COLS,))


def _unlay(v):
    lead = v.shape[:-1]
    v = v.reshape(lead + (STATE_CHUNKS, 2, HALF))
    re = v[..., 0, :].reshape(lead + (N_GROUPS_B, SSM_STATE))
    im = v[..., 1, :].reshape(lead + (N_GROUPS_B, SSM_STATE))
    return re, im


def _s5_tables(lam_re, lam_im, log_dt, b_re, b_im, c_re, c_im):
    dt = jnp.exp(log_dt.astype(F32))[:, None]
    lr, li = lam_re.astype(F32), lam_im.astype(F32)

    def apow(k):
        mag = jnp.exp(lr * dt * k)
        return mag * jnp.cos(li * dt * k), mag * jnp.sin(li * dt * k)

    ar, ai = apow(1.0)
    den = lr * lr + li * li
    nr, ni = ar - 1.0, ai
    kr, ki = (nr * lr + ni * li) / den, (ni * lr - nr * li) / den
    br, bi = b_re.astype(F32), b_im.astype(F32)
    bbr = kr[..., None] * br - ki[..., None] * bi
    bbi = kr[..., None] * bi + ki[..., None] * br
    eye = jnp.eye(GROUPS_PER_CHUNK, dtype=F32)

    def bproj(bb):
        bb = bb.reshape(STATE_CHUNKS, GROUPS_PER_CHUNK, SSM_STATE, SSM_GROUP)
        w = jnp.einsum("mgph,gk->mghkp", bb, eye)
        return w.reshape(STATE_CHUNKS, GROUPS_PER_CHUNK * SSM_GROUP, HALF)

    wb = jnp.concatenate([bproj(bbr), bproj(bbi)], axis=-1).astype(BF16)

    def cproj(cc):
        cc = cc.reshape(STATE_CHUNKS, GROUPS_PER_CHUNK, SSM_GROUP, SSM_STATE)
        w = jnp.einsum("mghp,gk->mgpkh", cc, eye)
        return w.reshape(STATE_CHUNKS, HALF, GROUPS_PER_CHUNK * SSM_GROUP)

    wc = jnp.concatenate([cproj(c_re.astype(F32)), cproj(-c_im.astype(F32))], axis=1).astype(BF16)

    rows = jnp.arange(SUB, dtype=F32)
    tabs = []
    for i in range(SUB):
        pr, pi = apow(float(i + 1))
        tabs.append(jnp.broadcast_to(_lay(pr, pi)[None], (SUB, STATE_COLS)))
    for d in (1, 2, 4):
        pr, pi = apow(float(SUB * d))
        keep = (rows >= d).astype(F32)[:, None]
        tabs.append(_lay(pr, pi)[None] * keep)
    pr, pi = apow(SUB * rows[:, None, None])
    tabs.append(_lay(pr, pi))
    return wb, wc, jnp.stack(tabs)


def _cmul(ar, ai, br, bi):
    return ar * br - ai * bi, ar * bi + ai * br


def _s5_scan_block(bu_ref, row0, h_ref, tab_ref):
    row_id = lax.broadcasted_iota(I32, (SUB, SCAN_W), 0)
    tiles = SCAN_W // LANES
    for m in range(STATE_CHUNKS):
        for hf in range(HALF // SCAN_W):
            c_re0 = m * 2 * HALF + hf * SCAN_W
            c_im0 = c_re0 + HALF
            cre = pl.ds(c_re0, SCAN_W)
            cim = pl.ds(c_im0, SCAN_W)

            def tab(slot):
                return tab_ref[slot, :, cre], tab_ref[slot, :, cim]

            def load(i, c0):
                rows = pl.ds(row0 + i, SUB, stride=SUB)
                return jnp.concatenate([bu_ref[c0 // LANES + j, rows, :] for j in range(tiles)], axis=1)

            def store(i, c0, val):
                rows = pl.ds(row0 + i, SUB, stride=SUB)
                for j in range(tiles):
                    bu_ref[c0 // LANES + j, rows, :] = val[:, j * LANES:(j + 1) * LANES]

            a_re, a_im = tab(TAB_PW)
            s_re = load(0, c_re0)
            s_im = load(0, c_im0)
            loc = [(s_re, s_im)]
            for i in range(1, SUB):
                p_re, p_im = _cmul(a_re, a_im, s_re, s_im)
                s_re = p_re + load(i, c_re0)
                s_im = p_im + load(i, c_im0)
                loc.append((s_re, s_im))
            e_re, e_im = s_re, s_im
            for n, d in enumerate((1, 2, 4)):
                m_re, m_im = tab(TAB_MD + n)
                q_re, q_im = _cmul(m_re, m_im, pltpu.roll(e_re, d, 0), pltpu.roll(e_im, d, 0))
                e_re, e_im = e_re + q_re, e_im + q_im
            p8_re, p8_im = tab(TAB_P8)
            c_re, c_im = _cmul(p8_re, p8_im, h_ref[:, cre], h_ref[:, cim])
            c_re = c_re + jnp.where(row_id >= 1, pltpu.roll(e_re, 1, 0), 0.0)
            c_im = c_im + jnp.where(row_id >= 1, pltpu.roll(e_im, 1, 0), 0.0)
            for i in range(SUB):
                w_re, w_im = tab(TAB_PW + i)
                q_re, q_im = _cmul(w_re, w_im, c_re, c_im)
                f_re, f_im = loc[i][0] + q_re, loc[i][1] + q_im
                store(i, c_re0, f_re)
                store(i, c_im0, f_im)
            h_ref[:, cre] = jnp.broadcast_to(f_re[SUB - 1:SUB, :], (SUB, SCAN_W))
            h_ref[:, cim] = jnp.broadcast_to(f_im[SUB - 1:SUB, :], (SUB, SCAN_W))


def _mix_tile(seqs, ln, chain, new_seq, x_ref, mod_ref, h0_ref, ws_ref, bsf_ref, shared, outs,
              hfin_ref, v_ref, scratch):
    (n1g_ref, win_ref, gv_ref, wb_ref, wc_ref, tab_ref, dsk_ref, wglu_ref, bglu_ref, oga_ref,
     ogb_ref, wout_ref, n2g_ref, wr_ref, rb_ref, wsg_ref, wsu_ref, wsd_ref, ltri_ref) = shared
    xs1_ref, h2t_ref, slab_ref, wslab_ref, cnt_ref = outs
    bu_ref, h_scr, cnt_scr = scratch
    t_rows = MIX_ROWS
    rows_per_seq = t_rows // seqs

    x = x_ref[...].reshape(t_rows, D_MODEL)

    def modrow(j):
        parts = [jnp.broadcast_to(mod_ref[s, j:j + 1, :], (rows_per_seq, D_MODEL)) for s in range(seqs)]
        return parts[0] if seqs == 1 else jnp.concatenate(parts, axis=0)

    h = (x * _rms(x)) * n1g_ref[...] * (1.0 + modrow(1)) + modrow(0)
    z = _dot(h.astype(BF16), win_ref[...])
    u_a = z[:, :D_A]
    v_a = z[:, D_A:2 * D_A]
    u_b = z[:, 2 * D_A:]

    vh_parts = []
    for hh in range(N_HEADS_A):
        cols = slice(hh * HEAD_A, (hh + 1) * HEAD_A)
        vv = v_a[:, cols]
        vh_parts.append((vv * _rms(vv)) * gv_ref[:, cols])
    if v_ref is not None:
        v_ref[...] = jnp.concatenate(vh_parts, axis=1).reshape(seqs, rows_per_seq, D_A)
    n_sgu = t_rows // ln
    s_parts = []
    for hh in range(N_HEADS_A):
        vb = vh_parts[hh].astype(BF16)
        w_h = ws_ref[hh]
        s_parts.append(jnp.concatenate(
            [_dot(w_h, vb[c * ln:(c + 1) * ln, :]) for c in range(n_sgu)], axis=0))
    bsf = bsf_ref[...]
    s_mix = jnp.concatenate(s_parts, axis=1) + jnp.concatenate([bsf] * n_sgu, axis=0)
    y_a = u_a * s_mix

    ub16 = u_b.astype(BF16)
    gc = GROUPS_PER_CHUNK * SSM_GROUP
    tiles_per_chunk = 2 * HALF // LANES
    for m in range(STATE_CHUNKS):
        bu = _dot(ub16[:, m * gc:(m + 1) * gc], wb_ref[m])
        for j in range(tiles_per_chunk):
            bu_ref[m * tiles_per_chunk + j] = bu[:, j * LANES:(j + 1) * LANES]
    n_blk = t_rows // CHUNK
    if chain:
        @pl.when(new_seq)
        def _():
            h_scr[...] = jnp.zeros_like(h_scr)
    for blk in range(n_blk):
        if not chain:
            h_scr[...] = h0_ref[blk]
        _s5_scan_block(bu_ref, blk * CHUNK, h_scr, tab_ref)
        if not chain:
            hfin_ref[blk] = h_scr[...]
    if chain:
        hfin_ref[0] = h_scr[...]
    y_parts = []
    for m in range(STATE_CHUNKS):
        st = jnp.concatenate([bu_ref[m * tiles_per_chunk + j] for j in range(tiles_per_chunk)], axis=1)
        y_parts.append(_dot(st.astype(BF16), wc_ref[m]))
    y_s = jnp.concatenate(y_parts, axis=1) + dsk_ref[...] * u_b
    g_b = jax.nn.gelu(y_s)
    y_b = g_b * jax.nn.sigmoid(_dot(g_b.astype(BF16), wglu_ref[...]) + bglu_ref[...])

    na = (y_a * _rms(y_a)) * oga_ref[...]
    nb = (y_b * _rms(y_b)) * ogb_ref[...]
    mix = _dot(jnp.concatenate([na, nb], axis=1).astype(BF16), wout_ref[...])
    x1 = x + modrow(2) * mix

    h2 = (x1 * _rms(x1)) * n2g_ref[...] * (1.0 + modrow(4)) + modrow(3)
    for j in range(ROW_TILES):
        h2t_ref[_tile_rows(j, t_rows), :] = h2[:, j * LANES:(j + 1) * LANES]
    h2b = h2.astype(BF16)
    act = jax.nn.silu(_dot(h2b, wsg_ref[...])) * _dot(h2b, wsu_ref[...])
    shared_out = _dot(act.astype(BF16), wsd_ref[...])
    xs1_ref[...] = x1 + modrow(5) * shared_out

    scores = jax.nn.sigmoid(_dot(h2b, wr_ref[...]))
    lane = lax.broadcasted_iota(I32, (t_rows, N_EXPERTS), 1).astype(F32)
    work = scores + rb_ref[...]
    onehot = jnp.zeros((t_rows, N_EXPERTS), F32)
    idxs, sels = [], []
    for _ in range(TOP_K):
        top = jnp.max(work, axis=-1, keepdims=True)
        idx = jnp.min(jnp.where(work == top, lane, float(N_EXPERTS)), axis=-1, keepdims=True)
        pick = lane == idx
        sels.append(jnp.sum(jnp.where(pick, scores, 0.0), axis=-1, keepdims=True))
        idxs.append(idx)
        work = jnp.where(pick, -jnp.inf, work)
        onehot = jnp.where(pick, 1.0, onehot)
    total = sels[0]
    for k in range(1, TOP_K):
        total = total + sels[k]
    ranktab = _dot(ltri_ref[...], onehot.astype(BF16)) + cnt_scr[0:1, :]
    slab_lane = lax.broadcasted_iota(I32, (t_rows, SLAB), 1)
    slab = jnp.zeros((t_rows, SLAB), I32)
    wslab = jnp.zeros((t_rows, SLAB), F32)
    for k in range(TOP_K):
        rank = jnp.sum(jnp.where(lane == idxs[k], ranktab, 0.0), axis=-1, keepdims=True)
        slab = jnp.where(slab_lane == k, idxs[k].astype(I32), slab)
        slab = jnp.where(slab_lane == TOP_K + k, rank.astype(I32), slab)
        wslab = jnp.where(slab_lane == k, ROUTE_SCALE * sels[k] / total, wslab)
    slab_ref[...] = slab
    wslab_ref[...] = wslab
    cnt_scr[...] = cnt_scr[...] + jnp.sum(onehot, axis=0, keepdims=True)
    cnt_ref[...] = cnt_scr[...]


def _mix_body(n_prompt_tiles, tiles_per_seq, seqs_s, ln_s,
              xp_ref, xsm_ref, modp_ref, mods_ref, h0s_ref, wsp_ref, bsfp_ref, wss_ref, bsfs_ref,
              *rest):
    shared = rest[:19]
    xs1_ref, h2t_ref, slab_ref, wslab_ref, cnt_ref, hfp_ref, hfs_ref, v_ref = rest[19:27]
    scratch = rest[27:]
    outs = (xs1_ref, h2t_ref, slab_ref, wslab_ref, cnt_ref)
    s = pl.program_id(0)

    @pl.when(s == 0)
    def _():
        scratch[2][...] = jnp.zeros_like(scratch[2])

    @pl.when(s < n_prompt_tiles)
    def _():
        _mix_tile(1, SGU_LEN, True, lax.rem(s, tiles_per_seq) == 0, xp_ref, modp_ref, None,
                  wsp_ref, bsfp_ref, shared, outs, hfp_ref, None, scratch)

    @pl.when(s >= n_prompt_tiles)
    def _():
        _mix_tile(seqs_s, ln_s, False, None, xsm_ref, mods_ref, h0s_ref,
                  wss_ref, bsfs_ref, shared, outs, hfs_ref, v_ref, scratch)


def _mix(x_prompt, x_sample, mod, h0_s, ws_p, bsf_p, ws_s, bsf_s, shared):
    bp, sp, _ = x_prompt.shape
    bs, ss, _ = x_sample.shape
    assert sp % MIX_ROWS == 0 and MIX_ROWS % ss == 0 and ss == CHUNK
    seqs_s = MIX_ROWS // ss
    assert bs % seqs_s == 0 and bp % seqs_s == 0
    tiles_per_seq = sp // MIX_ROWS
    n_pt = bp * tiles_per_seq
    n_st = bs // seqs_s
    n_tok = bp * sp + bs * ss

    def p_tile(s):
        return jnp.minimum(s, n_pt - 1)

    def s_tile(s):
        return jnp.maximum(s - n_pt, 0)

    def const(shape):
        nd = len(shape)
        return pl.BlockSpec(shape, lambda s: (0,) * nd)

    in_specs = [
        pl.BlockSpec((1, MIX_ROWS, D_MODEL), lambda s: (p_tile(s) // tiles_per_seq, p_tile(s) % tiles_per_seq, 0)),
        pl.BlockSpec((seqs_s, ss, D_MODEL), lambda s: (s_tile(s), 0, 0)),
        pl.BlockSpec((1, 6, D_MODEL), lambda s: (p_tile(s) // tiles_per_seq, 0, 0)),
        pl.BlockSpec((seqs_s, 6, D_MODEL), lambda s: (bp // seqs_s + s_tile(s), 0, 0)),
        pl.BlockSpec((seqs_s, SUB, STATE_COLS), lambda s: (s_tile(s), 0, 0)),
        const(ws_p.shape), const(bsf_p.shape), const(ws_s.shape), const(bsf_s.shape),
    ] + [const(w.shape) for w in shared]
    out_shape = [
        jax.ShapeDtypeStruct((n_tok, D_MODEL), F32),
        jax.ShapeDtypeStruct((n_tok * ROW_TILES, LANES), F32),
        jax.ShapeDtypeStruct((n_tok, SLAB), I32),
        jax.ShapeDtypeStruct((n_tok, SLAB), F32),
        jax.ShapeDtypeStruct((SUB, N_EXPERTS), F32),
        jax.ShapeDtypeStruct((bp, SUB, STATE_COLS), F32),
        jax.ShapeDtypeStruct((bs, SUB, STATE_COLS), F32),
        jax.ShapeDtypeStruct((bs, ss, D_A), F32),
    ]
    out_specs = [
        pl.BlockSpec((MIX_ROWS, D_MODEL), lambda s: (s, 0)),
        pl.BlockSpec((MIX_ROWS * ROW_TILES, LANES), lambda s: (s, 0)),
        pl.BlockSpec((MIX_ROWS, SLAB), lambda s: (s, 0)),
        pl.BlockSpec((MIX_ROWS, SLAB), lambda s: (s, 0)),
        const((SUB, N_EXPERTS)),
        pl.BlockSpec((1, SUB, STATE_COLS), lambda s: (p_tile(s) // tiles_per_seq, 0, 0)),
        pl.BlockSpec((seqs_s, SUB, STATE_COLS), lambda s: (s_tile(s), 0, 0)),
        pl.BlockSpec((seqs_s, ss, D_A), lambda s: (s_tile(s), 0, 0)),
    ]
    return pl.pallas_call(
        functools.partial(_mix_body, n_pt, tiles_per_seq, seqs_s, ss),
        out_shape=out_shape,
        grid=(n_pt + n_st,),
        in_specs=in_specs,
        out_specs=out_specs,
        scratch_shapes=[pltpu.VMEM((STATE_COLS // LANES, MIX_ROWS, LANES), F32),
                        pltpu.VMEM((SUB, STATE_COLS), F32),
                        pltpu.VMEM((SUB, N_EXPERTS), F32)],
        compiler_params=pltpu.CompilerParams(dimension_semantics=("arbitrary",),
                                             vmem_limit_bytes=VMEM_LIMIT),
        name="mix",
    )(x_prompt, x_sample, mod, mod, h0_s, ws_p, bsf_p, ws_s, bsf_s, *shared)


SLOT_TILE = 1024


def _slots_body(slab_ref, pstart_ref, dst_ref):
    slab = slab_ref[...]
    pstart = pstart_ref[...]
    lane = lax.broadcasted_iota(I32, (SLOT_TILE, N_EXPERTS), 1)
    out_lane = lax.broadcasted_iota(I32, (SLOT_TILE, SLAB), 1)
    out = jnp.zeros((SLOT_TILE, SLAB), I32)
    for k in range(TOP_K):
        start = jnp.sum(jnp.where(lane == slab[:, k:k + 1], pstart, 0.0), axis=-1, keepdims=True)
        out = jnp.where(out_lane == k, start.astype(I32) + slab[:, TOP_K + k:TOP_K + k + 1], out)
    dst_ref[...] = out


def _slots(slab, pstart):
    n_tok = slab.shape[0]
    assert n_tok % SLOT_TILE == 0
    return pl.pallas_call(
        _slots_body,
        out_shape=jax.ShapeDtypeStruct((n_tok, SLAB), I32),
        grid=(n_tok // SLOT_TILE,),
        in_specs=[pl.BlockSpec((SLOT_TILE, SLAB), lambda i: (i, 0)),
                  pl.BlockSpec((1, N_EXPERTS), lambda i: (0, 0))],
        out_specs=pl.BlockSpec((SLOT_TILE, SLAB), lambda i: (i, 0)),
        compiler_params=pltpu.CompilerParams(dimension_semantics=("arbitrary",)),
        name="moe_slots",
    )(slab, pstart.astype(F32).reshape(1, N_EXPERTS))


def _token_rows(t):
    return pl.ds(pl.multiple_of(t * ROW_TILES, ROW_TILES), ROW_TILES)


def _dispatch_body(dst_ref, pstart_ref, pend_ref, h2t_hbm, xs_hbm, tile_ref, zero_ref, in_sems, row_sems,
                   zsem):
    i = pl.program_id(0)
    n_tiles = pl.num_programs(0)
    blk_rows = MOE_BLK * ROW_TILES
    tile_rows = TOK_TILE * ROW_TILES
    n_blocks = xs_hbm.shape[0] // blk_rows

    def fetch(tile):
        slot = lax.rem(tile, DISPATCH_BUFS)
        src = h2t_hbm.at[pl.ds(pl.multiple_of(tile * tile_rows, tile_rows), tile_rows)]
        return pltpu.make_async_copy(src, tile_ref.at[slot], in_sems.at[slot])

    def drain(tile):
        for _ in range(TOP_K):
            pltpu.make_async_copy(tile_ref.at[0], xs_hbm.at[pl.ds(0, tile_rows)],
                                  row_sems.at[lax.rem(tile, 2)]).wait()

    @pl.when(i == 0)
    def _():
        fetch(0).start()

    @pl.when(i + 1 < n_tiles)
    def _():
        fetch(i + 1).start()

    @pl.when(i == 0)
    def _():
        zero_ref[...] = jnp.zeros_like(zero_ref)

        def clear(blk):
            dst = xs_hbm.at[pl.ds(pl.multiple_of(blk * blk_rows, blk_rows), blk_rows)]
            return pltpu.make_async_copy(zero_ref, dst, zsem)

        def each(fn):
            def fill(e, carry):
                @pl.when(pend_ref[e] > pstart_ref[e])
                def _():
                    fn(clear(pend_ref[e] // MOE_BLK - 1))
                return carry
            lax.fori_loop(0, N_EXPERTS, fill, 0)

            def tail(b, carry):
                fn(clear(b))
                return carry
            lax.fori_loop(pend_ref[N_EXPERTS - 1] // MOE_BLK, n_blocks, tail, 0)

        each(lambda cp: cp.start())
        each(lambda cp: cp.wait())

    fetch(i).wait()
    src_tile = tile_ref.at[lax.rem(i, DISPATCH_BUFS)]
    row_sem = row_sems.at[lax.rem(i, 2)]

    def issue(t, carry):
        for k in range(TOP_K):
            dst = dst_ref[0, 0, t * TOP_K + k]
            pltpu.make_async_copy(src_tile.at[_token_rows(t)], xs_hbm.at[_token_rows(dst)],
                                  row_sem).start(priority=k % 2)
        return carry
    lax.fori_loop(0, TOK_TILE, issue, 0)

    @pl.when(i > 0)
    def _():
        drain(i - 1)

    @pl.when(i == n_tiles - 1)
    def _():
        drain(i)


def _dispatch(dst, pstart, pend, h2t, n_rows):
    n_tiles = h2t.shape[0] // (TOK_TILE * ROW_TILES)
    smem_tile = pl.BlockSpec((1, 1, TOK_TILE * TOP_K), lambda i: (i, 0, 0), memory_space=pltpu.SMEM)
    smem_all = pl.BlockSpec(memory_space=pltpu.SMEM)
    return pl.pallas_call(
        _dispatch_body,
        out_shape=jax.ShapeDtypeStruct((n_rows * ROW_TILES, LANES), F32),
        grid=(n_tiles,),
        in_specs=[smem_tile, smem_all, smem_all, pl.BlockSpec(memory_space=pl.ANY)],
        out_specs=pl.BlockSpec(memory_space=pl.ANY),
        scratch_shapes=[pltpu.VMEM((DISPATCH_BUFS, TOK_TILE * ROW_TILES, LANES), F32),
                        pltpu.VMEM((MOE_BLK * ROW_TILES, LANES), F32),
                        pltpu.SemaphoreType.DMA((DISPATCH_BUFS,)), pltpu.SemaphoreType.DMA((2,)),
                        pltpu.SemaphoreType.DMA],
        compiler_params=pltpu.CompilerParams(dimension_semantics=("arbitrary",)),
        name="moe_dispatch",
    )(dst.reshape(n_tiles, 1, TOK_TILE * TOP_K), pstart, pend, h2t)


def _experts_body(pstart_ref, pend_ref, xs_hbm, wg_ref, wu_ref, wd_ref, ys_hbm,
                  xbuf, obuf, wgu_scr, wd_scr, in_sems, out_sems):
    e = pl.program_id(0)
    blk_rows = MOE_BLK * ROW_TILES
    n_blocks = xs_hbm.shape[0] // blk_rows
    n_used = pend_ref[N_EXPERTS - 1] // MOE_BLK
    first_blk = pstart_ref[e] // MOE_BLK
    last_blk = pend_ref[e] // MOE_BLK

    def block(ref, blk):
        return ref.at[pl.ds(pl.multiple_of(blk * blk_rows, blk_rows), blk_rows)]

    def fetch(blk):
        slot = lax.rem(blk, IN_BUFS)
        return pltpu.make_async_copy(block(xs_hbm, blk), xbuf.at[slot], in_sems.at[slot])

    def put(blk):
        slot = lax.rem(blk, OUT_BUFS)
        return pltpu.make_async_copy(obuf.at[slot], block(ys_hbm, blk), out_sems.at[slot])

    def free_out(blk):
        @pl.when(blk >= OUT_BUFS)
        def _():
            put(blk - OUT_BUFS).wait()

    @pl.when(e == 0)
    def _():
        for first in range(IN_BUFS - 1):
            @pl.when(first < n_used)
            def _():
                fetch(first).start()

    @pl.when(last_blk > first_blk)
    def _():
        wgu_scr[:, :D_EXPERT] = wg_ref[0].astype(BF16)
        wgu_scr[:, D_EXPERT:] = wu_ref[0].astype(BF16)
        wd_scr[...] = wd_ref[0].astype(BF16)

    def one_block(b, carry):
        free_out(b)
        ahead = b + IN_BUFS - 1

        @pl.when(ahead < n_used)
        def _():
            fetch(ahead).start()
        fetch(b).wait()
        rows_in = xbuf.at[lax.rem(b, IN_BUFS)]
        out = obuf.at[lax.rem(b, OUT_BUFS)]
        xb = jnp.concatenate([rows_in[_tile_rows(j, MOE_BLK), :] for j in range(ROW_TILES)],
                             axis=1).astype(BF16)
        gu = _dot(xb, wgu_scr[...])
        act = (jax.nn.silu(gu[:, :D_EXPERT]) * gu[:, D_EXPERT:]).astype(BF16)
        y = _dot(act, wd_scr[...])
        for j in range(ROW_TILES):
            out[_tile_rows(j, MOE_BLK), :] = y[:, j * LANES:(j + 1) * LANES]
        put(b).start()
        return carry
    lax.fori_loop(first_blk, last_blk, one_block, 0)

    @pl.when(e == N_EXPERTS - 1)
    def _():
        def tail(b, carry):
            free_out(b)
            obuf[lax.rem(b, OUT_BUFS)] = jnp.zeros((blk_rows, LANES), F32)
            put(b).start()
            return carry
        lax.fori_loop(n_used, n_blocks, tail, 0)
        for back in range(OUT_BUFS):
            put(n_blocks - 1 - back).wait()


def _experts(pstart, pend, xs, w_gate, w_up, w_down):
    blk_rows = MOE_BLK * ROW_TILES
    n_blocks = xs.shape[0] // blk_rows
    assert n_blocks >= OUT_BUFS

    def w_map(e, pstart_ref, pend_ref):
        return (e, 0, 0)

    return pl.pallas_call(
        _experts_body,
        out_shape=jax.ShapeDtypeStruct(xs.shape, F32),
        grid_spec=pltpu.PrefetchScalarGridSpec(
            num_scalar_prefetch=2,
            grid=(N_EXPERTS,),
            in_specs=[pl.BlockSpec(memory_space=pl.ANY),
                      pl.BlockSpec((1, D_MODEL, D_EXPERT), w_map),
                      pl.BlockSpec((1, D_MODEL, D_EXPERT), w_map),
                      pl.BlockSpec((1, D_EXPERT, D_MODEL), w_map)],
            out_specs=pl.BlockSpec(memory_space=pl.ANY),
            scratch_shapes=[pltpu.VMEM((IN_BUFS, blk_rows, LANES), F32),
                            pltpu.VMEM((OUT_BUFS, blk_rows, LANES), F32),
                            pltpu.VMEM((D_MODEL, 2 * D_EXPERT), BF16),
                            pltpu.VMEM((D_EXPERT, D_MODEL), BF16),
                            pltpu.SemaphoreType.DMA((IN_BUFS,)),
                            pltpu.SemaphoreType.DMA((OUT_BUFS,))]),
        compiler_params=pltpu.CompilerParams(dimension_semantics=("arbitrary",),
                                             vmem_limit_bytes=VMEM_LIMIT),
        name="moe_experts",
    )(pstart, pend, xs, w_gate, w_up, w_down)


def _combine_body(n_prompt_tiles, n_tiles, dst_ref, dnext_ref, xs1_ref, g2_ref, wts_ref, fg_ref,
                  ys_hbm, yp_ref, ysm_ref, buf_ref, wb_ref, sems):
    i = pl.program_id(0)

    def issue(d_ref, slot):
        def body(t, carry):
            for k in range(TOP_K):
                src = d_ref[0, 0, t * TOP_K + k]
                pltpu.make_async_copy(ys_hbm.at[_token_rows(src)], buf_ref.at[slot, k, _token_rows(t)],
                                      sems.at[slot]).start(priority=k % 2)
            return carry
        lax.fori_loop(0, TOK_TILE, body, 0)

    def drain(slot):
        for k in range(TOP_K):
            pltpu.make_async_copy(ys_hbm.at[pl.ds(0, TOK_TILE * ROW_TILES)], buf_ref.at[slot, k],
                                  sems.at[slot]).wait()

    @pl.when(i == 0)
    def _():
        issue(dst_ref, 0)

    def step(slot):
        @pl.when(i + 1 < n_tiles)
        def _():
            issue(dnext_ref, 1 - slot)
        wts = wts_ref[...]
        for k in range(TOP_K):
            wb_ref[k] = jnp.broadcast_to(wts[:, k:k + 1], (TOK_TILE, LANES))
        drain(slot)
        parts = []
        for j in range(ROW_TILES):
            acc = wb_ref[0] * buf_ref[slot, 0, _tile_rows(j, TOK_TILE), :]
            for k in range(1, TOP_K):
                acc = acc + wb_ref[k] * buf_ref[slot, k, _tile_rows(j, TOK_TILE), :]
            parts.append(acc)
        seg = TOK_TILE // g2_ref.shape[0]
        g2 = jnp.concatenate(
            [jnp.broadcast_to(g2_ref[s], (seg, D_MODEL)) for s in range(g2_ref.shape[0])], axis=0)
        x = xs1_ref[...] + g2 * jnp.concatenate(parts, axis=1)
        y = (x * _rms(x)) * fg_ref[...]

        @pl.when(i < n_prompt_tiles)
        def _():
            yp_ref[...] = y

        @pl.when(i >= n_prompt_tiles)
        def _():
            ysm_ref[...] = y

    for slot in range(2):
        pl.when(lax.rem(i, 2) == slot)(functools.partial(step, slot))


def _combine(dst, xs1, g2_blocks, wts, final_g, ys, n_p):
    n_tok = xs1.shape[0]
    n_tiles = n_tok // TOK_TILE
    n_pt = n_p // TOK_TILE
    segs = TOK_TILE // CHUNK
    smem_tile = pl.BlockSpec((1, 1, TOK_TILE * TOP_K), lambda i: (i, 0, 0), memory_space=pltpu.SMEM)
    smem_next = pl.BlockSpec((1, 1, TOK_TILE * TOP_K), lambda i: (jnp.minimum(i + 1, n_tiles - 1), 0, 0),
                             memory_space=pltpu.SMEM)
    dst = dst.reshape(n_tiles, 1, TOK_TILE * TOP_K)
    return pl.pallas_call(
        functools.partial(_combine_body, n_pt, n_tiles),
        out_shape=[jax.ShapeDtypeStruct((n_p, D_MODEL), F32),
                   jax.ShapeDtypeStruct((n_tok - n_p, D_MODEL), F32)],
        grid=(n_tiles,),
        in_specs=[smem_tile, smem_next,
                  pl.BlockSpec((TOK_TILE, D_MODEL), lambda i: (i, 0)),
                  pl.BlockSpec((segs, 1, D_MODEL), lambda i: (i, 0, 0)),
                  pl.BlockSpec((TOK_TILE, TOP_K), lambda i: (i, 0)),
                  pl.BlockSpec((1, D_MODEL), lambda i: (0, 0)),
                  pl.BlockSpec(memory_space=pl.ANY)],
        out_specs=[pl.BlockSpec((TOK_TILE, D_MODEL), lambda i: (jnp.minimum(i, n_pt - 1), 0)),
                   pl.BlockSpec((TOK_TILE, D_MODEL), lambda i: (jnp.maximum(i - n_pt, 0), 0))],
        scratch_shapes=[pltpu.VMEM((2, TOP_K, TOK_TILE * ROW_TILES, LANES), F32),
                        pltpu.VMEM((TOP_K, TOK_TILE, LANES), F32),
                        pltpu.SemaphoreType.DMA((2,))],
        compiler_params=pltpu.CompilerParams(dimension_semantics=("arbitrary",),
                                             vmem_limit_bytes=VMEM_LIMIT),
        name="moe_combine",
    )(dst, dst, xs1, g2_blocks, wts, final_g.reshape(1, D_MODEL), ys)


def kernel(x_prompt, x_sample, c_prompt, c_sample, state_ssm_re, state_ssm_im, norm1_g, norm2_g,
           w_ada, b_ada, w_in, w_s, b_s, g_v, lam_re, lam_im, log_dt, b_re, b_im, c_re, c_im,
           d_skip, w_glu, b_glu, out_g_a, out_g_b, w_out, w_router, router_bias, w_gate, w_up,
           w_down, ws_gate, ws_up, ws_down, final_g):
    assert norm1_g.shape[0] == 1
    bp, sp, _ = x_prompt.shape
    bs, ss, _ = x_sample.shape
    n_p, n_s = bp * sp, bs * ss
    n_tok = n_p + n_s
    l = 0

    mod = _adaln(jnp.concatenate([c_prompt, c_sample], axis=0), w_ada[l], b_ada[l])
    mod = mod.reshape(bp + bs, 6, D_MODEL)

    wb, wc, tabs = _s5_tables(lam_re[l], lam_im[l], log_dt[l], b_re[l], b_im[l], c_re[l], c_im[l])
    pos = jnp.arange(SGU_LEN)
    mask = (pos[:, None] // CHUNK) >= (pos[None, :] // CHUNK)
    ws_masked = jnp.where(mask[None], w_s[l], 0.0)
    row = lax.broadcasted_iota(I32, (MIX_ROWS, MIX_ROWS), 0)
    col = lax.broadcasted_iota(I32, (MIX_ROWS, MIX_ROWS), 1)
    ltri = (col < row).astype(BF16)

    def sgu_weights(ln):
        bsf = jnp.repeat(b_s[l][:, :ln].T, HEAD_A, axis=1)
        return ws_masked[:, :ln, :ln].astype(BF16), bsf

    ws_p, bsf_p = sgu_weights(SGU_LEN)
    ws_s, bsf_s = sgu_weights(ss)
    shared = [norm1_g[l].reshape(1, D_MODEL), w_in[l].astype(BF16), g_v[l].reshape(1, D_A),
              wb, wc, tabs, d_skip[l].reshape(1, D_B), w_glu[l].astype(BF16),
              b_glu[l].reshape(1, D_B), out_g_a[l].reshape(1, D_A), out_g_b[l].reshape(1, D_B),
              w_out[l].astype(BF16), norm2_g[l].reshape(1, D_MODEL), w_router[l].astype(BF16),
              router_bias[l].reshape(1, N_EXPERTS), ws_gate[l].astype(BF16),
              ws_up[l].astype(BF16), ws_down[l].astype(BF16), ltri]
    h0_s = jnp.broadcast_to(_lay(state_ssm_re[l], state_ssm_im[l])[:, None, :], (bs, SUB, STATE_COLS))
    xs1, h2t, slab, wslab, cnt_all, hfin_p, hfin_s, v_rows = _mix(
        x_prompt, x_sample, mod, h0_s, ws_p, bsf_p, ws_s, bsf_s, shared)

    wts = wslab[:, :TOP_K]
    counts = cnt_all[0].astype(I32)
    padded = (counts + MOE_BLK - 1) // MOE_BLK * MOE_BLK
    pend = jnp.cumsum(padded).astype(I32)
    pstart = pend - padded
    n_blocks = -(-n_tok * TOP_K // MOE_BLK) + N_EXPERTS

    dst = _slots(slab, pstart)[:, :TOP_K]
    xs = _dispatch(dst, pstart, pend, h2t, n_blocks * MOE_BLK)
    ys = _experts(pstart, pend, xs, w_gate[l], w_up[l], w_down[l])

    g2 = mod[:, 5, :]
    g2_blocks = jnp.concatenate([jnp.repeat(g2[:bp], sp // CHUNK, axis=0),
                                 jnp.repeat(g2[bp:], ss // CHUNK, axis=0)], axis=0)
    y_p, y_s = _combine(dst, xs1, g2_blocks.reshape(n_tok // CHUNK, 1, D_MODEL), wts, final_g, ys, n_p)

    re_p, im_p = _unlay(hfin_p[:, 0, :])
    re_s, im_s = _unlay(hfin_s[:, 0, :])
    return (y_p.reshape(bp, sp, D_MODEL), y_s.reshape(bs, ss, D_MODEL),
            re_p[None], im_p[None], re_s[None], im_s[None], v_rows[None])
```

```python
import functools

import jax
import jax.numpy as jnp
from jax import lax
from jax.experimental import pallas as pl
from jax.experimental.pallas import tpu as pltpu

F32 = jnp.float32
BF16 = jnp.bfloat16
I32 = jnp.int32

D_MODEL = 1024
D_A = 512
D_B = 512
N_HEADS_A = 4
HEAD_A = 128
SSM_GROUP = 16
N_GROUPS_B = 32
SSM_STATE = 64
N_EXPERTS = 256
TOP_K = 8
D_EXPERT = 256
ROUTE_SCALE = 2.5
CHUNK = 64
SGU_LEN = 128
EPS = 1e-6

STATE_CHUNKS = 4
GROUPS_PER_CHUNK = N_GROUPS_B // STATE_CHUNKS
HALF = GROUPS_PER_CHUNK * SSM_STATE
STATE_COLS = STATE_CHUNKS * 2 * HALF
SCAN_W = 256
SUB = 8
LANES = 128
ROW_TILES = D_MODEL // LANES

TAB_PW = 0
TAB_MD = 8
TAB_P8 = 11

MIX_ROWS = 256
MOE_BLK = 256
DISPATCH_BUFS = 3
IN_BUFS = 4
OUT_BUFS = 2
TOK_TILE = 256
SLAB = 128
VMEM_LIMIT = 56 * 1024 * 1024


def _dot(a, b):
    return jnp.dot(a, b, preferred_element_type=F32)


def _rms(x):
    return lax.rsqrt(jnp.mean(x * x, axis=-1, keepdims=True) + EPS)


def _tile_rows(j, n):
    return pl.ds(j, n, stride=ROW_TILES)


def _adaln_body(c_ref, w_ref, b_ref, o_ref):
    c = c_ref[...]
    o_ref[...] = _dot(jax.nn.silu(c).astype(BF16), w_ref[...].astype(BF16)) + b_ref[...]


def _adaln(c, w_ada, b_ada):
    n = c.shape[0]
    cols = w_ada.shape[1]
    blk = 1536
    return pl.pallas_call(
        _adaln_body,
        out_shape=jax.ShapeDtypeStruct((n, cols), F32),
        grid=(cols // blk,),
        in_specs=[pl.BlockSpec((n, D_MODEL), lambda j: (0, 0)),
                  pl.BlockSpec((D_MODEL, blk), lambda j: (0, j)),
                  pl.BlockSpec((1, blk), lambda j: (0, j))],
        out_specs=pl.BlockSpec((n, blk), lambda j: (0, j)),
        compiler_params=pltpu.CompilerParams(dimension_semantics=("arbitrary",)),
        name="adaln",
    )(c, w_ada, b_ada.reshape(1, cols))


def _lay(re, im):
    lead = re.shape[:-2]
    re = re.reshape(lead + (STATE_CHUNKS, HALF))
    im = im.reshape(lead + (STATE_CHUNKS, HALF))
    return jnp.concatenate([re, im], axis=-1).reshape(lead + (STATE_COLS,))


def _unlay(v):
    lead = v.shape[:-1]
    v = v.reshape(lead + (STATE_CHUNKS, 2, HALF))
    re = v[..., 0, :].reshape(lead + (N_GROUPS_B, SSM_STATE))
    im = v[..., 1, :].reshape(lead + (N_GROUPS_B, SSM_STATE))
    return re, im


def _s5_tables(lam_re, lam_im, log_dt, b_re, b_im, c_re, c_im):
    dt = jnp.exp(log_dt.astype(F32))[:, None]
    lr, li = lam_re.astype(F32), lam_im.astype(F32)

    def apow(k):
        mag = jnp.exp(lr * dt * k)
        return mag * jnp.cos(li * dt * k), mag * jnp.sin(li * dt * k)

    ar, ai = apow(1.0)
    den = lr * lr + li * li
    nr, ni = ar - 1.0, ai
    kr, ki = (nr * lr + ni * li) / den, (ni * lr - nr * li) / den
    br, bi = b_re.astype(F32), b_im.astype(F32)
    bbr = kr[..., None] * br - ki[..., None] * bi
    bbi = kr[..., None] * bi + ki[..., None] * br
    eye = jnp.eye(GROUPS_PER_CHUNK, dtype=F32)

    def bproj(bb):
        bb = bb.reshape(STATE_CHUNKS, GROUPS_PER_CHUNK, SSM_STATE, SSM_GROUP)
        w = jnp.einsum("mgph,gk->mghkp", bb, eye)
        return w.reshape(STATE_CHUNKS, GROUPS_PER_CHUNK * SSM_GROUP, HALF)

    wb = jnp.concatenate([bproj(bbr), bproj(bbi)], axis=-1).astype(BF16)

    def cproj(cc):
        cc = cc.reshape(STATE_CHUNKS, GROUPS_PER_CHUNK, SSM_GROUP, SSM_STATE)
        w = jnp.einsum("mghp,gk->mgpkh", cc, eye)
        return w.reshape(STATE_CHUNKS, HALF, GROUPS_PER_CHUNK * SSM_GROUP)

    wc = jnp.concatenate([cproj(c_re.astype(F32)), cproj(-c_im.astype(F32))], axis=1).astype(BF16)

    rows = jnp.arange(SUB, dtype=F32)
    tabs = []
    for i in range(SUB):
        pr, pi = apow(float(i + 1))
        tabs.append(jnp.broadcast_to(_lay(pr, pi)[None], (SUB, STATE_COLS)))
    for d in (1, 2, 4):
        pr, pi = apow(float(SUB * d))
        keep = (rows >= d).astype(F32)[:, None]
        tabs.append(_lay(pr, pi)[None] * keep)
    pr, pi = apow(SUB * rows[:, None, None])
    tabs.append(_lay(pr, pi))
    return wb, wc, jnp.stack(tabs)


def _cmul(ar, ai, br, bi):
    return ar * br - ai * bi, ar * bi + ai * br


def _s5_scan_block(bu_ref, row0, h_ref, tab_ref):
    row_id = lax.broadcasted_iota(I32, (SUB, SCAN_W), 0)
    tiles = SCAN_W // LANES
    for m in range(STATE_CHUNKS):
        for hf in range(HALF // SCAN_W):
            c_re0 = m * 2 * HALF + hf * SCAN_W
            c_im0 = c_re0 + HALF
            cre = pl.ds(c_re0, SCAN_W)
            cim = pl.ds(c_im0, SCAN_W)

            def tab(slot):
                return tab_ref[slot, :, cre], tab_ref[slot, :, cim]

            def load(i, c0):
                rows = pl.ds(row0 + i, SUB, stride=SUB)
                return jnp.concatenate([bu_ref[c0 // LANES + j, rows, :] for j in range(tiles)], axis=1)

            def store(i, c0, val):
                rows = pl.ds(row0 + i, SUB, stride=SUB)
                for j in range(tiles):
                    bu_ref[c0 // LANES + j, rows, :] = val[:, j * LANES:(j + 1) * LANES]

            a_re, a_im = tab(TAB_PW)
            s_re = load(0, c_re0)
            s_im = load(0, c_im0)
            loc = [(s_re, s_im)]
            for i in range(1, SUB):
                p_re, p_im = _cmul(a_re, a_im, s_re, s_im)
                s_re = p_re + load(i, c_re0)
                s_im = p_im + load(i, c_im0)
                loc.append((s_re, s_im))
            e_re, e_im = s_re, s_im
            for n, d in enumerate((1, 2, 4)):
                m_re, m_im = tab(TAB_MD + n)
                q_re, q_im = _cmul(m_re, m_im, pltpu.roll(e_re, d, 0), pltpu.roll(e_im, d, 0))
                e_re, e_im = e_re + q_re, e_im + q_im
            p8_re, p8_im = tab(TAB_P8)
            c_re, c_im = _cmul(p8_re, p8_im, h_ref[:, cre], h_ref[:, cim])
            c_re = c_re + jnp.where(row_id >= 1, pltpu.roll(e_re, 1, 0), 0.0)
            c_im = c_im + jnp.where(row_id >= 1, pltpu.roll(e_im, 1, 0), 0.0)
            for i in range(SUB):
                w_re, w_im = tab(TAB_PW + i)
                q_re, q_im = _cmul(w_re, w_im, c_re, c_im)
                f_re, f_im = loc[i][0] + q_re, loc[i][1] + q_im
                store(i, c_re0, f_re)
                store(i, c_im0, f_im)
            h_ref[:, cre] = jnp.broadcast_to(f_re[SUB - 1:SUB, :], (SUB, SCAN_W))
            h_ref[:, cim] = jnp.broadcast_to(f_im[SUB - 1:SUB, :], (SUB, SCAN_W))


def _mix_tile(seqs, ln, chain, new_seq, x_ref, mod_ref, h0_ref, ws_ref, bsf_ref, shared, outs,
              hfin_ref, v_ref, scratch):
    (n1g_ref, win_ref, gv_ref, wb_ref, wc_ref, tab_ref, dsk_ref, wglu_ref, bglu_ref, oga_ref,
     ogb_ref, wout_ref, n2g_ref, wr_ref, rb_ref, wsg_ref, wsu_ref, wsd_ref, ltri_ref) = shared
    xs1_ref, h2t_ref, slab_ref, wslab_ref, cnt_ref = outs
    bu_ref, h_scr, cnt_scr = scratch
    t_rows = MIX_ROWS
    rows_per_seq = t_rows // seqs

    x = x_ref[...].reshape(t_rows, D_MODEL)

    def modrow(j):
        parts = [jnp.broadcast_to(mod_ref[s, j:j + 1, :], (rows_per_seq, D_MODEL)) for s in range(seqs)]
        return parts[0] if seqs == 1 else jnp.concatenate(parts, axis=0)

    h = (x * _rms(x)) * n1g_ref[...] * (1.0 + modrow(1)) + modrow(0)
    z = _dot(h.astype(BF16), win_ref[...])
    u_a = z[:, :D_A]
    v_a = z[:, D_A:2 * D_A]
    u_b = z[:, 2 * D_A:]

    vh_parts = []
    for hh in range(N_HEADS_A):
        cols = slice(hh * HEAD_A, (hh + 1) * HEAD_A)
        vv = v_a[:, cols]
        vh_parts.append((vv * _rms(vv)) * gv_ref[:, cols])
    if v_ref is not None:
        v_ref[...] = jnp.concatenate(vh_parts, axis=1).reshape(seqs, rows_per_seq, D_A)
    n_sgu = t_rows // ln
    s_parts = []
    for hh in range(N_HEADS_A):
        vb = vh_parts[hh].astype(BF16)
        w_h = ws_ref[hh]
        s_parts.append(jnp.concatenate(
            [_dot(w_h, vb[c * ln:(c + 1) * ln, :]) for c in range(n_sgu)], axis=0))
    bsf = bsf_ref[...]
    s_mix = jnp.concatenate(s_parts, axis=1) + jnp.concatenate([bsf] * n_sgu, axis=0)
    y_a = u_a * s_mix

    ub16 = u_b.astype(BF16)
    gc = GROUPS_PER_CHUNK * SSM_GROUP
    tiles_per_chunk = 2 * HALF // LANES
    for m in range(STATE_CHUNKS):
        bu = _dot(ub16[:, m * gc:(m + 1) * gc], wb_ref[m])
        for j in range(tiles_per_chunk):
            bu_ref[m * tiles_per_chunk + j] = bu[:, j * LANES:(j + 1) * LANES]
    n_blk = t_rows // CHUNK
    if chain:
        @pl.when(new_seq)
        def _():
            h_scr[...] = jnp.zeros_like(h_scr)
    for blk in range(n_blk):
        if not chain:
            h_scr[...] = h0_ref[blk]
        _s5_scan_block(bu_ref, blk * CHUNK, h_scr, tab_ref)
        if not chain:
            hfin_ref[blk] = h_scr[...]
    if chain:
        hfin_ref[0] = h_scr[...]
    y_parts = []
    for m in range(STATE_CHUNKS):
        st = jnp.concatenate([bu_ref[m * tiles_per_chunk + j] for j in range(tiles_per_chunk)], axis=1)
        y_parts.append(_dot(st.astype(BF16), wc_ref[m]))
    y_s = jnp.concatenate(y_parts, axis=1) + dsk_ref[...] * u_b
    g_b = jax.nn.gelu(y_s)
    y_b = g_b * jax.nn.sigmoid(_dot(g_b.astype(BF16), wglu_ref[...]) + bglu_ref[...])

    na = (y_a * _rms(y_a)) * oga_ref[...]
    nb = (y_b * _rms(y_b)) * ogb_ref[...]
    mix = _dot(jnp.concatenate([na, nb], axis=1).astype(BF16), wout_ref[...])
    x1 = x + modrow(2) * mix

    h2 = (x1 * _rms(x1)) * n2g_ref[...] * (1.0 + modrow(4)) + modrow(3)
    for j in range(ROW_TILES):
        h2t_ref[_tile_rows(j, t_rows), :] = h2[:, j * LANES:(j + 1) * LANES]
    h2b = h2.astype(BF16)
    act = jax.nn.silu(_dot(h2b, wsg_ref[...])) * _dot(h2b, wsu_ref[...])
    shared_out = _dot(act.astype(BF16), wsd_ref[...])
    xs1_ref[...] = x1 + modrow(5) * shared_out

    scores = jax.nn.sigmoid(_dot(h2b, wr_ref[...]))
    lane = lax.broadcasted_iota(I32, (t_rows, N_EXPERTS), 1).astype(F32)
    work = scores + rb_ref[...]
    onehot = jnp.zeros((t_rows, N_EXPERTS), F32)
    idxs, sels = [], []
    for _ in range(TOP_K):
        top = jnp.max(work, axis=-1, keepdims=True)
        idx = jnp.min(jnp.where(work == top, lane, float(N_EXPERTS)), axis=-1, keepdims=True)
        pick = lane == idx
        sels.append(jnp.sum(jnp.where(pick, scores, 0.0), axis=-1, keepdims=True))
        idxs.append(idx)
        work = jnp.where(pick, -jnp.inf, work)
        onehot = jnp.where(pick, 1.0, onehot)
    total = sels[0]
    for k in range(1, TOP_K):
        total = total + sels[k]
    ranktab = _dot(ltri_ref[...], onehot.astype(BF16)) + cnt_scr[0:1, :]
    slab_lane = lax.broadcasted_iota(I32, (t_rows, SLAB), 1)
    slab = jnp.zeros((t_rows, SLAB), I32)
    wslab = jnp.zeros((t_rows, SLAB), F32)
    for k in range(TOP_K):
        rank = jnp.sum(jnp.where(lane == idxs[k], ranktab, 0.0), axis=-1, keepdims=True)
        slab = jnp.where(slab_lane == k, idxs[k].astype(I32), slab)
        slab = jnp.where(slab_lane == TOP_K + k, rank.astype(I32), slab)
        wslab = jnp.where(slab_lane == k, ROUTE_SCALE * sels[k] / total, wslab)
    slab_ref[...] = slab
    wslab_ref[...] = wslab
    cnt_scr[...] = cnt_scr[...] + jnp.sum(onehot, axis=0, keepdims=True)
    cnt_ref[...] = cnt_scr[...]


def _mix_body(n_prompt_tiles, tiles_per_seq, seqs_s, ln_s,
              xp_ref, xsm_ref, modp_ref, mods_ref, h0s_ref, wsp_ref, bsfp_ref, wss_ref, bsfs_ref,
              *rest):
    shared = rest[:19]
    xs1_ref, h2t_ref, slab_ref, wslab_ref, cnt_ref, hfp_ref, hfs_ref, v_ref = rest[19:27]
    scratch = rest[27:]
    outs = (xs1_ref, h2t_ref, slab_ref, wslab_ref, cnt_ref)
    s = pl.program_id(0)

    @pl.when(s == 0)
    def _():
        scratch[2][...] = jnp.zeros_like(scratch[2])

    @pl.when(s < n_prompt_tiles)
    def _():
        _mix_tile(1, SGU_LEN, True, lax.rem(s, tiles_per_seq) == 0, xp_ref, modp_ref, None,
                  wsp_ref, bsfp_ref, shared, outs, hfp_ref, None, scratch)

    @pl.when(s >= n_prompt_tiles)
    def _():
        _mix_tile(seqs_s, ln_s, False, None, xsm_ref, mods_ref, h0s_ref,
                  wss_ref, bsfs_ref, shared, outs, hfs_ref, v_ref, scratch)


def _mix(x_prompt, x_sample, mod, h0_s, ws_p, bsf_p, ws_s, bsf_s, shared):
    bp, sp, _ = x_prompt.shape
    bs, ss, _ = x_sample.shape
    assert sp % MIX_ROWS == 0 and MIX_ROWS % ss == 0 and ss == CHUNK
    seqs_s = MIX_ROWS // ss
    assert bs % seqs_s == 0 and bp % seqs_s == 0
    tiles_per_seq = sp // MIX_ROWS
    n_pt = bp * tiles_per_seq
    n_st = bs // seqs_s
    n_tok = bp * sp + bs * ss

    def p_tile(s):
        return jnp.minimum(s, n_pt - 1)

    def s_tile(s):
        return jnp.maximum(s - n_pt, 0)

    def const(shape):
        nd = len(shape)
        return pl.BlockSpec(shape, lambda s: (0,) * nd)

    in_specs = [
        pl.BlockSpec((1, MIX_ROWS, D_MODEL), lambda s: (p_tile(s) // tiles_per_seq, p_tile(s) % tiles_per_seq, 0)),
        pl.BlockSpec((seqs_s, ss, D_MODEL), lambda s: (s_tile(s), 0, 0)),
        pl.BlockSpec((1, 6, D_MODEL), lambda s: (p_tile(s) // tiles_per_seq, 0, 0)),
        pl.BlockSpec((seqs_s, 6, D_MODEL), lambda s: (bp // seqs_s + s_tile(s), 0, 0)),
        pl.BlockSpec((seqs_s, SUB, STATE_COLS), lambda s: (s_tile(s), 0, 0)),
        const(ws_p.shape), const(bsf_p.shape), const(ws_s.shape), const(bsf_s.shape),
    ] + [const(w.shape) for w in shared]
    out_shape = [
        jax.ShapeDtypeStruct((n_tok, D_MODEL), F32),
        jax.ShapeDtypeStruct((n_tok * ROW_TILES, LANES), F32),
        jax.ShapeDtypeStruct((n_tok, SLAB), I32),
        jax.ShapeDtypeStruct((n_tok, SLAB), F32),
        jax.ShapeDtypeStruct((SUB, N_EXPERTS), F32),
        jax.ShapeDtypeStruct((bp, SUB, STATE_COLS), F32),
        jax.ShapeDtypeStruct((bs, SUB, STATE_COLS), F32),
        jax.ShapeDtypeStruct((bs, ss, D_A), F32),
    ]
    out_specs = [
        pl.BlockSpec((MIX_ROWS, D_MODEL), lambda s: (s, 0)),
        pl.BlockSpec((MIX_ROWS * ROW_TILES, LANES), lambda s: (s, 0)),
        pl.BlockSpec((MIX_ROWS, SLAB), lambda s: (s, 0)),
        pl.BlockSpec((MIX_ROWS, SLAB), lambda s: (s, 0)),
        const((SUB, N_EXPERTS)),
        pl.BlockSpec((1, SUB, STATE_COLS), lambda s: (p_tile(s) // tiles_per_seq, 0, 0)),
        pl.BlockSpec((seqs_s, SUB, STATE_COLS), lambda s: (s_tile(s), 0, 0)),
        pl.BlockSpec((seqs_s, ss, D_A), lambda s: (s_tile(s), 0, 0)),
    ]
    return pl.pallas_call(
        functools.partial(_mix_body, n_pt, tiles_per_seq, seqs_s, ss),
        out_shape=out_shape,
        grid=(n_pt + n_st,),
        in_specs=in_specs,
        out_specs=out_specs,
        scratch_shapes=[pltpu.VMEM((STATE_COLS // LANES, MIX_ROWS, LANES), F32),
                        pltpu.VMEM((SUB, STATE_COLS), F32),
                        pltpu.VMEM((SUB, N_EXPERTS), F32)],
        compiler_params=pltpu.CompilerParams(dimension_semantics=("arbitrary",),
                                             vmem_limit_bytes=VMEM_LIMIT),
        name="mix",
    )(x_prompt, x_sample, mod, mod, h0_s, ws_p, bsf_p, ws_s, bsf_s, *shared)


SLOT_TILE = 1024


DIGIT = 256


def _slots_body(slab_ref, digits_ref, dst_ref):
    slab = slab_ref[...]
    digits = digits_ref[...]
    lane = lax.broadcasted_iota(I32, (SLOT_TILE, N_EXPERTS), 1)
    out_lane = lax.broadcasted_iota(I32, (SLOT_TILE, SLAB), 1)
    start = jnp.zeros((SLOT_TILE, SLAB), F32)
    for k in range(TOP_K):
        onehot = jnp.where(lane == slab[:, k:k + 1], 1.0, 0.0).astype(BF16)
        d = _dot(onehot, digits)
        value = d[:, :SLAB] + float(DIGIT) * d[:, SLAB:2 * SLAB] + float(DIGIT * DIGIT) * d[:, 2 * SLAB:]
        start = jnp.where(out_lane == k, value, start)
    ranks = pltpu.roll(slab, SLAB - TOP_K, 1)
    dst_ref[...] = jnp.where(out_lane < TOP_K, start.astype(I32) + ranks, 0)


def _slots(slab, pstart):
    n_tok = slab.shape[0]
    assert n_tok % SLOT_TILE == 0
    parts = [pstart % DIGIT, (pstart // DIGIT) % DIGIT, pstart // (DIGIT * DIGIT)]
    digits = jnp.concatenate([jnp.broadcast_to(p[:, None], (N_EXPERTS, SLAB)) for p in parts], axis=1)
    return pl.pallas_call(
        _slots_body,
        out_shape=jax.ShapeDtypeStruct((n_tok, SLAB), I32),
        grid=(n_tok // SLOT_TILE,),
        in_specs=[pl.BlockSpec((SLOT_TILE, SLAB), lambda i: (i, 0)),
                  pl.BlockSpec((N_EXPERTS, 3 * SLAB), lambda i: (0, 0))],
        out_specs=pl.BlockSpec((SLOT_TILE, SLAB), lambda i: (i, 0)),
        compiler_params=pltpu.CompilerParams(dimension_semantics=("arbitrary",)),
        name="moe_slots",
    )(slab, digits.astype(BF16))


def _token_rows(t):
    return pl.ds(pl.multiple_of(t * ROW_TILES, ROW_TILES), ROW_TILES)


def _dispatch_body(dst_ref, pstart_ref, pend_ref, h2t_hbm, xs_hbm, tile_ref, zero_ref, in_sems, row_sems,
                   zsem):
    i = pl.program_id(0)
    n_tiles = pl.num_programs(0)
    blk_rows = MOE_BLK * ROW_TILES
    tile_rows = TOK_TILE * ROW_TILES
    n_blocks = xs_hbm.shape[0] // blk_rows

    def fetch(tile):
        slot = lax.rem(tile, DISPATCH_BUFS)
        src = h2t_hbm.at[pl.ds(pl.multiple_of(tile * tile_rows, tile_rows), tile_rows)]
        return pltpu.make_async_copy(src, tile_ref.at[slot], in_sems.at[slot])

    def drain(tile):
        for _ in range(TOP_K):
            pltpu.make_async_copy(tile_ref.at[0], xs_hbm.at[pl.ds(0, tile_rows)],
                                  row_sems.at[lax.rem(tile, 2)]).wait()

    @pl.when(i == 0)
    def _():
        fetch(0).start()

    @pl.when(i + 1 < n_tiles)
    def _():
        fetch(i + 1).start()

    @pl.when(i == 0)
    def _():
        zero_ref[...] = jnp.zeros_like(zero_ref)

        def clear(blk):
            dst = xs_hbm.at[pl.ds(pl.multiple_of(blk * blk_rows, blk_rows), blk_rows)]
            return pltpu.make_async_copy(zero_ref, dst, zsem)

        def each(fn):
            def fill(e, carry):
                @pl.when(pend_ref[e] > pstart_ref[e])
                def _():
                    fn(clear(pend_ref[e] // MOE_BLK - 1))
                return carry
            lax.fori_loop(0, N_EXPERTS, fill, 0)

            def tail(b, carry):
                fn(clear(b))
                return carry
            lax.fori_loop(pend_ref[N_EXPERTS - 1] // MOE_BLK, n_blocks, tail, 0)

        each(lambda cp: cp.start())
        each(lambda cp: cp.wait())

    fetch(i).wait()
    src_tile = tile_ref.at[lax.rem(i, DISPATCH_BUFS)]
    row_sem = row_sems.at[lax.rem(i, 2)]

    def issue(t, carry):
        for k in range(TOP_K):
            dst = dst_ref[0, 0, t * TOP_K + k]
            pltpu.make_async_copy(src_tile.at[_token_rows(t)], xs_hbm.at[_token_rows(dst)],
                                  row_sem).start(priority=k % 2)
        return carry
    lax.fori_loop(0, TOK_TILE, issue, 0)

    @pl.when(i > 0)
    def _():
        drain(i - 1)

    @pl.when(i == n_tiles - 1)
    def _():
        drain(i)


def _dispatch(dst, pstart, pend, h2t, n_rows):
    n_tiles = h2t.shape[0] // (TOK_TILE * ROW_TILES)
    smem_tile = pl.BlockSpec((1, 1, TOK_TILE * TOP_K), lambda i: (i, 0, 0), memory_space=pltpu.SMEM)
    smem_all = pl.BlockSpec(memory_space=pltpu.SMEM)
    return pl.pallas_call(
        _dispatch_body,
        out_shape=jax.ShapeDtypeStruct((n_rows * ROW_TILES, LANES), F32),
        grid=(n_tiles,),
        in_specs=[smem_tile, smem_all, smem_all, pl.BlockSpec(memory_space=pl.ANY)],
        out_specs=pl.BlockSpec(memory_space=pl.ANY),
        scratch_shapes=[pltpu.VMEM((DISPATCH_BUFS, TOK_TILE * ROW_TILES, LANES), F32),
                        pltpu.VMEM((MOE_BLK * ROW_TILES, LANES), F32),
                        pltpu.SemaphoreType.DMA((DISPATCH_BUFS,)), pltpu.SemaphoreType.DMA((2,)),
                        pltpu.SemaphoreType.DMA],
        compiler_params=pltpu.CompilerParams(dimension_semantics=("arbitrary",)),
        name="moe_dispatch",
    )(dst.reshape(n_tiles, 1, TOK_TILE * TOP_K), pstart, pend, h2t)


def _experts_body(pstart_ref, pend_ref, xs_hbm, wg_ref, wu_ref, wd_ref, ys_hbm,
                  xbuf, obuf, wgu_scr, wd_scr, in_sems, out_sems):
    e = pl.program_id(0)
    blk_rows = MOE_BLK * ROW_TILES
    n_blocks = xs_hbm.shape[0] // blk_rows
    n_used = pend_ref[N_EXPERTS - 1] // MOE_BLK
    first_blk = pstart_ref[e] // MOE_BLK
    last_blk = pend_ref[e] // MOE_BLK

    def block(ref, blk):
        return ref.at[pl.ds(pl.multiple_of(blk * blk_rows, blk_rows), blk_rows)]

    def fetch(blk):
        slot = lax.rem(blk, IN_BUFS)
        return pltpu.make_async_copy(block(xs_hbm, blk), xbuf.at[slot], in_sems.at[slot])

    def put(blk):
        slot = lax.rem(blk, OUT_BUFS)
        return pltpu.make_async_copy(obuf.at[slot], block(ys_hbm, blk), out_sems.at[slot])

    def free_out(blk):
        @pl.when(blk >= OUT_BUFS)
        def _():
            put(blk - OUT_BUFS).wait()

    @pl.when(e == 0)
    def _():
        for first in range(IN_BUFS - 1):
            @pl.when(first < n_used)
            def _():
                fetch(first).start()

    @pl.when(last_blk > first_blk)
    def _():
        wgu_scr[:, :D_EXPERT] = wg_ref[0].astype(BF16)
        wgu_scr[:, D_EXPERT:] = wu_ref[0].astype(BF16)
        wd_scr[...] = wd_ref[0].astype(BF16)

    def one_block(b, carry):
        free_out(b)
        ahead = b + IN_BUFS - 1

        @pl.when(ahead < n_used)
        def _():
            fetch(ahead).start()
        fetch(b).wait()
        rows_in = xbuf.at[lax.rem(b, IN_BUFS)]
        out = obuf.at[lax.rem(b, OUT_BUFS)]
        xb = jnp.concatenate([rows_in[_tile_rows(j, MOE_BLK), :] for j in range(ROW_TILES)],
                             axis=1).astype(BF16)
        gu = _dot(xb, wgu_scr[...])
        act = (jax.nn.silu(gu[:, :D_EXPERT]) * gu[:, D_EXPERT:]).astype(BF16)
        y = _dot(act, wd_scr[...])
        for j in range(ROW_TILES):
            out[_tile_rows(j, MOE_BLK), :] = y[:, j * LANES:(j + 1) * LANES]
        put(b).start()
        return carry
    lax.fori_loop(first_blk, last_blk, one_block, 0)

    @pl.when(e == N_EXPERTS - 1)
    def _():
        def tail(b, carry):
            free_out(b)
            obuf[lax.rem(b, OUT_BUFS)] = jnp.zeros((blk_rows, LANES), F32)
            put(b).start()
            return carry
        lax.fori_loop(n_used, n_blocks, tail, 0)
        for back in range(OUT_BUFS):
            put(n_blocks - 1 - back).wait()


def _experts(pstart, pend, xs, w_gate, w_up, w_down):
    blk_rows = MOE_BLK * ROW_TILES
    n_blocks = xs.shape[0] // blk_rows
    assert n_blocks >= OUT_BUFS

    def w_map(e, pstart_ref, pend_ref):
        return (e, 0, 0)

    return pl.pallas_call(
        _experts_body,
        out_shape=jax.ShapeDtypeStruct(xs.shape, F32),
        grid_spec=pltpu.PrefetchScalarGridSpec(
            num_scalar_prefetch=2,
            grid=(N_EXPERTS,),
            in_specs=[pl.BlockSpec(memory_space=pl.ANY),
                      pl.BlockSpec((1, D_MODEL, D_EXPERT), w_map),
                      pl.BlockSpec((1, D_MODEL, D_EXPERT), w_map),
                      pl.BlockSpec((1, D_EXPERT, D_MODEL), w_map)],
            out_specs=pl.BlockSpec(memory_space=pl.ANY),
            scratch_shapes=[pltpu.VMEM((IN_BUFS, blk_rows, LANES), F32),
                            pltpu.VMEM((OUT_BUFS, blk_rows, LANES), F32),
                            pltpu.VMEM((D_MODEL, 2 * D_EXPERT), BF16),
                            pltpu.VMEM((D_EXPERT, D_MODEL), BF16),
                            pltpu.SemaphoreType.DMA((IN_BUFS,)),
                            pltpu.SemaphoreType.DMA((OUT_BUFS,))]),
        compiler_params=pltpu.CompilerParams(dimension_semantics=("arbitrary",),
                                             vmem_limit_bytes=VMEM_LIMIT),
        name="moe_experts",
    )(pstart, pend, xs, w_gate, w_up, w_down)


def _combine_body(n_prompt_tiles, n_tiles, dst_ref, dnext_ref, xs1_ref, g2_ref, wts_ref, fg_ref,
                  ys_hbm, yp_ref, ysm_ref, buf_ref, wb_ref, sems):
    i = pl.program_id(0)

    def issue(d_ref, slot):
        def body(t, carry):
            for k in range(TOP_K):
                src = d_ref[0, 0, t * TOP_K + k]
                pltpu.make_async_copy(ys_hbm.at[_token_rows(src)], buf_ref.at[slot, k, _token_rows(t)],
                                      sems.at[slot]).start(priority=k % 2)
            return carry
        lax.fori_loop(0, TOK_TILE, body, 0)

    def drain(slot):
        for k in range(TOP_K):
            pltpu.make_async_copy(ys_hbm.at[pl.ds(0, TOK_TILE * ROW_TILES)], buf_ref.at[slot, k],
                                  sems.at[slot]).wait()

    @pl.when(i == 0)
    def _():
        issue(dst_ref, 0)

    def step(slot):
        @pl.when(i + 1 < n_tiles)
        def _():
            issue(dnext_ref, 1 - slot)
        wts = wts_ref[...]
        for k in range(TOP_K):
            wb_ref[k] = jnp.broadcast_to(wts[:, k:k + 1], (TOK_TILE, LANES))
        drain(slot)
        parts = []
        for j in range(ROW_TILES):
            acc = wb_ref[0] * buf_ref[slot, 0, _tile_rows(j, TOK_TILE), :]
            for k in range(1, TOP_K):
                acc = acc + wb_ref[k] * buf_ref[slot, k, _tile_rows(j, TOK_TILE), :]
            parts.append(acc)
        seg = TOK_TILE // g2_ref.shape[0]
        g2 = jnp.concatenate(
            [jnp.broadcast_to(g2_ref[s], (seg, D_MODEL)) for s in range(g2_ref.shape[0])], axis=0)
        x = xs1_ref[...] + g2 * jnp.concatenate(parts, axis=1)
        y = (x * _rms(x)) * fg_ref[...]

        @pl.when(i < n_prompt_tiles)
        def _():
            yp_ref[...] = y

        @pl.when(i >= n_prompt_tiles)
        def _():
            ysm_ref[...] = y

    for slot in range(2):
        pl.when(lax.rem(i, 2) == slot)(functools.partial(step, slot))


def _combine(dst, xs1, g2_blocks, wts, final_g, ys, n_p):
    n_tok = xs1.shape[0]
    n_tiles = n_tok // TOK_TILE
    n_pt = n_p // TOK_TILE
    segs = TOK_TILE // CHUNK
    smem_tile = pl.BlockSpec((1, 1, TOK_TILE * TOP_K), lambda i: (i, 0, 0), memory_space=pltpu.SMEM)
    smem_next = pl.BlockSpec((1, 1, TOK_TILE * TOP_K), lambda i: (jnp.minimum(i + 1, n_tiles - 1), 0, 0),
                             memory_space=pltpu.SMEM)
    dst = dst.reshape(n_tiles, 1, TOK_TILE * TOP_K)
    return pl.pallas_call(
        functools.partial(_combine_body, n_pt, n_tiles),
        out_shape=[jax.ShapeDtypeStruct((n_p, D_MODEL), F32),
                   jax.ShapeDtypeStruct((n_tok - n_p, D_MODEL), F32)],
        grid=(n_tiles,),
        in_specs=[smem_tile, smem_next,
                  pl.BlockSpec((TOK_TILE, D_MODEL), lambda i: (i, 0)),
                  pl.BlockSpec((segs, 1, D_MODEL), lambda i: (i, 0, 0)),
                  pl.BlockSpec((TOK_TILE, TOP_K), lambda i: (i, 0)),
                  pl.BlockSpec((1, D_MODEL), lambda i: (0, 0)),
                  pl.BlockSpec(memory_space=pl.ANY)],
        out_specs=[pl.BlockSpec((TOK_TILE, D_MODEL), lambda i: (jnp.minimum(i, n_pt - 1), 0)),
                   pl.BlockSpec((TOK_TILE, D_MODEL), lambda i: (jnp.maximum(i - n_pt, 0), 0))],
        scratch_shapes=[pltpu.VMEM((2, TOP_K, TOK_TILE * ROW_TILES, LANES), F32),
                        pltpu.VMEM((TOP_K, TOK_TILE, LANES), F32),
                        pltpu.SemaphoreType.DMA((2,))],
        compiler_params=pltpu.CompilerParams(dimension_semantics=("arbitrary",),
                                             vmem_limit_bytes=VMEM_LIMIT),
        name="moe_combine",
    )(dst, dst, xs1, g2_blocks, wts, final_g.reshape(1, D_MODEL), ys)


def kernel(x_prompt, x_sample, c_prompt, c_sample, state_ssm_re, state_ssm_im, norm1_g, norm2_g,
           w_ada, b_ada, w_in, w_s, b_s, g_v, lam_re, lam_im, log_dt, b_re, b_im, c_re, c_im,
           d_skip, w_glu, b_glu, out_g_a, out_g_b, w_out, w_router, router_bias, w_gate, w_up,
           w_down, ws_gate, ws_up, ws_down, final_g):
    assert norm1_g.shape[0] == 1
    bp, sp, _ = x_prompt.shape
    bs, ss, _ = x_sample.shape
    n_p, n_s = bp * sp, bs * ss
    n_tok = n_p + n_s
    l = 0

    mod = _adaln(jnp.concatenate([c_prompt, c_sample], axis=0), w_ada[l], b_ada[l])
    mod = mod.reshape(bp + bs, 6, D_MODEL)

    wb, wc, tabs = _s5_tables(lam_re[l], lam_im[l], log_dt[l], b_re[l], b_im[l], c_re[l], c_im[l])
    pos = jnp.arange(SGU_LEN)
    mask = (pos[:, None] // CHUNK) >= (pos[None, :] // CHUNK)
    ws_masked = jnp.where(mask[None], w_s[l], 0.0)
    row = lax.broadcasted_iota(I32, (MIX_ROWS, MIX_ROWS), 0)
    col = lax.broadcasted_iota(I32, (MIX_ROWS, MIX_ROWS), 1)
    ltri = (col < row).astype(BF16)

    def sgu_weights(ln):
        bsf = jnp.repeat(b_s[l][:, :ln].T, HEAD_A, axis=1)
        return ws_masked[:, :ln, :ln].astype(BF16), bsf

    ws_p, bsf_p = sgu_weights(SGU_LEN)
    ws_s, bsf_s = sgu_weights(ss)
    shared = [norm1_g[l].reshape(1, D_MODEL), w_in[l].astype(BF16), g_v[l].reshape(1, D_A),
              wb, wc, tabs, d_skip[l].reshape(1, D_B), w_glu[l].astype(BF16),
              b_glu[l].reshape(1, D_B), out_g_a[l].reshape(1, D_A), out_g_b[l].reshape(1, D_B),
              w_out[l].astype(BF16), norm2_g[l].reshape(1, D_MODEL), w_router[l].astype(BF16),
              router_bias[l].reshape(1, N_EXPERTS), ws_gate[l].astype(BF16),
              ws_up[l].astype(BF16), ws_down[l].astype(BF16), ltri]
    h0_s = jnp.broadcast_to(_lay(state_ssm_re[l], state_ssm_im[l])[:, None, :], (bs, SUB, STATE_COLS))
    xs1, h2t, slab, wslab, cnt_all, hfin_p, hfin_s, v_rows = _mix(
        x_prompt, x_sample, mod, h0_s, ws_p, bsf_p, ws_s, bsf_s, shared)

    wts = wslab[:, :TOP_K]
    counts = cnt_all[0].astype(I32)
    padded = (counts + MOE_BLK - 1) // MOE_BLK * MOE_BLK
    pend = jnp.cumsum(padded).astype(I32)
    pstart = pend - padded
    n_blocks = -(-n_tok * TOP_K // MOE_BLK) + N_EXPERTS

    dst = _slots(slab, pstart)[:, :TOP_K]
    xs = _dispatch(dst, pstart, pend, h2t, n_blocks * MOE_BLK)
    ys = _experts(pstart, pend, xs, w_gate[l], w_up[l], w_down[l])

    g2 = mod[:, 5, :]
    g2_blocks = jnp.concatenate([jnp.repeat(g2[:bp], sp // CHUNK, axis=0),
                                 jnp.repeat(g2[bp:], ss // CHUNK, axis=0)], axis=0)
    y_p, y_s = _combine(dst, xs1, g2_blocks.reshape(n_tok // CHUNK, 1, D_MODEL), wts, final_g, ys, n_p)

    re_p, im_p = _unlay(hfin_p[:, 0, :])
    re_s, im_s = _unlay(hfin_s[:, 0, :])
    return (y_p.reshape(bp, sp, D_MODEL), y_s.reshape(bs, ss, D_MODEL),
            re_p[None], im_p[None], re_s[None], im_s[None], v_rows[None])
```

```python
import functools

import jax
import jax.numpy as jnp
from jax import lax
from jax.experimental import pallas as pl
from jax.experimental.pallas import tpu as pltpu

F32 = jnp.float32
BF16 = jnp.bfloat16
I32 = jnp.int32

D_MODEL = 1024
D_A = 512
D_B = 512
N_HEADS_A = 4
HEAD_A = 128
SSM_GROUP = 16
N_GROUPS_B = 32
SSM_STATE = 64
N_EXPERTS = 256
TOP_K = 8
D_EXPERT = 256
ROUTE_SCALE = 2.5
CHUNK = 64
SGU_LEN = 128
EPS = 1e-6

STATE_CHUNKS = 4
GROUPS_PER_CHUNK = N_GROUPS_B // STATE_CHUNKS
HALF = GROUPS_PER_CHUNK * SSM_STATE
STATE_COLS = STATE_CHUNKS * 2 * HALF
SCAN_W = 256
SUB = 8
LANES = 128
ROW_TILES = D_MODEL // LANES

TAB_PW = 0
TAB_MD = 8
TAB_P8 = 11

MIX_ROWS = 256
MOE_BLK = 256
DISPATCH_BUFS = 3
IN_BUFS = 6
OUT_BUFS = 3
TOK_TILE = 256
SLAB = 128
VMEM_LIMIT = 56 * 1024 * 1024


def _dot(a, b):
    return jnp.dot(a, b, preferred_element_type=F32)


def _rms(x):
    return lax.rsqrt(jnp.mean(x * x, axis=-1, keepdims=True) + EPS)


def _tile_rows(j, n):
    return pl.ds(j, n, stride=ROW_TILES)


def _adaln_body(c_ref, w_ref, b_ref, o_ref):
    c = c_ref[...]
    o_ref[...] = _dot(jax.nn.silu(c).astype(BF16), w_ref[...].astype(BF16)) + b_ref[...]


def _adaln(c, w_ada, b_ada):
    n = c.shape[0]
    cols = w_ada.shape[1]
    blk = 1536
    return pl.pallas_call(
        _adaln_body,
        out_shape=jax.ShapeDtypeStruct((n, cols), F32),
        grid=(cols // blk,),
        in_specs=[pl.BlockSpec((n, D_MODEL), lambda j: (0, 0)),
                  pl.BlockSpec((D_MODEL, blk), lambda j: (0, j)),
                  pl.BlockSpec((1, blk), lambda j: (0, j))],
        out_specs=pl.BlockSpec((n, blk), lambda j: (0, j)),
        compiler_params=pltpu.CompilerParams(dimension_semantics=("arbitrary",)),
        name="adaln",
    )(c, w_ada, b_ada.reshape(1, cols))


def _lay(re, im):
    lead = re.shape[:-2]
    re = re.reshape(lead + (STATE_CHUNKS, HALF))
    im = im.reshape(lead + (STATE_CHUNKS, HALF))
    return jnp.concatenate([re, im], axis=-1).reshape(lead + (STATE_COLS,))


def _unlay(v):
    lead = v.shape[:-1]
    v = v.reshape(lead + (STATE_CHUNKS, 2, HALF))
    re = v[..., 0, :].reshape(lead + (N_GROUPS_B, SSM_STATE))
    im = v[..., 1, :].reshape(lead + (N_GROUPS_B, SSM_STATE))
    return re, im


def _s5_tables(lam_re, lam_im, log_dt, b_re, b_im, c_re, c_im):
    dt = jnp.exp(log_dt.astype(F32))[:, None]
    lr, li = lam_re.astype(F32), lam_im.astype(F32)

    def apow(k):
        mag = jnp.exp(lr * dt * k)
        return mag * jnp.cos(li * dt * k), mag * jnp.sin(li * dt * k)

    ar, ai = apow(1.0)
    den = lr * lr + li * li
    nr, ni = ar - 1.0, ai
    kr, ki = (nr * lr + ni * li) / den, (ni * lr - nr * li) / den
    br, bi = b_re.astype(F32), b_im.astype(F32)
    bbr = kr[..., None] * br - ki[..., None] * bi
    bbi = kr[..., None] * bi + ki[..., None] * br
    eye = jnp.eye(GROUPS_PER_CHUNK, dtype=F32)

    def bproj(bb):
        bb = bb.reshape(STATE_CHUNKS, GROUPS_PER_CHUNK, SSM_STATE, SSM_GROUP)
        w = jnp.einsum("mgph,gk->mghkp", bb, eye)
        return w.reshape(STATE_CHUNKS, GROUPS_PER_CHUNK * SSM_GROUP, HALF)

    wb = jnp.concatenate([bproj(bbr), bproj(bbi)], axis=-1).astype(BF16)

    def cproj(cc):
        cc = cc.reshape(STATE_CHUNKS, GROUPS_PER_CHUNK, SSM_GROUP, SSM_STATE)
        w = jnp.einsum("mghp,gk->mgpkh", cc, eye)
        return w.reshape(STATE_CHUNKS, HALF, GROUPS_PER_CHUNK * SSM_GROUP)

    wc = jnp.concatenate([cproj(c_re.astype(F32)), cproj(-c_im.astype(F32))], axis=1).astype(BF16)

    rows = jnp.arange(SUB, dtype=F32)
    tabs = []
    for i in range(SUB):
        pr, pi = apow(float(i + 1))
        tabs.append(jnp.broadcast_to(_lay(pr, pi)[None], (SUB, STATE_COLS)))
    for d in (1, 2, 4):
        pr, pi = apow(float(SUB * d))
        keep = (rows >= d).astype(F32)[:, None]
        tabs.append(_lay(pr, pi)[None] * keep)
    pr, pi = apow(SUB * rows[:, None, None])
    tabs.append(_lay(pr, pi))
    return wb, wc, jnp.stack(tabs)


def _cmul(ar, ai, br, bi):
    return ar * br - ai * bi, ar * bi + ai * br


def _s5_scan_block(bu_ref, row0, h_ref, tab_ref):
    row_id = lax.broadcasted_iota(I32, (SUB, SCAN_W), 0)
    tiles = SCAN_W // LANES
    for m in range(STATE_CHUNKS):
        for hf in range(HALF // SCAN_W):
            c_re0 = m * 2 * HALF + hf * SCAN_W
            c_im0 = c_re0 + HALF
            cre = pl.ds(c_re0, SCAN_W)
            cim = pl.ds(c_im0, SCAN_W)

            def tab(slot):
                return tab_ref[slot, :, cre], tab_ref[slot, :, cim]

            def load(i, c0):
                rows = pl.ds(row0 + i, SUB, stride=SUB)
                return jnp.concatenate([bu_ref[c0 // LANES + j, rows, :] for j in range(tiles)], axis=1)

            def store(i, c0, val):
                rows = pl.ds(row0 + i, SUB, stride=SUB)
                for j in range(tiles):
                    bu_ref[c0 // LANES + j, rows, :] = val[:, j * LANES:(j + 1) * LANES]

            a_re, a_im = tab(TAB_PW)
            s_re = load(0, c_re0)
            s_im = load(0, c_im0)
            loc = [(s_re, s_im)]
            for i in range(1, SUB):
                p_re, p_im = _cmul(a_re, a_im, s_re, s_im)
                s_re = p_re + load(i, c_re0)
                s_im = p_im + load(i, c_im0)
                loc.append((s_re, s_im))
            e_re, e_im = s_re, s_im
            for n, d in enumerate((1, 2, 4)):
                m_re, m_im = tab(TAB_MD + n)
                q_re, q_im = _cmul(m_re, m_im, pltpu.roll(e_re, d, 0), pltpu.roll(e_im, d, 0))
                e_re, e_im = e_re + q_re, e_im + q_im
            p8_re, p8_im = tab(TAB_P8)
            c_re, c_im = _cmul(p8_re, p8_im, h_ref[:, cre], h_ref[:, cim])
            c_re = c_re + jnp.where(row_id >= 1, pltpu.roll(e_re, 1, 0), 0.0)
            c_im = c_im + jnp.where(row_id >= 1, pltpu.roll(e_im, 1, 0), 0.0)
            for i in range(SUB):
                w_re, w_im = tab(TAB_PW + i)
                q_re, q_im = _cmul(w_re, w_im, c_re, c_im)
                f_re, f_im = loc[i][0] + q_re, loc[i][1] + q_im
                store(i, c_re0, f_re)
                store(i, c_im0, f_im)
            h_ref[:, cre] = jnp.broadcast_to(f_re[SUB - 1:SUB, :], (SUB, SCAN_W))
            h_ref[:, cim] = jnp.broadcast_to(f_im[SUB - 1:SUB, :], (SUB, SCAN_W))


def _mix_tile(seqs, ln, chain, new_seq, x_ref, mod_ref, h0_ref, ws_ref, bsf_ref, shared, outs,
              hfin_ref, v_ref, scratch):
    (n1g_ref, win_ref, gv_ref, wb_ref, wc_ref, tab_ref, dsk_ref, wglu_ref, bglu_ref, oga_ref,
     ogb_ref, wout_ref, n2g_ref, wr_ref, rb_ref, wsg_ref, wsu_ref, wsd_ref, ltri_ref) = shared
    xs1_ref, h2t_ref, slab_ref, wslab_ref, cnt_ref = outs
    bu_ref, h_scr, cnt_scr = scratch
    t_rows = MIX_ROWS
    rows_per_seq = t_rows // seqs

    x = x_ref[...].reshape(t_rows, D_MODEL)

    def modrow(j):
        parts = [jnp.broadcast_to(mod_ref[s, j:j + 1, :], (rows_per_seq, D_MODEL)) for s in range(seqs)]
        return parts[0] if seqs == 1 else jnp.concatenate(parts, axis=0)

    h = (x * _rms(x)) * n1g_ref[...] * (1.0 + modrow(1)) + modrow(0)
    z = _dot(h.astype(BF16), win_ref[...])
    u_a = z[:, :D_A]
    v_a = z[:, D_A:2 * D_A]
    u_b = z[:, 2 * D_A:]

    vh_parts = []
    for hh in range(N_HEADS_A):
        cols = slice(hh * HEAD_A, (hh + 1) * HEAD_A)
        vv = v_a[:, cols]
        vh_parts.append((vv * _rms(vv)) * gv_ref[:, cols])
    if v_ref is not None:
        v_ref[...] = jnp.concatenate(vh_parts, axis=1).reshape(seqs, rows_per_seq, D_A)
    n_sgu = t_rows // ln
    s_parts = []
    for hh in range(N_HEADS_A):
        vb = vh_parts[hh].astype(BF16)
        w_h = ws_ref[hh]
        s_parts.append(jnp.concatenate(
            [_dot(w_h, vb[c * ln:(c + 1) * ln, :]) for c in range(n_sgu)], axis=0))
    bsf = bsf_ref[...]
    s_mix = jnp.concatenate(s_parts, axis=1) + jnp.concatenate([bsf] * n_sgu, axis=0)
    y_a = u_a * s_mix

    ub16 = u_b.astype(BF16)
    gc = GROUPS_PER_CHUNK * SSM_GROUP
    tiles_per_chunk = 2 * HALF // LANES
    for m in range(STATE_CHUNKS):
        bu = _dot(ub16[:, m * gc:(m + 1) * gc], wb_ref[m])
        for j in range(tiles_per_chunk):
            bu_ref[m * tiles_per_chunk + j] = bu[:, j * LANES:(j + 1) * LANES]
    n_blk = t_rows // CHUNK
    if chain:
        @pl.when(new_seq)
        def _():
            h_scr[...] = jnp.zeros_like(h_scr)
    for blk in range(n_blk):
        if not chain:
            h_scr[...] = h0_ref[blk]
        _s5_scan_block(bu_ref, blk * CHUNK, h_scr, tab_ref)
        if not chain:
            hfin_ref[blk] = h_scr[...]
    if chain:
        hfin_ref[0] = h_scr[...]
    y_parts = []
    for m in range(STATE_CHUNKS):
        st = jnp.concatenate([bu_ref[m * tiles_per_chunk + j] for j in range(tiles_per_chunk)], axis=1)
        y_parts.append(_dot(st.astype(BF16), wc_ref[m]))
    y_s = jnp.concatenate(y_parts, axis=1) + dsk_ref[...] * u_b
    g_b = jax.nn.gelu(y_s)
    y_b = g_b * jax.nn.sigmoid(_dot(g_b.astype(BF16), wglu_ref[...]) + bglu_ref[...])

    na = (y_a * _rms(y_a)) * oga_ref[...]
    nb = (y_b * _rms(y_b)) * ogb_ref[...]
    mix = _dot(jnp.concatenate([na, nb], axis=1).astype(BF16), wout_ref[...])
    x1 = x + modrow(2) * mix

    h2 = (x1 * _rms(x1)) * n2g_ref[...] * (1.0 + modrow(4)) + modrow(3)
    for j in range(ROW_TILES):
        h2t_ref[_tile_rows(j, t_rows), :] = h2[:, j * LANES:(j + 1) * LANES]
    h2b = h2.astype(BF16)
    act = jax.nn.silu(_dot(h2b, wsg_ref[...])) * _dot(h2b, wsu_ref[...])
    shared_out = _dot(act.astype(BF16), wsd_ref[...])
    xs1_ref[...] = x1 + modrow(5) * shared_out

    scores = jax.nn.sigmoid(_dot(h2b, wr_ref[...]))
    lane = lax.broadcasted_iota(I32, (t_rows, N_EXPERTS), 1).astype(F32)
    work = scores + rb_ref[...]
    onehot = jnp.zeros((t_rows, N_EXPERTS), F32)
    idxs, sels = [], []
    for _ in range(TOP_K):
        top = jnp.max(work, axis=-1, keepdims=True)
        idx = jnp.min(jnp.where(work == top, lane, float(N_EXPERTS)), axis=-1, keepdims=True)
        pick = lane == idx
        sels.append(jnp.sum(jnp.where(pick, scores, 0.0), axis=-1, keepdims=True))
        idxs.append(idx)
        work = jnp.where(pick, -jnp.inf, work)
        onehot = jnp.where(pick, 1.0, onehot)
    total = sels[0]
    for k in range(1, TOP_K):
        total = total + sels[k]
    ranktab = _dot(ltri_ref[...], onehot.astype(BF16)) + cnt_scr[0:1, :]
    slab_lane = lax.broadcasted_iota(I32, (t_rows, SLAB), 1)
    slab = jnp.zeros((t_rows, SLAB), I32)
    wslab = jnp.zeros((t_rows, SLAB), F32)
    for k in range(TOP_K):
        rank = jnp.sum(jnp.where(lane == idxs[k], ranktab, 0.0), axis=-1, keepdims=True)
        slab = jnp.where(slab_lane == k, idxs[k].astype(I32), slab)
        slab = jnp.where(slab_lane == TOP_K + k, rank.astype(I32), slab)
        wslab = jnp.where(slab_lane == k, ROUTE_SCALE * sels[k] / total, wslab)
    slab_ref[...] = slab
    wslab_ref[...] = wslab
    cnt_scr[...] = cnt_scr[...] + jnp.sum(onehot, axis=0, keepdims=True)
    cnt_ref[...] = cnt_scr[...]


def _mix_body(n_prompt_tiles, tiles_per_seq, seqs_s, ln_s,
              xp_ref, xsm_ref, modp_ref, mods_ref, h0s_ref, wsp_ref, bsfp_ref, wss_ref, bsfs_ref,
              *rest):
    shared = rest[:19]
    xs1_ref, h2t_ref, slab_ref, wslab_ref, cnt_ref, hfp_ref, hfs_ref, v_ref = rest[19:27]
    scratch = rest[27:]
    outs = (xs1_ref, h2t_ref, slab_ref, wslab_ref, cnt_ref)
    s = pl.program_id(0)

    @pl.when(s == 0)
    def _():
        scratch[2][...] = jnp.zeros_like(scratch[2])

    @pl.when(s < n_prompt_tiles)
    def _():
        _mix_tile(1, SGU_LEN, True, lax.rem(s, tiles_per_seq) == 0, xp_ref, modp_ref, None,
                  wsp_ref, bsfp_ref, shared, outs, hfp_ref, None, scratch)

    @pl.when(s >= n_prompt_tiles)
    def _():
        _mix_tile(seqs_s, ln_s, False, None, xsm_ref, mods_ref, h0s_ref,
                  wss_ref, bsfs_ref, shared, outs, hfs_ref, v_ref, scratch)


def _mix(x_prompt, x_sample, mod, h0_s, ws_p, bsf_p, ws_s, bsf_s, shared):
    bp, sp, _ = x_prompt.shape
    bs, ss, _ = x_sample.shape
    assert sp % MIX_ROWS == 0 and MIX_ROWS % ss == 0 and ss == CHUNK
    seqs_s = MIX_ROWS // ss
    assert bs % seqs_s == 0 and bp % seqs_s == 0
    tiles_per_seq = sp // MIX_ROWS
    n_pt = bp * tiles_per_seq
    n_st = bs // seqs_s
    n_tok = bp * sp + bs * ss

    def p_tile(s):
        return jnp.minimum(s, n_pt - 1)

    def s_tile(s):
        return jnp.maximum(s - n_pt, 0)

    def const(shape):
        nd = len(shape)
        return pl.BlockSpec(shape, lambda s: (0,) * nd)

    in_specs = [
        pl.BlockSpec((1, MIX_ROWS, D_MODEL), lambda s: (p_tile(s) // tiles_per_seq, p_tile(s) % tiles_per_seq, 0)),
        pl.BlockSpec((seqs_s, ss, D_MODEL), lambda s: (s_tile(s), 0, 0)),
        pl.BlockSpec((1, 6, D_MODEL), lambda s: (p_tile(s) // tiles_per_seq, 0, 0)),
        pl.BlockSpec((seqs_s, 6, D_MODEL), lambda s: (bp // seqs_s + s_tile(s), 0, 0)),
        pl.BlockSpec((seqs_s, SUB, STATE_COLS), lambda s: (s_tile(s), 0, 0)),
        const(ws_p.shape), const(bsf_p.shape), const(ws_s.shape), const(bsf_s.shape),
    ] + [const(w.shape) for w in shared]
    out_shape = [
        jax.ShapeDtypeStruct((n_tok, D_MODEL), F32),
        jax.ShapeDtypeStruct((n_tok * ROW_TILES, LANES), F32),
        jax.ShapeDtypeStruct((n_tok, SLAB), I32),
        jax.ShapeDtypeStruct((n_tok, SLAB), F32),
        jax.ShapeDtypeStruct((SUB, N_EXPERTS), F32),
        jax.ShapeDtypeStruct((bp, SUB, STATE_COLS), F32),
        jax.ShapeDtypeStruct((bs, SUB, STATE_COLS), F32),
        jax.ShapeDtypeStruct((bs, ss, D_A), F32),
    ]
    out_specs = [
        pl.BlockSpec((MIX_ROWS, D_MODEL), lambda s: (s, 0)),
        pl.BlockSpec((MIX_ROWS * ROW_TILES, LANES), lambda s: (s, 0)),
        pl.BlockSpec((MIX_ROWS, SLAB), lambda s: (s, 0)),
        pl.BlockSpec((MIX_ROWS, SLAB), lambda s: (s, 0)),
        const((SUB, N_EXPERTS)),
        pl.BlockSpec((1, SUB, STATE_COLS), lambda s: (p_tile(s) // tiles_per_seq, 0, 0)),
        pl.BlockSpec((seqs_s, SUB, STATE_COLS), lambda s: (s_tile(s), 0, 0)),
        pl.BlockSpec((seqs_s, ss, D_A), lambda s: (s_tile(s), 0, 0)),
    ]
    return pl.pallas_call(
        functools.partial(_mix_body, n_pt, tiles_per_seq, seqs_s, ss),
        out_shape=out_shape,
        grid=(n_pt + n_st,),
        in_specs=in_specs,
        out_specs=out_specs,
        scratch_shapes=[pltpu.VMEM((STATE_COLS // LANES, MIX_ROWS, LANES), F32),
                        pltpu.VMEM((SUB, STATE_COLS), F32),
                        pltpu.VMEM((SUB, N_EXPERTS), F32)],
        compiler_params=pltpu.CompilerParams(dimension_semantics=("arbitrary",),
                                             vmem_limit_bytes=VMEM_LIMIT),
        name="mix",
    )(x_prompt, x_sample, mod, mod, h0_s, ws_p, bsf_p, ws_s, bsf_s, *shared)


SLOT_TILE = 1024


DIGIT = 256


def _slots_body(slab_ref, digits_ref, dst_ref):
    slab = slab_ref[...]
    digits = digits_ref[...]
    lane = lax.broadcasted_iota(I32, (SLOT_TILE, N_EXPERTS), 1)
    out_lane = lax.broadcasted_iota(I32, (SLOT_TILE, SLAB), 1)
    start = jnp.zeros((SLOT_TILE, SLAB), F32)
    for k in range(TOP_K):
        onehot = jnp.where(lane == slab[:, k:k + 1], 1.0, 0.0).astype(BF16)
        d = _dot(onehot, digits)
        value = d[:, :SLAB] + float(DIGIT) * d[:, SLAB:2 * SLAB] + float(DIGIT * DIGIT) * d[:, 2 * SLAB:]
        start = jnp.where(out_lane == k, value, start)
    ranks = pltpu.roll(slab, SLAB - TOP_K, 1)
    dst_ref[...] = jnp.where(out_lane < TOP_K, start.astype(I32) + ranks, 0)


def _slots(slab, pstart):
    n_tok = slab.shape[0]
    assert n_tok % SLOT_TILE == 0
    parts = [pstart % DIGIT, (pstart // DIGIT) % DIGIT, pstart // (DIGIT * DIGIT)]
    digits = jnp.concatenate([jnp.broadcast_to(p[:, None], (N_EXPERTS, SLAB)) for p in parts], axis=1)
    return pl.pallas_call(
        _slots_body,
        out_shape=jax.ShapeDtypeStruct((n_tok, SLAB), I32),
        grid=(n_tok // SLOT_TILE,),
        in_specs=[pl.BlockSpec((SLOT_TILE, SLAB), lambda i: (i, 0)),
                  pl.BlockSpec((N_EXPERTS, 3 * SLAB), lambda i: (0, 0))],
        out_specs=pl.BlockSpec((SLOT_TILE, SLAB), lambda i: (i, 0)),
        compiler_params=pltpu.CompilerParams(dimension_semantics=("arbitrary",)),
        name="moe_slots",
    )(slab, digits.astype(BF16))


def _token_rows(t):
    return pl.ds(pl.multiple_of(t * ROW_TILES, ROW_TILES), ROW_TILES)


def _dispatch_body(dst_ref, pstart_ref, pend_ref, h2t_hbm, xs_hbm, tile_ref, zero_ref, in_sems, row_sems,
                   zsem):
    i = pl.program_id(0)
    n_tiles = pl.num_programs(0)
    blk_rows = MOE_BLK * ROW_TILES
    tile_rows = TOK_TILE * ROW_TILES
    n_blocks = xs_hbm.shape[0] // blk_rows

    def fetch(tile):
        slot = lax.rem(tile, DISPATCH_BUFS)
        src = h2t_hbm.at[pl.ds(pl.multiple_of(tile * tile_rows, tile_rows), tile_rows)]
        return pltpu.make_async_copy(src, tile_ref.at[slot], in_sems.at[slot])

    def drain(tile):
        for _ in range(TOP_K):
            pltpu.make_async_copy(tile_ref.at[0], xs_hbm.at[pl.ds(0, tile_rows)],
                                  row_sems.at[lax.rem(tile, 2)]).wait()

    @pl.when(i == 0)
    def _():
        fetch(0).start()

    @pl.when(i + 1 < n_tiles)
    def _():
        fetch(i + 1).start()

    @pl.when(i == 0)
    def _():
        zero_ref[...] = jnp.zeros_like(zero_ref)

        def clear(blk):
            dst = xs_hbm.at[pl.ds(pl.multiple_of(blk * blk_rows, blk_rows), blk_rows)]
            return pltpu.make_async_copy(zero_ref, dst, zsem)

        def each(fn):
            def fill(e, carry):
                @pl.when(pend_ref[e] > pstart_ref[e])
                def _():
                    fn(clear(pend_ref[e] // MOE_BLK - 1))
                return carry
            lax.fori_loop(0, N_EXPERTS, fill, 0)

            def tail(b, carry):
                fn(clear(b))
                return carry
            lax.fori_loop(pend_ref[N_EXPERTS - 1] // MOE_BLK, n_blocks, tail, 0)

        each(lambda cp: cp.start())
        each(lambda cp: cp.wait())

    fetch(i).wait()
    src_tile = tile_ref.at[lax.rem(i, DISPATCH_BUFS)]
    row_sem = row_sems.at[lax.rem(i, 2)]

    def issue(t, carry):
        for k in range(TOP_K):
            dst = dst_ref[0, 0, t * TOP_K + k]
            pltpu.make_async_copy(src_tile.at[_token_rows(t)], xs_hbm.at[_token_rows(dst)],
                                  row_sem).start(priority=k % 2)
        return carry
    lax.fori_loop(0, TOK_TILE, issue, 0)

    @pl.when(i > 0)
    def _():
        drain(i - 1)

    @pl.when(i == n_tiles - 1)
    def _():
        drain(i)


def _dispatch(dst, pstart, pend, h2t, n_rows):
    n_tiles = h2t.shape[0] // (TOK_TILE * ROW_TILES)
    smem_tile = pl.BlockSpec((1, 1, TOK_TILE * TOP_K), lambda i: (i, 0, 0), memory_space=pltpu.SMEM)
    smem_all = pl.BlockSpec(memory_space=pltpu.SMEM)
    return pl.pallas_call(
        _dispatch_body,
        out_shape=jax.ShapeDtypeStruct((n_rows * ROW_TILES, LANES), F32),
        grid=(n_tiles,),
        in_specs=[smem_tile, smem_all, smem_all, pl.BlockSpec(memory_space=pl.ANY)],
        out_specs=pl.BlockSpec(memory_space=pl.ANY),
        scratch_shapes=[pltpu.VMEM((DISPATCH_BUFS, TOK_TILE * ROW_TILES, LANES), F32),
                        pltpu.VMEM((MOE_BLK * ROW_TILES, LANES), F32),
                        pltpu.SemaphoreType.DMA((DISPATCH_BUFS,)), pltpu.SemaphoreType.DMA((2,)),
                        pltpu.SemaphoreType.DMA],
        compiler_params=pltpu.CompilerParams(dimension_semantics=("arbitrary",)),
        name="moe_dispatch",
    )(dst.reshape(n_tiles, 1, TOK_TILE * TOP_K), pstart, pend, h2t)


def _experts_body(pstart_ref, pend_ref, xs_hbm, wg_ref, wu_ref, wd_ref, ys_hbm,
                  xbuf, obuf, wgu_scr, wd_scr, in_sems, out_sems):
    e = pl.program_id(0)
    blk_rows = MOE_BLK * ROW_TILES
    n_blocks = xs_hbm.shape[0] // blk_rows
    n_used = pend_ref[N_EXPERTS - 1] // MOE_BLK
    first_blk = pstart_ref[e] // MOE_BLK
    last_blk = pend_ref[e] // MOE_BLK

    def block(ref, blk):
        return ref.at[pl.ds(pl.multiple_of(blk * blk_rows, blk_rows), blk_rows)]

    def fetch(blk):
        slot = lax.rem(blk, IN_BUFS)
        return pltpu.make_async_copy(block(xs_hbm, blk), xbuf.at[slot], in_sems.at[slot])

    def put(blk):
        slot = lax.rem(blk, OUT_BUFS)
        return pltpu.make_async_copy(obuf.at[slot], block(ys_hbm, blk), out_sems.at[slot])

    def free_out(blk):
        @pl.when(blk >= OUT_BUFS)
        def _():
            put(blk - OUT_BUFS).wait()

    @pl.when(e == 0)
    def _():
        for first in range(IN_BUFS - 1):
            @pl.when(first < n_used)
            def _():
                fetch(first).start()

    @pl.when(last_blk > first_blk)
    def _():
        wgu_scr[:, :D_EXPERT] = wg_ref[0].astype(BF16)
        wgu_scr[:, D_EXPERT:] = wu_ref[0].astype(BF16)
        wd_scr[...] = wd_ref[0].astype(BF16)

    def one_block(b, carry):
        free_out(b)
        ahead = b + IN_BUFS - 1

        @pl.when(ahead < n_used)
        def _():
            fetch(ahead).start()
        fetch(b).wait()
        rows_in = xbuf.at[lax.rem(b, IN_BUFS)]
        out = obuf.at[lax.rem(b, OUT_BUFS)]
        xb = jnp.concatenate([rows_in[_tile_rows(j, MOE_BLK), :] for j in range(ROW_TILES)],
                             axis=1).astype(BF16)
        gu = _dot(xb, wgu_scr[...])
        act = (jax.nn.silu(gu[:, :D_EXPERT]) * gu[:, D_EXPERT:]).astype(BF16)
        y = _dot(act, wd_scr[...])
        for j in range(ROW_TILES):
            out[_tile_rows(j, MOE_BLK), :] = y[:, j * LANES:(j + 1) * LANES]
        put(b).start()
        return carry
    lax.fori_loop(first_blk, last_blk, one_block, 0)

    @pl.when(e == N_EXPERTS - 1)
    def _():
        def tail(b, carry):
            free_out(b)
            obuf[lax.rem(b, OUT_BUFS)] = jnp.zeros((blk_rows, LANES), F32)
            put(b).start()
            return carry
        lax.fori_loop(n_used, n_blocks, tail, 0)
        for back in range(OUT_BUFS):
            put(n_blocks - 1 - back).wait()


def _experts(pstart, pend, xs, w_gate, w_up, w_down):
    blk_rows = MOE_BLK * ROW_TILES
    n_blocks = xs.shape[0] // blk_rows
    assert n_blocks >= OUT_BUFS

    def w_map(e, pstart_ref, pend_ref):
        return (e, 0, 0)

    return pl.pallas_call(
        _experts_body,
        out_shape=jax.ShapeDtypeStruct(xs.shape, F32),
        grid_spec=pltpu.PrefetchScalarGridSpec(
            num_scalar_prefetch=2,
            grid=(N_EXPERTS,),
            in_specs=[pl.BlockSpec(memory_space=pl.ANY),
                      pl.BlockSpec((1, D_MODEL, D_EXPERT), w_map),
                      pl.BlockSpec((1, D_MODEL, D_EXPERT), w_map),
                      pl.BlockSpec((1, D_EXPERT, D_MODEL), w_map)],
            out_specs=pl.BlockSpec(memory_space=pl.ANY),
            scratch_shapes=[pltpu.VMEM((IN_BUFS, blk_rows, LANES), F32),
                            pltpu.VMEM((OUT_BUFS, blk_rows, LANES), F32),
                            pltpu.VMEM((D_MODEL, 2 * D_EXPERT), BF16),
                            pltpu.VMEM((D_EXPERT, D_MODEL), BF16),
                            pltpu.SemaphoreType.DMA((IN_BUFS,)),
                            pltpu.SemaphoreType.DMA((OUT_BUFS,))]),
        compiler_params=pltpu.CompilerParams(dimension_semantics=("arbitrary",),
                                             vmem_limit_bytes=VMEM_LIMIT),
        name="moe_experts",
    )(pstart, pend, xs, w_gate, w_up, w_down)


def _combine_body(n_prompt_tiles, n_tiles, dst_ref, dnext_ref, xs1_ref, g2_ref, wts_ref, fg_ref,
                  ys_hbm, yp_ref, ysm_ref, buf_ref, wb_ref, sems):
    i = pl.program_id(0)

    def issue(d_ref, slot):
        def body(t, carry):
            for k in range(TOP_K):
                src = d_ref[0, 0, t * TOP_K + k]
                pltpu.make_async_copy(ys_hbm.at[_token_rows(src)], buf_ref.at[slot, k, _token_rows(t)],
                                      sems.at[slot]).start(priority=k % 2)
            return carry
        lax.fori_loop(0, TOK_TILE, body, 0)

    def drain(slot):
        for k in range(TOP_K):
            pltpu.make_async_copy(ys_hbm.at[pl.ds(0, TOK_TILE * ROW_TILES)], buf_ref.at[slot, k],
                                  sems.at[slot]).wait()

    @pl.when(i == 0)
    def _():
        issue(dst_ref, 0)

    def step(slot):
        @pl.when(i + 1 < n_tiles)
        def _():
            issue(dnext_ref, 1 - slot)
        wts = wts_ref[...]
        for k in range(TOP_K):
            wb_ref[k] = jnp.broadcast_to(wts[:, k:k + 1], (TOK_TILE, LANES))
        drain(slot)
        parts = []
        for j in range(ROW_TILES):
            acc = wb_ref[0] * buf_ref[slot, 0, _tile_rows(j, TOK_TILE), :]
            for k in range(1, TOP_K):
                acc = acc + wb_ref[k] * buf_ref[slot, k, _tile_rows(j, TOK_TILE), :]
            parts.append(acc)
        seg = TOK_TILE // g2_ref.shape[0]
        g2 = jnp.concatenate(
            [jnp.broadcast_to(g2_ref[s], (seg, D_MODEL)) for s in range(g2_ref.shape[0])], axis=0)
        x = xs1_ref[...] + g2 * jnp.concatenate(parts, axis=1)
        y = (x * _rms(x)) * fg_ref[...]

        @pl.when(i < n_prompt_tiles)
        def _():
            yp_ref[...] = y

        @pl.when(i >= n_prompt_tiles)
        def _():
            ysm_ref[...] = y

    for slot in range(2):
        pl.when(lax.rem(i, 2) == slot)(functools.partial(step, slot))


def _combine(dst, xs1, g2_blocks, wts, final_g, ys, n_p):
    n_tok = xs1.shape[0]
    n_tiles = n_tok // TOK_TILE
    n_pt = n_p // TOK_TILE
    segs = TOK_TILE // CHUNK
    smem_tile = pl.BlockSpec((1, 1, TOK_TILE * TOP_K), lambda i: (i, 0, 0), memory_space=pltpu.SMEM)
    smem_next = pl.BlockSpec((1, 1, TOK_TILE * TOP_K), lambda i: (jnp.minimum(i + 1, n_tiles - 1), 0, 0),
                             memory_space=pltpu.SMEM)
    dst = dst.reshape(n_tiles, 1, TOK_TILE * TOP_K)
    return pl.pallas_call(
        functools.partial(_combine_body, n_pt, n_tiles),
        out_shape=[jax.ShapeDtypeStruct((n_p, D_MODEL), F32),
                   jax.ShapeDtypeStruct((n_tok - n_p, D_MODEL), F32)],
        grid=(n_tiles,),
        in_specs=[smem_tile, smem_next,
                  pl.BlockSpec((TOK_TILE, D_MODEL), lambda i: (i, 0)),
                  pl.BlockSpec((segs, 1, D_MODEL), lambda i: (i, 0, 0)),
                  pl.BlockSpec((TOK_TILE, TOP_K), lambda i: (i, 0)),
                  pl.BlockSpec((1, D_MODEL), lambda i: (0, 0)),
                  pl.BlockSpec(memory_space=pl.ANY)],
        out_specs=[pl.BlockSpec((TOK_TILE, D_MODEL), lambda i: (jnp.minimum(i, n_pt - 1), 0)),
                   pl.BlockSpec((TOK_TILE, D_MODEL), lambda i: (jnp.maximum(i - n_pt, 0), 0))],
        scratch_shapes=[pltpu.VMEM((2, TOP_K, TOK_TILE * ROW_TILES, LANES), F32),
                        pltpu.VMEM((TOP_K, TOK_TILE, LANES), F32),
                        pltpu.SemaphoreType.DMA((2,))],
        compiler_params=pltpu.CompilerParams(dimension_semantics=("arbitrary",),
                                             vmem_limit_bytes=VMEM_LIMIT),
        name="moe_combine",
    )(dst, dst, xs1, g2_blocks, wts, final_g.reshape(1, D_MODEL), ys)


def kernel(x_prompt, x_sample, c_prompt, c_sample, state_ssm_re, state_ssm_im, norm1_g, norm2_g,
           w_ada, b_ada, w_in, w_s, b_s, g_v, lam_re, lam_im, log_dt, b_re, b_im, c_re, c_im,
           d_skip, w_glu, b_glu, out_g_a, out_g_b, w_out, w_router, router_bias, w_gate, w_up,
           w_down, ws_gate, ws_up, ws_down, final_g):
    assert norm1_g.shape[0] == 1
    bp, sp, _ = x_prompt.shape
    bs, ss, _ = x_sample.shape
    n_p, n_s = bp * sp, bs * ss
    n_tok = n_p + n_s
    l = 0

    mod = _adaln(jnp.concatenate([c_prompt, c_sample], axis=0), w_ada[l], b_ada[l])
    mod = mod.reshape(bp + bs, 6, D_MODEL)

    wb, wc, tabs = _s5_tables(lam_re[l], lam_im[l], log_dt[l], b_re[l], b_im[l], c_re[l], c_im[l])
    pos = jnp.arange(SGU_LEN)
    mask = (pos[:, None] // CHUNK) >= (pos[None, :] // CHUNK)
    ws_masked = jnp.where(mask[None], w_s[l], 0.0)
    row = lax.broadcasted_iota(I32, (MIX_ROWS, MIX_ROWS), 0)
    col = lax.broadcasted_iota(I32, (MIX_ROWS, MIX_ROWS), 1)
    ltri = (col < row).astype(BF16)

    def sgu_weights(ln):
        bsf = jnp.repeat(b_s[l][:, :ln].T, HEAD_A, axis=1)
        return ws_masked[:, :ln, :ln].astype(BF16), bsf

    ws_p, bsf_p = sgu_weights(SGU_LEN)
    ws_s, bsf_s = sgu_weights(ss)
    shared = [norm1_g[l].reshape(1, D_MODEL), w_in[l].astype(BF16), g_v[l].reshape(1, D_A),
              wb, wc, tabs, d_skip[l].reshape(1, D_B), w_glu[l].astype(BF16),
              b_glu[l].reshape(1, D_B), out_g_a[l].reshape(1, D_A), out_g_b[l].reshape(1, D_B),
              w_out[l].astype(BF16), norm2_g[l].reshape(1, D_MODEL), w_router[l].astype(BF16),
              router_bias[l].reshape(1, N_EXPERTS), ws_gate[l].astype(BF16),
              ws_up[l].astype(BF16), ws_down[l].astype(BF16), ltri]
    h0_s = jnp.broadcast_to(_lay(state_ssm_re[l], state_ssm_im[l])[:, None, :], (bs, SUB, STATE_COLS))
    xs1, h2t, slab, wslab, cnt_all, hfin_p, hfin_s, v_rows = _mix(
        x_prompt, x_sample, mod, h0_s, ws_p, bsf_p, ws_s, bsf_s, shared)

    wts = wslab[:, :TOP_K]
    counts = cnt_all[0].astype(I32)
    padded = (counts + MOE_BLK - 1) // MOE_BLK * MOE_BLK
    pend = jnp.cumsum(padded).astype(I32)
    pstart = pend - padded
    n_blocks = -(-n_tok * TOP_K // MOE_BLK) + N_EXPERTS

    dst = _slots(slab, pstart)[:, :TOP_K]
    xs = _dispatch(dst, pstart, pend, h2t, n_blocks * MOE_BLK)
    ys = _experts(pstart, pend, xs, w_gate[l], w_up[l], w_down[l])

    g2 = mod[:, 5, :]
    g2_blocks = jnp.concatenate([jnp.repeat(g2[:bp], sp // CHUNK, axis=0),
                                 jnp.repeat(g2[bp:], ss // CHUNK, axis=0)], axis=0)
    y_p, y_s = _combine(dst, xs1, g2_blocks.reshape(n_tok // CHUNK, 1, D_MODEL), wts, final_g, ys, n_p)

    re_p, im_p = _unlay(hfin_p[:, 0, :])
    re_s, im_s = _unlay(hfin_s[:, 0, :])
    return (y_p.reshape(bp, sp, D_MODEL), y_s.reshape(bs, ss, D_MODEL),
            re_p[None], im_p[None], re_s[None], im_s[None], v_rows[None])
```

```python
import functools

import jax
import jax.numpy as jnp
from jax import lax
from jax.experimental import pallas as pl
from jax.experimental.pallas import tpu as pltpu

F32 = jnp.float32
BF16 = jnp.bfloat16
I32 = jnp.int32

D_MODEL = 1024
D_A = 512
D_B = 512
N_HEADS_A = 4
HEAD_A = 128
SSM_GROUP = 16
N_GROUPS_B = 32
SSM_STATE = 64
N_EXPERTS = 256
TOP_K = 8
D_EXPERT = 256
ROUTE_SCALE = 2.5
CHUNK = 64
SGU_LEN = 128
EPS = 1e-6

STATE_CHUNKS = 4
GROUPS_PER_CHUNK = N_GROUPS_B // STATE_CHUNKS
HALF = GROUPS_PER_CHUNK * SSM_STATE
STATE_COLS = STATE_CHUNKS * 2 * HALF
SCAN_W = 256
SUB = 8
LANES = 128
ROW_TILES = D_MODEL // LANES

TAB_PW = 0
TAB_MD = 8
TAB_P8 = 11

MIX_ROWS = 256
MOE_BLK = 256
DISPATCH_BUFS = 3
IN_BUFS = 8
OUT_BUFS = 4
TOK_TILE = 256
SLAB = 128
VMEM_LIMIT = 56 * 1024 * 1024


def _dot(a, b):
    return jnp.dot(a, b, preferred_element_type=F32)


def _rms(x):
    return lax.rsqrt(jnp.mean(x * x, axis=-1, keepdims=True) + EPS)


def _tile_rows(j, n):
    return pl.ds(j, n, stride=ROW_TILES)


def _adaln_body(c_ref, w_ref, b_ref, o_ref):
    c = c_ref[...]
    o_ref[...] = _dot(jax.nn.silu(c).astype(BF16), w_ref[...].astype(BF16)) + b_ref[...]


def _adaln(c, w_ada, b_ada):
    n = c.shape[0]
    cols = w_ada.shape[1]
    blk = 1536
    return pl.pallas_call(
        _adaln_body,
        out_shape=jax.ShapeDtypeStruct((n, cols), F32),
        grid=(cols // blk,),
        in_specs=[pl.BlockSpec((n, D_MODEL), lambda j: (0, 0)),
                  pl.BlockSpec((D_MODEL, blk), lambda j: (0, j)),
                  pl.BlockSpec((1, blk), lambda j: (0, j))],
        out_specs=pl.BlockSpec((n, blk), lambda j: (0, j)),
        compiler_params=pltpu.CompilerParams(dimension_semantics=("arbitrary",)),
        name="adaln",
    )(c, w_ada, b_ada.reshape(1, cols))


def _lay(re, im):
    lead = re.shape[:-2]
    re = re.reshape(lead + (STATE_CHUNKS, HALF))
    im = im.reshape(lead + (STATE_CHUNKS, HALF))
    return jnp.concatenate([re, im], axis=-1).reshape(lead + (STATE_COLS,))


def _unlay(v):
    lead = v.shape[:-1]
    v = v.reshape(lead + (STATE_CHUNKS, 2, HALF))
    re = v[..., 0, :].reshape(lead + (N_GROUPS_B, SSM_STATE))
    im = v[..., 1, :].reshape(lead + (N_GROUPS_B, SSM_STATE))
    return re, im


def _s5_tables(lam_re, lam_im, log_dt, b_re, b_im, c_re, c_im):
    dt = jnp.exp(log_dt.astype(F32))[:, None]
    lr, li = lam_re.astype(F32), lam_im.astype(F32)

    def apow(k):
        mag = jnp.exp(lr * dt * k)
        return mag * jnp.cos(li * dt * k), mag * jnp.sin(li * dt * k)

    ar, ai = apow(1.0)
    den = lr * lr + li * li
    nr, ni = ar - 1.0, ai
    kr, ki = (nr * lr + ni * li) / den, (ni * lr - nr * li) / den
    br, bi = b_re.astype(F32), b_im.astype(F32)
    bbr = kr[..., None] * br - ki[..., None] * bi
    bbi = kr[..., None] * bi + ki[..., None] * br
    eye = jnp.eye(GROUPS_PER_CHUNK, dtype=F32)

    def bproj(bb):
        bb = bb.reshape(STATE_CHUNKS, GROUPS_PER_CHUNK, SSM_STATE, SSM_GROUP)
        w = jnp.einsum("mgph,gk->mghkp", bb, eye)
        return w.reshape(STATE_CHUNKS, GROUPS_PER_CHUNK * SSM_GROUP, HALF)

    wb = jnp.concatenate([bproj(bbr), bproj(bbi)], axis=-1).astype(BF16)

    def cproj(cc):
        cc = cc.reshape(STATE_CHUNKS, GROUPS_PER_CHUNK, SSM_GROUP, SSM_STATE)
        w = jnp.einsum("mghp,gk->mgpkh", cc, eye)
        return w.reshape(STATE_CHUNKS, HALF, GROUPS_PER_CHUNK * SSM_GROUP)

    wc = jnp.concatenate([cproj(c_re.astype(F32)), cproj(-c_im.astype(F32))], axis=1).astype(BF16)

    rows = jnp.arange(SUB, dtype=F32)
    tabs = []
    for i in range(SUB):
        pr, pi = apow(float(i + 1))
        tabs.append(jnp.broadcast_to(_lay(pr, pi)[None], (SUB, STATE_COLS)))
    for d in (1, 2, 4):
        pr, pi = apow(float(SUB * d))
        keep = (rows >= d).astype(F32)[:, None]
        tabs.append(_lay(pr, pi)[None] * keep)
    pr, pi = apow(SUB * rows[:, None, None])
    tabs.append(_lay(pr, pi))
    return wb, wc, jnp.stack(tabs)


def _cmul(ar, ai, br, bi):
    return ar * br - ai * bi, ar * bi + ai * br


def _s5_scan_block(bu_ref, row0, h_ref, tab_ref):
    row_id = lax.broadcasted_iota(I32, (SUB, SCAN_W), 0)
    tiles = SCAN_W // LANES
    for m in range(STATE_CHUNKS):
        for hf in range(HALF // SCAN_W):
            c_re0 = m * 2 * HALF + hf * SCAN_W
            c_im0 = c_re0 + HALF
            cre = pl.ds(c_re0, SCAN_W)
            cim = pl.ds(c_im0, SCAN_W)

            def tab(slot):
                return tab_ref[slot, :, cre], tab_ref[slot, :, cim]

            def load(i, c0):
                rows = pl.ds(row0 + i, SUB, stride=SUB)
                return jnp.concatenate([bu_ref[c0 // LANES + j, rows, :] for j in range(tiles)], axis=1)

            def store(i, c0, val):
                rows = pl.ds(row0 + i, SUB, stride=SUB)
                for j in range(tiles):
                    bu_ref[c0 // LANES + j, rows, :] = val[:, j * LANES:(j + 1) * LANES]

            a_re, a_im = tab(TAB_PW)
            s_re = load(0, c_re0)
            s_im = load(0, c_im0)
            loc = [(s_re, s_im)]
            for i in range(1, SUB):
                p_re, p_im = _cmul(a_re, a_im, s_re, s_im)
                s_re = p_re + load(i, c_re0)
                s_im = p_im + load(i, c_im0)
                loc.append((s_re, s_im))
            e_re, e_im = s_re, s_im
            for n, d in enumerate((1, 2, 4)):
                m_re, m_im = tab(TAB_MD + n)
                q_re, q_im = _cmul(m_re, m_im, pltpu.roll(e_re, d, 0), pltpu.roll(e_im, d, 0))
                e_re, e_im = e_re + q_re, e_im + q_im
            p8_re, p8_im = tab(TAB_P8)
            c_re, c_im = _cmul(p8_re, p8_im, h_ref[:, cre], h_ref[:, cim])
            c_re = c_re + jnp.where(row_id >= 1, pltpu.roll(e_re, 1, 0), 0.0)
            c_im = c_im + jnp.where(row_id >= 1, pltpu.roll(e_im, 1, 0), 0.0)
            for i in range(SUB):
                w_re, w_im = tab(TAB_PW + i)
                q_re, q_im = _cmul(w_re, w_im, c_re, c_im)
                f_re, f_im = loc[i][0] + q_re, loc[i][1] + q_im
                store(i, c_re0, f_re)
                store(i, c_im0, f_im)
            h_ref[:, cre] = jnp.broadcast_to(f_re[SUB - 1:SUB, :], (SUB, SCAN_W))
            h_ref[:, cim] = jnp.broadcast_to(f_im[SUB - 1:SUB, :], (SUB, SCAN_W))


def _mix_tile(seqs, ln, chain, new_seq, x_ref, mod_ref, h0_ref, ws_ref, bsf_ref, shared, outs,
              hfin_ref, v_ref, scratch):
    (n1g_ref, win_ref, gv_ref, wb_ref, wc_ref, tab_ref, dsk_ref, wglu_ref, bglu_ref, oga_ref,
     ogb_ref, wout_ref, n2g_ref, wr_ref, rb_ref, wsg_ref, wsu_ref, wsd_ref, ltri_ref) = shared
    xs1_ref, h2t_ref, slab_ref, wslab_ref, cnt_ref = outs
    bu_ref, h_scr, cnt_scr = scratch
    t_rows = MIX_ROWS
    rows_per_seq = t_rows // seqs

    x = x_ref[...].reshape(t_rows, D_MODEL)

    def modrow(j):
        parts = [jnp.broadcast_to(mod_ref[s, j:j + 1, :], (rows_per_seq, D_MODEL)) for s in range(seqs)]
        return parts[0] if seqs == 1 else jnp.concatenate(parts, axis=0)

    h = (x * _rms(x)) * n1g_ref[...] * (1.0 + modrow(1)) + modrow(0)
    z = _dot(h.astype(BF16), win_ref[...])
    u_a = z[:, :D_A]
    v_a = z[:, D_A:2 * D_A]
    u_b = z[:, 2 * D_A:]

    vh_parts = []
    for hh in range(N_HEADS_A):
        cols = slice(hh * HEAD_A, (hh + 1) * HEAD_A)
        vv = v_a[:, cols]
        vh_parts.append((vv * _rms(vv)) * gv_ref[:, cols])
    if v_ref is not None:
        v_ref[...] = jnp.concatenate(vh_parts, axis=1).reshape(seqs, rows_per_seq, D_A)
    n_sgu = t_rows // ln
    s_parts = []
    for hh in range(N_HEADS_A):
        vb = vh_parts[hh].astype(BF16)
        w_h = ws_ref[hh]
        s_parts.append(jnp.concatenate(
            [_dot(w_h, vb[c * ln:(c + 1) * ln, :]) for c in range(n_sgu)], axis=0))
    bsf = bsf_ref[...]
    s_mix = jnp.concatenate(s_parts, axis=1) + jnp.concatenate([bsf] * n_sgu, axis=0)
    y_a = u_a * s_mix

    ub16 = u_b.astype(BF16)
    gc = GROUPS_PER_CHUNK * SSM_GROUP
    tiles_per_chunk = 2 * HALF // LANES
    for m in range(STATE_CHUNKS):
        bu = _dot(ub16[:, m * gc:(m + 1) * gc], wb_ref[m])
        for j in range(tiles_per_chunk):
            bu_ref[m * tiles_per_chunk + j] = bu[:, j * LANES:(j + 1) * LANES]
    n_blk = t_rows // CHUNK
    if chain:
        @pl.when(new_seq)
        def _():
            h_scr[...] = jnp.zeros_like(h_scr)
    for blk in range(n_blk):
        if not chain:
            h_scr[...] = h0_ref[blk]
        _s5_scan_block(bu_ref, blk * CHUNK, h_scr, tab_ref)
        if not chain:
            hfin_ref[blk] = h_scr[...]
    if chain:
        hfin_ref[0] = h_scr[...]
    y_parts = []
    for m in range(STATE_CHUNKS):
        st = jnp.concatenate([bu_ref[m * tiles_per_chunk + j] for j in range(tiles_per_chunk)], axis=1)
        y_parts.append(_dot(st.astype(BF16), wc_ref[m]))
    y_s = jnp.concatenate(y_parts, axis=1) + dsk_ref[...] * u_b
    g_b = jax.nn.gelu(y_s)
    y_b = g_b * jax.nn.sigmoid(_dot(g_b.astype(BF16), wglu_ref[...]) + bglu_ref[...])

    na = (y_a * _rms(y_a)) * oga_ref[...]
    nb = (y_b * _rms(y_b)) * ogb_ref[...]
    mix = _dot(jnp.concatenate([na, nb], axis=1).astype(BF16), wout_ref[...])
    x1 = x + modrow(2) * mix

    h2 = (x1 * _rms(x1)) * n2g_ref[...] * (1.0 + modrow(4)) + modrow(3)
    for j in range(ROW_TILES):
        h2t_ref[_tile_rows(j, t_rows), :] = h2[:, j * LANES:(j + 1) * LANES]
    h2b = h2.astype(BF16)
    act = jax.nn.silu(_dot(h2b, wsg_ref[...])) * _dot(h2b, wsu_ref[...])
    shared_out = _dot(act.astype(BF16), wsd_ref[...])
    xs1_ref[...] = x1 + modrow(5) * shared_out

    scores = jax.nn.sigmoid(_dot(h2b, wr_ref[...]))
    lane = lax.broadcasted_iota(I32, (t_rows, N_EXPERTS), 1).astype(F32)
    work = scores + rb_ref[...]
    onehot = jnp.zeros((t_rows, N_EXPERTS), F32)
    idxs, sels = [], []
    for _ in range(TOP_K):
        top = jnp.max(work, axis=-1, keepdims=True)
        idx = jnp.min(jnp.where(work == top, lane, float(N_EXPERTS)), axis=-1, keepdims=True)
        pick = lane == idx
        sels.append(jnp.sum(jnp.where(pick, scores, 0.0), axis=-1, keepdims=True))
        idxs.append(idx)
        work = jnp.where(pick, -jnp.inf, work)
        onehot = jnp.where(pick, 1.0, onehot)
    total = sels[0]
    for k in range(1, TOP_K):
        total = total + sels[k]
    ranktab = _dot(ltri_ref[...], onehot.astype(BF16)) + cnt_scr[0:1, :]
    slab_lane = lax.broadcasted_iota(I32, (t_rows, SLAB), 1)
    slab = jnp.zeros((t_rows, SLAB), I32)
    wslab = jnp.zeros((t_rows, SLAB), F32)
    for k in range(TOP_K):
        rank = jnp.sum(jnp.where(lane == idxs[k], ranktab, 0.0), axis=-1, keepdims=True)
        slab = jnp.where(slab_lane == k, idxs[k].astype(I32), slab)
        slab = jnp.where(slab_lane == TOP_K + k, rank.astype(I32), slab)
        wslab = jnp.where(slab_lane == k, ROUTE_SCALE * sels[k] / total, wslab)
    slab_ref[...] = slab
    wslab_ref[...] = wslab
    cnt_scr[...] = cnt_scr[...] + jnp.sum(onehot, axis=0, keepdims=True)
    cnt_ref[...] = cnt_scr[...]


def _mix_body(n_prompt_tiles, tiles_per_seq, seqs_s, ln_s,
              xp_ref, xsm_ref, modp_ref, mods_ref, h0s_ref, wsp_ref, bsfp_ref, wss_ref, bsfs_ref,
              *rest):
    shared = rest[:19]
    xs1_ref, h2t_ref, slab_ref, wslab_ref, cnt_ref, hfp_ref, hfs_ref, v_ref = rest[19:27]
    scratch = rest[27:]
    outs = (xs1_ref, h2t_ref, slab_ref, wslab_ref, cnt_ref)
    s = pl.program_id(0)

    @pl.when(s == 0)
    def _():
        scratch[2][...] = jnp.zeros_like(scratch[2])

    @pl.when(s < n_prompt_tiles)
    def _():
        _mix_tile(1, SGU_LEN, True, lax.rem(s, tiles_per_seq) == 0, xp_ref, modp_ref, None,
                  wsp_ref, bsfp_ref, shared, outs, hfp_ref, None, scratch)

    @pl.when(s >= n_prompt_tiles)
    def _():
        _mix_tile(seqs_s, ln_s, False, None, xsm_ref, mods_ref, h0s_ref,
                  wss_ref, bsfs_ref, shared, outs, hfs_ref, v_ref, scratch)


def _mix(x_prompt, x_sample, mod, h0_s, ws_p, bsf_p, ws_s, bsf_s, shared):
    bp, sp, _ = x_prompt.shape
    bs, ss, _ = x_sample.shape
    assert sp % MIX_ROWS == 0 and MIX_ROWS % ss == 0 and ss == CHUNK
    seqs_s = MIX_ROWS // ss
    assert bs % seqs_s == 0 and bp % seqs_s == 0
    tiles_per_seq = sp // MIX_ROWS
    n_pt = bp * tiles_per_seq
    n_st = bs // seqs_s
    n_tok = bp * sp + bs * ss

    def p_tile(s):
        return jnp.minimum(s, n_pt - 1)

    def s_tile(s):
        return jnp.maximum(s - n_pt, 0)

    def const(shape):
        nd = len(shape)
        return pl.BlockSpec(shape, lambda s: (0,) * nd)

    in_specs = [
        pl.BlockSpec((1, MIX_ROWS, D_MODEL), lambda s: (p_tile(s) // tiles_per_seq, p_tile(s) % tiles_per_seq, 0)),
        pl.BlockSpec((seqs_s, ss, D_MODEL), lambda s: (s_tile(s), 0, 0)),
        pl.BlockSpec((1, 6, D_MODEL), lambda s: (p_tile(s) // tiles_per_seq, 0, 0)),
        pl.BlockSpec((seqs_s, 6, D_MODEL), lambda s: (bp // seqs_s + s_tile(s), 0, 0)),
        pl.BlockSpec((seqs_s, SUB, STATE_COLS), lambda s: (s_tile(s), 0, 0)),
        const(ws_p.shape), const(bsf_p.shape), const(ws_s.shape), const(bsf_s.shape),
    ] + [const(w.shape) for w in shared]
    out_shape = [
        jax.ShapeDtypeStruct((n_tok, D_MODEL), F32),
        jax.ShapeDtypeStruct((n_tok * ROW_TILES, LANES), F32),
        jax.ShapeDtypeStruct((n_tok, SLAB), I32),
        jax.ShapeDtypeStruct((n_tok, SLAB), F32),
        jax.ShapeDtypeStruct((SUB, N_EXPERTS), F32),
        jax.ShapeDtypeStruct((bp, SUB, STATE_COLS), F32),
        jax.ShapeDtypeStruct((bs, SUB, STATE_COLS), F32),
        jax.ShapeDtypeStruct((bs, ss, D_A), F32),
    ]
    out_specs = [
        pl.BlockSpec((MIX_ROWS, D_MODEL), lambda s: (s, 0)),
        pl.BlockSpec((MIX_ROWS * ROW_TILES, LANES), lambda s: (s, 0)),
        pl.BlockSpec((MIX_ROWS, SLAB), lambda s: (s, 0)),
        pl.BlockSpec((MIX_ROWS, SLAB), lambda s: (s, 0)),
        const((SUB, N_EXPERTS)),
        pl.BlockSpec((1, SUB, STATE_COLS), lambda s: (p_tile(s) // tiles_per_seq, 0, 0)),
        pl.BlockSpec((seqs_s, SUB, STATE_COLS), lambda s: (s_tile(s), 0, 0)),
        pl.BlockSpec((seqs_s, ss, D_A), lambda s: (s_tile(s), 0, 0)),
    ]
    return pl.pallas_call(
        functools.partial(_mix_body, n_pt, tiles_per_seq, seqs_s, ss),
        out_shape=out_shape,
        grid=(n_pt + n_st,),
        in_specs=in_specs,
        out_specs=out_specs,
        scratch_shapes=[pltpu.VMEM((STATE_COLS // LANES, MIX_ROWS, LANES), F32),
                        pltpu.VMEM((SUB, STATE_COLS), F32),
                        pltpu.VMEM((SUB, N_EXPERTS), F32)],
        compiler_params=pltpu.CompilerParams(dimension_semantics=("arbitrary",),
                                             vmem_limit_bytes=VMEM_LIMIT),
        name="mix",
    )(x_prompt, x_sample, mod, mod, h0_s, ws_p, bsf_p, ws_s, bsf_s, *shared)


SLOT_TILE = 2048


DIGIT = 256


def _slots_body(slab_ref, digits_ref, dst_ref):
    slab = slab_ref[...]
    digits = digits_ref[...]
    lane = lax.broadcasted_iota(I32, (SLOT_TILE, N_EXPERTS), 1)
    out_lane = lax.broadcasted_iota(I32, (SLOT_TILE, SLAB), 1)
    start = jnp.zeros((SLOT_TILE, SLAB), F32)
    for k in range(TOP_K):
        onehot = jnp.where(lane == slab[:, k:k + 1], 1.0, 0.0).astype(BF16)
        d = _dot(onehot, digits)
        value = d[:, :SLAB] + float(DIGIT) * d[:, SLAB:2 * SLAB] + float(DIGIT * DIGIT) * d[:, 2 * SLAB:]
        start = jnp.where(out_lane == k, value, start)
    ranks = pltpu.roll(slab, SLAB - TOP_K, 1)
    dst_ref[...] = jnp.where(out_lane < TOP_K, start.astype(I32) + ranks, 0)


def _slots(slab, pstart):
    n_tok = slab.shape[0]
    assert n_tok % SLOT_TILE == 0
    parts = [pstart % DIGIT, (pstart // DIGIT) % DIGIT, pstart // (DIGIT * DIGIT)]
    digits = jnp.concatenate([jnp.broadcast_to(p[:, None], (N_EXPERTS, SLAB)) for p in parts], axis=1)
    return pl.pallas_call(
        _slots_body,
        out_shape=jax.ShapeDtypeStruct((n_tok, SLAB), I32),
        grid=(n_tok // SLOT_TILE,),
        in_specs=[pl.BlockSpec((SLOT_TILE, SLAB), lambda i: (i, 0)),
                  pl.BlockSpec((N_EXPERTS, 3 * SLAB), lambda i: (0, 0))],
        out_specs=pl.BlockSpec((SLOT_TILE, SLAB), lambda i: (i, 0)),
        compiler_params=pltpu.CompilerParams(dimension_semantics=("arbitrary",)),
        name="moe_slots",
    )(slab, digits.astype(BF16))


def _token_rows(t):
    return pl.ds(pl.multiple_of(t * ROW_TILES, ROW_TILES), ROW_TILES)


def _dispatch_body(dst_ref, pstart_ref, pend_ref, h2t_hbm, xs_hbm, tile_ref, zero_ref, in_sems, row_sems,
                   zsem):
    i = pl.program_id(0)
    n_tiles = pl.num_programs(0)
    blk_rows = MOE_BLK * ROW_TILES
    tile_rows = TOK_TILE * ROW_TILES
    n_blocks = xs_hbm.shape[0] // blk_rows

    def fetch(tile):
        slot = lax.rem(tile, DISPATCH_BUFS)
        src = h2t_hbm.at[pl.ds(pl.multiple_of(tile * tile_rows, tile_rows), tile_rows)]
        return pltpu.make_async_copy(src, tile_ref.at[slot], in_sems.at[slot])

    def drain(tile):
        for _ in range(TOP_K):
            pltpu.make_async_copy(tile_ref.at[0], xs_hbm.at[pl.ds(0, tile_rows)],
                                  row_sems.at[lax.rem(tile, 2)]).wait()

    @pl.when(i == 0)
    def _():
        fetch(0).start()

    @pl.when(i + 1 < n_tiles)
    def _():
        fetch(i + 1).start()

    @pl.when(i == 0)
    def _():
        zero_ref[...] = jnp.zeros_like(zero_ref)

        def clear(blk):
            dst = xs_hbm.at[pl.ds(pl.multiple_of(blk * blk_rows, blk_rows), blk_rows)]
            return pltpu.make_async_copy(zero_ref, dst, zsem)

        def each(fn):
            def fill(e, carry):
                @pl.when(pend_ref[e] > pstart_ref[e])
                def _():
                    fn(clear(pend_ref[e] // MOE_BLK - 1))
                return carry
            lax.fori_loop(0, N_EXPERTS, fill, 0)

            def tail(b, carry):
                fn(clear(b))
                return carry
            lax.fori_loop(pend_ref[N_EXPERTS - 1] // MOE_BLK, n_blocks, tail, 0)

        each(lambda cp: cp.start())
        each(lambda cp: cp.wait())

    fetch(i).wait()
    src_tile = tile_ref.at[lax.rem(i, DISPATCH_BUFS)]
    row_sem = row_sems.at[lax.rem(i, 2)]

    def issue(t, carry):
        for k in range(TOP_K):
            dst = dst_ref[0, 0, t * TOP_K + k]
            pltpu.make_async_copy(src_tile.at[_token_rows(t)], xs_hbm.at[_token_rows(dst)],
                                  row_sem).start(priority=k % 2)
        return carry
    lax.fori_loop(0, TOK_TILE, issue, 0)

    @pl.when(i > 0)
    def _():
        drain(i - 1)

    @pl.when(i == n_tiles - 1)
    def _():
        drain(i)


def _dispatch(dst, pstart, pend, h2t, n_rows):
    n_tiles = h2t.shape[0] // (TOK_TILE * ROW_TILES)
    smem_tile = pl.BlockSpec((1, 1, TOK_TILE * TOP_K), lambda i: (i, 0, 0), memory_space=pltpu.SMEM)
    smem_all = pl.BlockSpec(memory_space=pltpu.SMEM)
    return pl.pallas_call(
        _dispatch_body,
        out_shape=jax.ShapeDtypeStruct((n_rows * ROW_TILES, LANES), F32),
        grid=(n_tiles,),
        in_specs=[smem_tile, smem_all, smem_all, pl.BlockSpec(memory_space=pl.ANY)],
        out_specs=pl.BlockSpec(memory_space=pl.ANY),
        scratch_shapes=[pltpu.VMEM((DISPATCH_BUFS, TOK_TILE * ROW_TILES, LANES), F32),
                        pltpu.VMEM((MOE_BLK * ROW_TILES, LANES), F32),
                        pltpu.SemaphoreType.DMA((DISPATCH_BUFS,)), pltpu.SemaphoreType.DMA((2,)),
                        pltpu.SemaphoreType.DMA],
        compiler_params=pltpu.CompilerParams(dimension_semantics=("arbitrary",)),
        name="moe_dispatch",
    )(dst.reshape(n_tiles, 1, TOK_TILE * TOP_K), pstart, pend, h2t)


def _experts_body(pstart_ref, pend_ref, xs_hbm, wg_ref, wu_ref, wd_ref, ys_hbm,
                  xbuf, obuf, wgu_scr, wd_scr, in_sems, out_sems):
    e = pl.program_id(0)
    blk_rows = MOE_BLK * ROW_TILES
    n_blocks = xs_hbm.shape[0] // blk_rows
    n_used = pend_ref[N_EXPERTS - 1] // MOE_BLK
    first_blk = pstart_ref[e] // MOE_BLK
    last_blk = pend_ref[e] // MOE_BLK

    def block(ref, blk):
        return ref.at[pl.ds(pl.multiple_of(blk * blk_rows, blk_rows), blk_rows)]

    def fetch(blk):
        slot = lax.rem(blk, IN_BUFS)
        return pltpu.make_async_copy(block(xs_hbm, blk), xbuf.at[slot], in_sems.at[slot])

    def put(blk):
        slot = lax.rem(blk, OUT_BUFS)
        return pltpu.make_async_copy(obuf.at[slot], block(ys_hbm, blk), out_sems.at[slot])

    def free_out(blk):
        @pl.when(blk >= OUT_BUFS)
        def _():
            put(blk - OUT_BUFS).wait()

    @pl.when(e == 0)
    def _():
        for first in range(IN_BUFS - 1):
            @pl.when(first < n_used)
            def _():
                fetch(first).start()

    @pl.when(last_blk > first_blk)
    def _():
        wgu_scr[:, :D_EXPERT] = wg_ref[0].astype(BF16)
        wgu_scr[:, D_EXPERT:] = wu_ref[0].astype(BF16)
        wd_scr[...] = wd_ref[0].astype(BF16)

    def one_block(b, carry):
        free_out(b)
        ahead = b + IN_BUFS - 1

        @pl.when(ahead < n_used)
        def _():
            fetch(ahead).start(priority=1)
        fetch(b).wait()
        rows_in = xbuf.at[lax.rem(b, IN_BUFS)]
        out = obuf.at[lax.rem(b, OUT_BUFS)]
        xb = jnp.concatenate([rows_in[_tile_rows(j, MOE_BLK), :] for j in range(ROW_TILES)],
                             axis=1).astype(BF16)
        gu = _dot(xb, wgu_scr[...])
        act = (jax.nn.silu(gu[:, :D_EXPERT]) * gu[:, D_EXPERT:]).astype(BF16)
        y = _dot(act, wd_scr[...])
        for j in range(ROW_TILES):
            out[_tile_rows(j, MOE_BLK), :] = y[:, j * LANES:(j + 1) * LANES]
        put(b).start()
        return carry
    lax.fori_loop(first_blk, last_blk, one_block, 0)

    @pl.when(e == N_EXPERTS - 1)
    def _():
        def tail(b, carry):
            free_out(b)
            obuf[lax.rem(b, OUT_BUFS)] = jnp.zeros((blk_rows, LANES), F32)
            put(b).start()
            return carry
        lax.fori_loop(n_used, n_blocks, tail, 0)
        for back in range(OUT_BUFS):
            put(n_blocks - 1 - back).wait()


def _experts(pstart, pend, xs, w_gate, w_up, w_down):
    blk_rows = MOE_BLK * ROW_TILES
    n_blocks = xs.shape[0] // blk_rows
    assert n_blocks >= OUT_BUFS

    def w_map(e, pstart_ref, pend_ref):
        return (e, 0, 0)

    return pl.pallas_call(
        _experts_body,
        out_shape=jax.ShapeDtypeStruct(xs.shape, F32),
        grid_spec=pltpu.PrefetchScalarGridSpec(
            num_scalar_prefetch=2,
            grid=(N_EXPERTS,),
            in_specs=[pl.BlockSpec(memory_space=pl.ANY),
                      pl.BlockSpec((1, D_MODEL, D_EXPERT), w_map),
                      pl.BlockSpec((1, D_MODEL, D_EXPERT), w_map),
                      pl.BlockSpec((1, D_EXPERT, D_MODEL), w_map)],
            out_specs=pl.BlockSpec(memory_space=pl.ANY),
            scratch_shapes=[pltpu.VMEM((IN_BUFS, blk_rows, LANES), F32),
                            pltpu.VMEM((OUT_BUFS, blk_rows, LANES), F32),
                            pltpu.VMEM((D_MODEL, 2 * D_EXPERT), BF16),
                            pltpu.VMEM((D_EXPERT, D_MODEL), BF16),
                            pltpu.SemaphoreType.DMA((IN_BUFS,)),
                            pltpu.SemaphoreType.DMA((OUT_BUFS,))]),
        compiler_params=pltpu.CompilerParams(dimension_semantics=("arbitrary",),
                                             vmem_limit_bytes=VMEM_LIMIT),
        name="moe_experts",
    )(pstart, pend, xs, w_gate, w_up, w_down)


def _combine_body(n_prompt_tiles, n_tiles, dst_ref, dnext_ref, xs1_ref, g2_ref, wts_ref, fg_ref,
                  ys_hbm, yp_ref, ysm_ref, buf_ref, wb_ref, sems):
    i = pl.program_id(0)

    def issue(d_ref, slot):
        def body(t, carry):
            for k in range(TOP_K):
                src = d_ref[0, 0, t * TOP_K + k]
                pltpu.make_async_copy(ys_hbm.at[_token_rows(src)], buf_ref.at[slot, k, _token_rows(t)],
                                      sems.at[slot]).start(priority=k % 2)
            return carry
        lax.fori_loop(0, TOK_TILE, body, 0)

    def drain(slot):
        for k in range(TOP_K):
            pltpu.make_async_copy(ys_hbm.at[pl.ds(0, TOK_TILE * ROW_TILES)], buf_ref.at[slot, k],
                                  sems.at[slot]).wait()

    @pl.when(i == 0)
    def _():
        issue(dst_ref, 0)

    def step(slot):
        @pl.when(i + 1 < n_tiles)
        def _():
            issue(dnext_ref, 1 - slot)
        wts = wts_ref[...]
        for k in range(TOP_K):
            wb_ref[k] = jnp.broadcast_to(wts[:, k:k + 1], (TOK_TILE, LANES))
        drain(slot)
        parts = []
        for j in range(ROW_TILES):
            acc = wb_ref[0] * buf_ref[slot, 0, _tile_rows(j, TOK_TILE), :]
            for k in range(1, TOP_K):
                acc = acc + wb_ref[k] * buf_ref[slot, k, _tile_rows(j, TOK_TILE), :]
            parts.append(acc)
        seg = TOK_TILE // g2_ref.shape[0]
        g2 = jnp.concatenate(
            [jnp.broadcast_to(g2_ref[s], (seg, D_MODEL)) for s in range(g2_ref.shape[0])], axis=0)
        x = xs1_ref[...] + g2 * jnp.concatenate(parts, axis=1)
        y = (x * _rms(x)) * fg_ref[...]

        @pl.when(i < n_prompt_tiles)
        def _():
            yp_ref[...] = y

        @pl.when(i >= n_prompt_tiles)
        def _():
            ysm_ref[...] = y

    for slot in range(2):
        pl.when(lax.rem(i, 2) == slot)(functools.partial(step, slot))


def _combine(dst, xs1, g2_blocks, wts, final_g, ys, n_p):
    n_tok = xs1.shape[0]
    n_tiles = n_tok // TOK_TILE
    n_pt = n_p // TOK_TILE
    segs = TOK_TILE // CHUNK
    smem_tile = pl.BlockSpec((1, 1, TOK_TILE * TOP_K), lambda i: (i, 0, 0), memory_space=pltpu.SMEM)
    smem_next = pl.BlockSpec((1, 1, TOK_TILE * TOP_K), lambda i: (jnp.minimum(i + 1, n_tiles - 1), 0, 0),
                             memory_space=pltpu.SMEM)
    dst = dst.reshape(n_tiles, 1, TOK_TILE * TOP_K)
    return pl.pallas_call(
        functools.partial(_combine_body, n_pt, n_tiles),
        out_shape=[jax.ShapeDtypeStruct((n_p, D_MODEL), F32),
                   jax.ShapeDtypeStruct((n_tok - n_p, D_MODEL), F32)],
        grid=(n_tiles,),
        in_specs=[smem_tile, smem_next,
                  pl.BlockSpec((TOK_TILE, D_MODEL), lambda i: (i, 0)),
                  pl.BlockSpec((segs, 1, D_MODEL), lambda i: (i, 0, 0)),
                  pl.BlockSpec((TOK_TILE, TOP_K), lambda i: (i, 0)),
                  pl.BlockSpec((1, D_MODEL), lambda i: (0, 0)),
                  pl.BlockSpec(memory_space=pl.ANY)],
        out_specs=[pl.BlockSpec((TOK_TILE, D_MODEL), lambda i: (jnp.minimum(i, n_pt - 1), 0)),
                   pl.BlockSpec((TOK_TILE, D_MODEL), lambda i: (jnp.maximum(i - n_pt, 0), 0))],
        scratch_shapes=[pltpu.VMEM((2, TOP_K, TOK_TILE * ROW_TILES, LANES), F32),
                        pltpu.VMEM((TOP_K, TOK_TILE, LANES), F32),
                        pltpu.SemaphoreType.DMA((2,))],
        compiler_params=pltpu.CompilerParams(dimension_semantics=("arbitrary",),
                                             vmem_limit_bytes=VMEM_LIMIT),
        name="moe_combine",
    )(dst, dst, xs1, g2_blocks, wts, final_g.reshape(1, D_MODEL), ys)


def kernel(x_prompt, x_sample, c_prompt, c_sample, state_ssm_re, state_ssm_im, norm1_g, norm2_g,
           w_ada, b_ada, w_in, w_s, b_s, g_v, lam_re, lam_im, log_dt, b_re, b_im, c_re, c_im,
           d_skip, w_glu, b_glu, out_g_a, out_g_b, w_out, w_router, router_bias, w_gate, w_up,
           w_down, ws_gate, ws_up, ws_down, final_g):
    assert norm1_g.shape[0] == 1
    bp, sp, _ = x_prompt.shape
    bs, ss, _ = x_sample.shape
    n_p, n_s = bp * sp, bs * ss
    n_tok = n_p + n_s
    l = 0

    mod = _adaln(jnp.concatenate([c_prompt, c_sample], axis=0), w_ada[l], b_ada[l])
    mod = mod.reshape(bp + bs, 6, D_MODEL)

    wb, wc, tabs = _s5_tables(lam_re[l], lam_im[l], log_dt[l], b_re[l], b_im[l], c_re[l], c_im[l])
    pos = jnp.arange(SGU_LEN)
    mask = (pos[:, None] // CHUNK) >= (pos[None, :] // CHUNK)
    ws_masked = jnp.where(mask[None], w_s[l], 0.0)
    row = lax.broadcasted_iota(I32, (MIX_ROWS, MIX_ROWS), 0)
    col = lax.broadcasted_iota(I32, (MIX_ROWS, MIX_ROWS), 1)
    ltri = (col < row).astype(BF16)

    def sgu_weights(ln):
        bsf = jnp.repeat(b_s[l][:, :ln].T, HEAD_A, axis=1)
        return ws_masked[:, :ln, :ln].astype(BF16), bsf

    ws_p, bsf_p = sgu_weights(SGU_LEN)
    ws_s, bsf_s = sgu_weights(ss)
    shared = [norm1_g[l].reshape(1, D_MODEL), w_in[l].astype(BF16), g_v[l].reshape(1, D_A),
              wb, wc, tabs, d_skip[l].reshape(1, D_B), w_glu[l].astype(BF16),
              b_glu[l].reshape(1, D_B), out_g_a[l].reshape(1, D_A), out_g_b[l].reshape(1, D_B),
              w_out[l].astype(BF16), norm2_g[l].reshape(1, D_MODEL), w_router[l].astype(BF16),
              router_bias[l].reshape(1, N_EXPERTS), ws_gate[l].astype(BF16),
              ws_up[l].astype(BF16), ws_down[l].astype(BF16), ltri]
    h0_s = jnp.broadcast_to(_lay(state_ssm_re[l], state_ssm_im[l])[:, None, :], (bs, SUB, STATE_COLS))
    xs1, h2t, slab, wslab, cnt_all, hfin_p, hfin_s, v_rows = _mix(
        x_prompt, x_sample, mod, h0_s, ws_p, bsf_p, ws_s, bsf_s, shared)

    wts = wslab[:, :TOP_K]
    counts = cnt_all[0].astype(I32)
    padded = (counts + MOE_BLK - 1) // MOE_BLK * MOE_BLK
    pend = jnp.cumsum(padded).astype(I32)
    pstart = pend - padded
    n_blocks = -(-n_tok * TOP_K // MOE_BLK) + N_EXPERTS

    dst = _slots(slab, pstart)[:, :TOP_K]
    xs = _dispatch(dst, pstart, pend, h2t, n_blocks * MOE_BLK)
    ys = _experts(pstart, pend, xs, w_gate[l], w_up[l], w_down[l])

    g2 = mod[:, 5, :]
    g2_blocks = jnp.concatenate([jnp.repeat(g2[:bp], sp // CHUNK, axis=0),
                                 jnp.repeat(g2[bp:], ss // CHUNK, axis=0)], axis=0)
    y_p, y_s = _combine(dst, xs1, g2_blocks.reshape(n_tok // CHUNK, 1, D_MODEL), wts, final_g, ys, n_p)

    re_p, im_p = _unlay(hfin_p[:, 0, :])
    re_s, im_s = _unlay(hfin_s[:, 0, :])
    return (y_p.reshape(bp, sp, D_MODEL), y_s.reshape(bs, ss, D_MODEL),
            re_p[None], im_p[None], re_s[None], im_s[None], v_rows[None])
```

```python
import functools

import jax
import jax.numpy as jnp
from jax import lax
from jax.experimental import pallas as pl
from jax.experimental.pallas import tpu as pltpu

F32 = jnp.float32
BF16 = jnp.bfloat16
I32 = jnp.int32

D_MODEL = 1024
D_A = 512
D_B = 512
N_HEADS_A = 4
HEAD_A = 128
SSM_GROUP = 16
N_GROUPS_B = 32
SSM_STATE = 64
N_EXPERTS = 256
TOP_K = 8
D_EXPERT = 256
ROUTE_SCALE = 2.5
CHUNK = 64
SGU_LEN = 128
EPS = 1e-6

STATE_CHUNKS = 4
GROUPS_PER_CHUNK = N_GROUPS_B // STATE_CHUNKS
HALF = GROUPS_PER_CHUNK * SSM_STATE
STATE_COLS = STATE_CHUNKS * 2 * HALF
SCAN_W = 256
SUB = 8
LANES = 128
ROW_TILES = D_MODEL // LANES
STAGE_STRIDE = ROW_TILES + 1

TAB_PW = 0
TAB_MD = 8
TAB_P8 = 11

MIX_ROWS = 256
MOE_BLK = 256
DISPATCH_BUFS = 3
IN_BUFS = 6
OUT_BUFS = 3
TOK_TILE = 256
SLAB = 128
VMEM_LIMIT = 56 * 1024 * 1024


def _dot(a, b):
    return jnp.dot(a, b, preferred_element_type=F32)


def _rms(x):
    return lax.rsqrt(jnp.mean(x * x, axis=-1, keepdims=True) + EPS)


def _tile_rows(j, n):
    return pl.ds(j, n, stride=ROW_TILES)


def _adaln_body(c_ref, w_ref, b_ref, o_ref):
    c = c_ref[...]
    o_ref[...] = _dot(jax.nn.silu(c).astype(BF16), w_ref[...].astype(BF16)) + b_ref[...]


def _adaln(c, w_ada, b_ada):
    n = c.shape[0]
    cols = w_ada.shape[1]
    blk = 1536
    return pl.pallas_call(
        _adaln_body,
        out_shape=jax.ShapeDtypeStruct((n, cols), F32),
        grid=(cols // blk,),
        in_specs=[pl.BlockSpec((n, D_MODEL), lambda j: (0, 0)),
                  pl.BlockSpec((D_MODEL, blk), lambda j: (0, j)),
                  pl.BlockSpec((1, blk), lambda j: (0, j))],
        out_specs=pl.BlockSpec((n, blk), lambda j: (0, j)),
        compiler_params=pltpu.CompilerParams(dimension_semantics=("arbitrary",)),
        name="adaln",
    )(c, w_ada, b_ada.reshape(1, cols))


def _lay(re, im):
    lead = re.shape[:-2]
    re = re.reshape(lead + (STATE_CHUNKS, HALF))
    im = im.reshape(lead + (STATE_CHUNKS, HALF))
    return jnp.concatenate([re, im], axis=-1).reshape(lead + (STATE_COLS,))


def _unlay(v):
    lead = v.shape[:-1]
    v = v.reshape(lead + (STATE_CHUNKS, 2, HALF))
    re = v[..., 0, :].reshape(lead + (N_GROUPS_B, SSM_STATE))
    im = v[..., 1, :].reshape(lead + (N_GROUPS_B, SSM_STATE))
    return re, im


def _s5_tables(lam_re, lam_im, log_dt, b_re, b_im, c_re, c_im):
    dt = jnp.exp(log_dt.astype(F32))[:, None]
    lr, li = lam_re.astype(F32), lam_im.astype(F32)

    def apow(k):
        mag = jnp.exp(lr * dt * k)
        return mag * jnp.cos(li * dt * k), mag * jnp.sin(li * dt * k)

    ar, ai = apow(1.0)
    den = lr * lr + li * li
    nr, ni = ar - 1.0, ai
    kr, ki = (nr * lr + ni * li) / den, (ni * lr - nr * li) / den
    br, bi = b_re.astype(F32), b_im.astype(F32)
    bbr = kr[..., None] * br - ki[..., None] * bi
    bbi = kr[..., None] * bi + ki[..., None] * br
    eye = jnp.eye(GROUPS_PER_CHUNK, dtype=F32)

    def bproj(bb):
        bb = bb.reshape(STATE_CHUNKS, GROUPS_PER_CHUNK, SSM_STATE, SSM_GROUP)
        w = jnp.einsum("mgph,gk->mghkp", bb, eye)
        return w.reshape(STATE_CHUNKS, GROUPS_PER_CHUNK * SSM_GROUP, HALF)

    wb = jnp.concatenate([bproj(bbr), bproj(bbi)], axis=-1).astype(BF16)

    def cproj(cc):
        cc = cc.reshape(STATE_CHUNKS, GROUPS_PER_CHUNK, SSM_GROUP, SSM_STATE)
        w = jnp.einsum("mghp,gk->mgpkh", cc, eye)
        return w.reshape(STATE_CHUNKS, HALF, GROUPS_PER_CHUNK * SSM_GROUP)

    wc = jnp.concatenate([cproj(c_re.astype(F32)), cproj(-c_im.astype(F32))], axis=1).astype(BF16)

    rows = jnp.arange(SUB, dtype=F32)
    tabs = []
    for i in range(SUB):
        pr, pi = apow(float(i + 1))
        tabs.append(jnp.broadcast_to(_lay(pr, pi)[None], (SUB, STATE_COLS)))
    for d in (1, 2, 4):
        pr, pi = apow(float(SUB * d))
        keep = (rows >= d).astype(F32)[:, None]
        tabs.append(_lay(pr, pi)[None] * keep)
    pr, pi = apow(SUB * rows[:, None, None])
    tabs.append(_lay(pr, pi))
    return wb, wc, jnp.stack(tabs)


def _cmul(ar, ai, br, bi):
    return ar * br - ai * bi, ar * bi + ai * br


def _s5_scan_block(bu_ref, row0, h_ref, tab_ref):
    row_id = lax.broadcasted_iota(I32, (SUB, SCAN_W), 0)
    tiles = SCAN_W // LANES
    for m in range(STATE_CHUNKS):
        for hf in range(HALF // SCAN_W):
            c_re0 = m * 2 * HALF + hf * SCAN_W
            c_im0 = c_re0 + HALF
            cre = pl.ds(c_re0, SCAN_W)
            cim = pl.ds(c_im0, SCAN_W)

            def tab(slot):
                return tab_ref[slot, :, cre], tab_ref[slot, :, cim]

            def load(i, c0):
                rows = pl.ds(row0 + i, SUB, stride=SUB)
                return jnp.concatenate([bu_ref[c0 // LANES + j, rows, :] for j in range(tiles)], axis=1)

            def store(i, c0, val):
                rows = pl.ds(row0 + i, SUB, stride=SUB)
                for j in range(tiles):
                    bu_ref[c0 // LANES + j, rows, :] = val[:, j * LANES:(j + 1) * LANES]

            a_re, a_im = tab(TAB_PW)
            s_re = load(0, c_re0)
            s_im = load(0, c_im0)
            loc = [(s_re, s_im)]
            for i in range(1, SUB):
                p_re, p_im = _cmul(a_re, a_im, s_re, s_im)
                s_re = p_re + load(i, c_re0)
                s_im = p_im + load(i, c_im0)
                loc.append((s_re, s_im))
            e_re, e_im = s_re, s_im
            for n, d in enumerate((1, 2, 4)):
                m_re, m_im = tab(TAB_MD + n)
                q_re, q_im = _cmul(m_re, m_im, pltpu.roll(e_re, d, 0), pltpu.roll(e_im, d, 0))
                e_re, e_im = e_re + q_re, e_im + q_im
            p8_re, p8_im = tab(TAB_P8)
            c_re, c_im = _cmul(p8_re, p8_im, h_ref[:, cre], h_ref[:, cim])
            c_re = c_re + jnp.where(row_id >= 1, pltpu.roll(e_re, 1, 0), 0.0)
            c_im = c_im + jnp.where(row_id >= 1, pltpu.roll(e_im, 1, 0), 0.0)
            for i in range(SUB):
                w_re, w_im = tab(TAB_PW + i)
                q_re, q_im = _cmul(w_re, w_im, c_re, c_im)
                f_re, f_im = loc[i][0] + q_re, loc[i][1] + q_im
                store(i, c_re0, f_re)
                store(i, c_im0, f_im)
            h_ref[:, cre] = jnp.broadcast_to(f_re[SUB - 1:SUB, :], (SUB, SCAN_W))
            h_ref[:, cim] = jnp.broadcast_to(f_im[SUB - 1:SUB, :], (SUB, SCAN_W))


def _mix_tile(seqs, ln, chain, new_seq, x_ref, mod_ref, h0_ref, ws_ref, bsf_ref, shared, outs,
              hfin_ref, v_ref, scratch):
    (n1g_ref, win_ref, gv_ref, wb_ref, wc_ref, tab_ref, dsk_ref, wglu_ref, bglu_ref, oga_ref,
     ogb_ref, wout_ref, n2g_ref, wr_ref, rb_ref, wsg_ref, wsu_ref, wsd_ref, ltri_ref) = shared
    xs1_ref, h2t_ref, slab_ref, wslab_ref, cnt_ref = outs
    bu_ref, h_scr, cnt_scr = scratch
    t_rows = MIX_ROWS
    rows_per_seq = t_rows // seqs

    x = x_ref[...].reshape(t_rows, D_MODEL)

    def modrow(j):
        parts = [jnp.broadcast_to(mod_ref[s, j:j + 1, :], (rows_per_seq, D_MODEL)) for s in range(seqs)]
        return parts[0] if seqs == 1 else jnp.concatenate(parts, axis=0)

    h = (x * _rms(x)) * n1g_ref[...] * (1.0 + modrow(1)) + modrow(0)
    z = _dot(h.astype(BF16), win_ref[...])
    u_a = z[:, :D_A]
    v_a = z[:, D_A:2 * D_A]
    u_b = z[:, 2 * D_A:]

    vh_parts = []
    for hh in range(N_HEADS_A):
        cols = slice(hh * HEAD_A, (hh + 1) * HEAD_A)
        vv = v_a[:, cols]
        vh_parts.append((vv * _rms(vv)) * gv_ref[:, cols])
    if v_ref is not None:
        v_ref[...] = jnp.concatenate(vh_parts, axis=1).reshape(seqs, rows_per_seq, D_A)
    n_sgu = t_rows // ln
    s_parts = []
    for hh in range(N_HEADS_A):
        vb = vh_parts[hh].astype(BF16)
        w_h = ws_ref[hh]
        s_parts.append(jnp.concatenate(
            [_dot(w_h, vb[c * ln:(c + 1) * ln, :]) for c in range(n_sgu)], axis=0))
    bsf = bsf_ref[...]
    s_mix = jnp.concatenate(s_parts, axis=1) + jnp.concatenate([bsf] * n_sgu, axis=0)
    y_a = u_a * s_mix

    ub16 = u_b.astype(BF16)
    gc = GROUPS_PER_CHUNK * SSM_GROUP
    tiles_per_chunk = 2 * HALF // LANES
    for m in range(STATE_CHUNKS):
        bu = _dot(ub16[:, m * gc:(m + 1) * gc], wb_ref[m])
        for j in range(tiles_per_chunk):
            bu_ref[m * tiles_per_chunk + j] = bu[:, j * LANES:(j + 1) * LANES]
    n_blk = t_rows // CHUNK
    if chain:
        @pl.when(new_seq)
        def _():
            h_scr[...] = jnp.zeros_like(h_scr)
    for blk in range(n_blk):
        if not chain:
            h_scr[...] = h0_ref[blk]
        _s5_scan_block(bu_ref, blk * CHUNK, h_scr, tab_ref)
        if not chain:
            hfin_ref[blk] = h_scr[...]
    if chain:
        hfin_ref[0] = h_scr[...]
    y_parts = []
    for m in range(STATE_CHUNKS):
        st = jnp.concatenate([bu_ref[m * tiles_per_chunk + j] for j in range(tiles_per_chunk)], axis=1)
        y_parts.append(_dot(st.astype(BF16), wc_ref[m]))
    y_s = jnp.concatenate(y_parts, axis=1) + dsk_ref[...] * u_b
    g_b = jax.nn.gelu(y_s)
    y_b = g_b * jax.nn.sigmoid(_dot(g_b.astype(BF16), wglu_ref[...]) + bglu_ref[...])

    na = (y_a * _rms(y_a)) * oga_ref[...]
    nb = (y_b * _rms(y_b)) * ogb_ref[...]
    mix = _dot(jnp.concatenate([na, nb], axis=1).astype(BF16), wout_ref[...])
    x1 = x + modrow(2) * mix

    h2 = (x1 * _rms(x1)) * n2g_ref[...] * (1.0 + modrow(4)) + modrow(3)
    for j in range(ROW_TILES):
        h2t_ref[_tile_rows(j, t_rows), :] = h2[:, j * LANES:(j + 1) * LANES]
    h2b = h2.astype(BF16)
    act = jax.nn.silu(_dot(h2b, wsg_ref[...])) * _dot(h2b, wsu_ref[...])
    shared_out = _dot(act.astype(BF16), wsd_ref[...])
    xs1_ref[...] = x1 + modrow(5) * shared_out

    scores = jax.nn.sigmoid(_dot(h2b, wr_ref[...]))
    lane = lax.broadcasted_iota(I32, (t_rows, N_EXPERTS), 1).astype(F32)
    work = scores + rb_ref[...]
    onehot = jnp.zeros((t_rows, N_EXPERTS), F32)
    idxs, sels = [], []
    for _ in range(TOP_K):
        top = jnp.max(work, axis=-1, keepdims=True)
        idx = jnp.min(jnp.where(work == top, lane, float(N_EXPERTS)), axis=-1, keepdims=True)
        pick = lane == idx
        sels.append(jnp.sum(jnp.where(pick, scores, 0.0), axis=-1, keepdims=True))
        idxs.append(idx)
        work = jnp.where(pick, -jnp.inf, work)
        onehot = jnp.where(pick, 1.0, onehot)
    total = sels[0]
    for k in range(1, TOP_K):
        total = total + sels[k]
    ranktab = _dot(ltri_ref[...], onehot.astype(BF16)) + cnt_scr[0:1, :]
    slab_lane = lax.broadcasted_iota(I32, (t_rows, SLAB), 1)
    slab = jnp.zeros((t_rows, SLAB), I32)
    wslab = jnp.zeros((t_rows, SLAB), F32)
    for k in range(TOP_K):
        rank = jnp.sum(jnp.where(lane == idxs[k], ranktab, 0.0), axis=-1, keepdims=True)
        slab = jnp.where(slab_lane == k, idxs[k].astype(I32), slab)
        slab = jnp.where(slab_lane == TOP_K + k, rank.astype(I32), slab)
        wslab = jnp.where(slab_lane == k, ROUTE_SCALE * sels[k] / total, wslab)
    slab_ref[...] = slab
    wslab_ref[...] = wslab
    cnt_scr[...] = cnt_scr[...] + jnp.sum(onehot, axis=0, keepdims=True)
    cnt_ref[...] = cnt_scr[...]


def _mix_body(n_prompt_tiles, tiles_per_seq, seqs_s, ln_s,
              xp_ref, xsm_ref, modp_ref, mods_ref, h0s_ref, wsp_ref, bsfp_ref, wss_ref, bsfs_ref,
              *rest):
    shared = rest[:19]
    xs1_ref, h2t_ref, slab_ref, wslab_ref, cnt_ref, hfp_ref, hfs_ref, v_ref = rest[19:27]
    scratch = rest[27:]
    outs = (xs1_ref, h2t_ref, slab_ref, wslab_ref, cnt_ref)
    s = pl.program_id(0)

    @pl.when(s == 0)
    def _():
        scratch[2][...] = jnp.zeros_like(scratch[2])

    @pl.when(s < n_prompt_tiles)
    def _():
        _mix_tile(1, SGU_LEN, True, lax.rem(s, tiles_per_seq) == 0, xp_ref, modp_ref, None,
                  wsp_ref, bsfp_ref, shared, outs, hfp_ref, None, scratch)

    @pl.when(s >= n_prompt_tiles)
    def _():
        _mix_tile(seqs_s, ln_s, False, None, xsm_ref, mods_ref, h0s_ref,
                  wss_ref, bsfs_ref, shared, outs, hfs_ref, v_ref, scratch)


def _mix(x_prompt, x_sample, mod, h0_s, ws_p, bsf_p, ws_s, bsf_s, shared):
    bp, sp, _ = x_prompt.shape
    bs, ss, _ = x_sample.shape
    assert sp % MIX_ROWS == 0 and MIX_ROWS % ss == 0 and ss == CHUNK
    seqs_s = MIX_ROWS // ss
    assert bs % seqs_s == 0 and bp % seqs_s == 0
    tiles_per_seq = sp // MIX_ROWS
    n_pt = bp * tiles_per_seq
    n_st = bs // seqs_s
    n_tok = bp * sp + bs * ss

    def p_tile(s):
        return jnp.minimum(s, n_pt - 1)

    def s_tile(s):
        return jnp.maximum(s - n_pt, 0)

    def const(shape):
        nd = len(shape)
        return pl.BlockSpec(shape, lambda s: (0,) * nd)

    in_specs = [
        pl.BlockSpec((1, MIX_ROWS, D_MODEL), lambda s: (p_tile(s) // tiles_per_seq, p_tile(s) % tiles_per_seq, 0)),
        pl.BlockSpec((seqs_s, ss, D_MODEL), lambda s: (s_tile(s), 0, 0)),
        pl.BlockSpec((1, 6, D_MODEL), lambda s: (p_tile(s) // tiles_per_seq, 0, 0)),
        pl.BlockSpec((seqs_s, 6, D_MODEL), lambda s: (bp // seqs_s + s_tile(s), 0, 0)),
        pl.BlockSpec((seqs_s, SUB, STATE_COLS), lambda s: (s_tile(s), 0, 0)),
        const(ws_p.shape), const(bsf_p.shape), const(ws_s.shape), const(bsf_s.shape),
    ] + [const(w.shape) for w in shared]
    out_shape = [
        jax.ShapeDtypeStruct((n_tok, D_MODEL), F32),
        jax.ShapeDtypeStruct((n_tok * ROW_TILES, LANES), F32),
        jax.ShapeDtypeStruct((n_tok, SLAB), I32),
        jax.ShapeDtypeStruct((n_tok, SLAB), F32),
        jax.ShapeDtypeStruct((SUB, N_EXPERTS), F32),
        jax.ShapeDtypeStruct((bp, SUB, STATE_COLS), F32),
        jax.ShapeDtypeStruct((bs, SUB, STATE_COLS), F32),
        jax.ShapeDtypeStruct((bs, ss, D_A), F32),
    ]
    out_specs = [
        pl.BlockSpec((MIX_ROWS, D_MODEL), lambda s: (s, 0)),
        pl.BlockSpec((MIX_ROWS * ROW_TILES, LANES), lambda s: (s, 0)),
        pl.BlockSpec((MIX_ROWS, SLAB), lambda s: (s, 0)),
        pl.BlockSpec((MIX_ROWS, SLAB), lambda s: (s, 0)),
        const((SUB, N_EXPERTS)),
        pl.BlockSpec((1, SUB, STATE_COLS), lambda s: (p_tile(s) // tiles_per_seq, 0, 0)),
        pl.BlockSpec((seqs_s, SUB, STATE_COLS), lambda s: (s_tile(s), 0, 0)),
        pl.BlockSpec((seqs_s, ss, D_A), lambda s: (s_tile(s), 0, 0)),
    ]
    return pl.pallas_call(
        functools.partial(_mix_body, n_pt, tiles_per_seq, seqs_s, ss),
        out_shape=out_shape,
        grid=(n_pt + n_st,),
        in_specs=in_specs,
        out_specs=out_specs,
        scratch_shapes=[pltpu.VMEM((STATE_COLS // LANES, MIX_ROWS, LANES), F32),
                        pltpu.VMEM((SUB, STATE_COLS), F32),
                        pltpu.VMEM((SUB, N_EXPERTS), F32)],
        compiler_params=pltpu.CompilerParams(dimension_semantics=("arbitrary",),
                                             vmem_limit_bytes=VMEM_LIMIT),
        name="mix",
    )(x_prompt, x_sample, mod, mod, h0_s, ws_p, bsf_p, ws_s, bsf_s, *shared)


SLOT_TILE = 1024


DIGIT = 256


def _slots_body(slab_ref, digits_ref, dst_ref):
    slab = slab_ref[...]
    digits = digits_ref[...]
    lane = lax.broadcasted_iota(I32, (SLOT_TILE, N_EXPERTS), 1)
    out_lane = lax.broadcasted_iota(I32, (SLOT_TILE, SLAB), 1)
    start = jnp.zeros((SLOT_TILE, SLAB), F32)
    for k in range(TOP_K):
        onehot = jnp.where(lane == slab[:, k:k + 1], 1.0, 0.0).astype(BF16)
        d = _dot(onehot, digits)
        value = d[:, :SLAB] + float(DIGIT) * d[:, SLAB:2 * SLAB] + float(DIGIT * DIGIT) * d[:, 2 * SLAB:]
        start = jnp.where(out_lane == k, value, start)
    ranks = pltpu.roll(slab, SLAB - TOP_K, 1)
    dst_ref[...] = jnp.where(out_lane < TOP_K, start.astype(I32) + ranks, 0)


def _slots(slab, pstart):
    n_tok = slab.shape[0]
    assert n_tok % SLOT_TILE == 0
    parts = [pstart % DIGIT, (pstart // DIGIT) % DIGIT, pstart // (DIGIT * DIGIT)]
    digits = jnp.concatenate([jnp.broadcast_to(p[:, None], (N_EXPERTS, SLAB)) for p in parts], axis=1)
    return pl.pallas_call(
        _slots_body,
        out_shape=jax.ShapeDtypeStruct((n_tok, SLAB), I32),
        grid=(n_tok // SLOT_TILE,),
        in_specs=[pl.BlockSpec((SLOT_TILE, SLAB), lambda i: (i, 0)),
                  pl.BlockSpec((N_EXPERTS, 3 * SLAB), lambda i: (0, 0))],
        out_specs=pl.BlockSpec((SLOT_TILE, SLAB), lambda i: (i, 0)),
        compiler_params=pltpu.CompilerParams(dimension_semantics=("arbitrary",)),
        name="moe_slots",
    )(slab, digits.astype(BF16))


def _token_rows(t):
    return pl.ds(pl.multiple_of(t * ROW_TILES, ROW_TILES), ROW_TILES)


def _dispatch_body(dst_ref, pstart_ref, pend_ref, h2t_hbm, xs_hbm, tile_ref, zero_ref, in_sems, row_sems,
                   zsem):
    i = pl.program_id(0)
    n_tiles = pl.num_programs(0)
    blk_rows = MOE_BLK * ROW_TILES
    tile_rows = TOK_TILE * ROW_TILES
    n_blocks = xs_hbm.shape[0] // blk_rows

    def fetch(tile):
        slot = lax.rem(tile, DISPATCH_BUFS)
        src = h2t_hbm.at[pl.ds(pl.multiple_of(tile * tile_rows, tile_rows), tile_rows)]
        return pltpu.make_async_copy(src, tile_ref.at[slot], in_sems.at[slot])

    def drain(tile):
        for _ in range(TOP_K):
            pltpu.make_async_copy(tile_ref.at[0], xs_hbm.at[pl.ds(0, tile_rows)],
                                  row_sems.at[lax.rem(tile, 2)]).wait()

    @pl.when(i == 0)
    def _():
        fetch(0).start()

    @pl.when(i + 1 < n_tiles)
    def _():
        fetch(i + 1).start()

    @pl.when(i == 0)
    def _():
        zero_ref[...] = jnp.zeros_like(zero_ref)

        def clear(blk):
            dst = xs_hbm.at[pl.ds(pl.multiple_of(blk * blk_rows, blk_rows), blk_rows)]
            return pltpu.make_async_copy(zero_ref, dst, zsem)

        def each(fn):
            def fill(e, carry):
                @pl.when(pend_ref[e] > pstart_ref[e])
                def _():
                    fn(clear(pend_ref[e] // MOE_BLK - 1))
                return carry
            lax.fori_loop(0, N_EXPERTS, fill, 0)

            def tail(b, carry):
                fn(clear(b))
                return carry
            lax.fori_loop(pend_ref[N_EXPERTS - 1] // MOE_BLK, n_blocks, tail, 0)

        each(lambda cp: cp.start())
        each(lambda cp: cp.wait())

    fetch(i).wait()
    src_tile = tile_ref.at[lax.rem(i, DISPATCH_BUFS)]
    row_sem = row_sems.at[lax.rem(i, 2)]

    def issue(t, carry):
        for k in range(TOP_K):
            dst = dst_ref[0, 0, t * TOP_K + k]
            pltpu.make_async_copy(src_tile.at[_token_rows(t)], xs_hbm.at[_token_rows(dst)],
                                  row_sem).start(priority=k % 2)
        return carry
    lax.fori_loop(0, TOK_TILE, issue, 0)

    @pl.when(i > 0)
    def _():
        drain(i - 1)

    @pl.when(i == n_tiles - 1)
    def _():
        drain(i)


def _dispatch(dst, pstart, pend, h2t, n_rows):
    n_tiles = h2t.shape[0] // (TOK_TILE * ROW_TILES)
    smem_tile = pl.BlockSpec((1, 1, TOK_TILE * TOP_K), lambda i: (i, 0, 0), memory_space=pltpu.SMEM)
    smem_all = pl.BlockSpec(memory_space=pltpu.SMEM)
    return pl.pallas_call(
        _dispatch_body,
        out_shape=jax.ShapeDtypeStruct((n_rows * ROW_TILES, LANES), F32),
        grid=(n_tiles,),
        in_specs=[smem_tile, smem_all, smem_all, pl.BlockSpec(memory_space=pl.ANY)],
        out_specs=pl.BlockSpec(memory_space=pl.ANY),
        scratch_shapes=[pltpu.VMEM((DISPATCH_BUFS, TOK_TILE * ROW_TILES, LANES), F32),
                        pltpu.VMEM((MOE_BLK * ROW_TILES, LANES), F32),
                        pltpu.SemaphoreType.DMA((DISPATCH_BUFS,)), pltpu.SemaphoreType.DMA((2,)),
                        pltpu.SemaphoreType.DMA],
        compiler_params=pltpu.CompilerParams(dimension_semantics=("arbitrary",)),
        name="moe_dispatch",
    )(dst.reshape(n_tiles, 1, TOK_TILE * TOP_K), pstart, pend, h2t)


def _experts_body(pstart_ref, pend_ref, xs_hbm, wg_ref, wu_ref, wd_ref, ys_hbm,
                  xbuf, obuf, wgu_scr, wd_scr, in_sems, out_sems):
    e = pl.program_id(0)
    blk_rows = MOE_BLK * ROW_TILES
    n_blocks = xs_hbm.shape[0] // blk_rows
    n_used = pend_ref[N_EXPERTS - 1] // MOE_BLK
    first_blk = pstart_ref[e] // MOE_BLK
    last_blk = pend_ref[e] // MOE_BLK

    def block(ref, blk):
        return ref.at[pl.ds(pl.multiple_of(blk * blk_rows, blk_rows), blk_rows)]

    def fetch(blk):
        slot = lax.rem(blk, IN_BUFS)
        return pltpu.make_async_copy(block(xs_hbm, blk), xbuf.at[slot], in_sems.at[slot])

    def put(blk):
        slot = lax.rem(blk, OUT_BUFS)
        return pltpu.make_async_copy(obuf.at[slot], block(ys_hbm, blk), out_sems.at[slot])

    def free_out(blk):
        @pl.when(blk >= OUT_BUFS)
        def _():
            put(blk - OUT_BUFS).wait()

    @pl.when(e == 0)
    def _():
        for first in range(IN_BUFS - 1):
            @pl.when(first < n_used)
            def _():
                fetch(first).start()

    @pl.when(last_blk > first_blk)
    def _():
        wgu_scr[:, :D_EXPERT] = wg_ref[0].astype(BF16)
        wgu_scr[:, D_EXPERT:] = wu_ref[0].astype(BF16)
        wd_scr[...] = wd_ref[0].astype(BF16)

    def one_block(b, carry):
        free_out(b)
        ahead = b + IN_BUFS - 1

        @pl.when(ahead < n_used)
        def _():
            fetch(ahead).start()
        fetch(b).wait()
        rows_in = xbuf.at[lax.rem(b, IN_BUFS)]
        out = obuf.at[lax.rem(b, OUT_BUFS)]
        xb = jnp.concatenate([rows_in[_tile_rows(j, MOE_BLK), :] for j in range(ROW_TILES)],
                             axis=1).astype(BF16)
        gu = _dot(xb, wgu_scr[...])
        act = (jax.nn.silu(gu[:, :D_EXPERT]) * gu[:, D_EXPERT:]).astype(BF16)
        y = _dot(act, wd_scr[...])
        for j in range(ROW_TILES):
            out[_tile_rows(j, MOE_BLK), :] = y[:, j * LANES:(j + 1) * LANES]
        put(b).start()
        return carry
    lax.fori_loop(first_blk, last_blk, one_block, 0)

    @pl.when(e == N_EXPERTS - 1)
    def _():
        def tail(b, carry):
            free_out(b)
            obuf[lax.rem(b, OUT_BUFS)] = jnp.zeros((blk_rows, LANES), F32)
            put(b).start()
            return carry
        lax.fori_loop(n_used, n_blocks, tail, 0)
        for back in range(OUT_BUFS):
            put(n_blocks - 1 - back).wait()


def _experts(pstart, pend, xs, w_gate, w_up, w_down):
    blk_rows = MOE_BLK * ROW_TILES
    n_blocks = xs.shape[0] // blk_rows
    assert n_blocks >= OUT_BUFS

    def w_map(e, pstart_ref, pend_ref):
        return (e, 0, 0)

    return pl.pallas_call(
        _experts_body,
        out_shape=jax.ShapeDtypeStruct(xs.shape, F32),
        grid_spec=pltpu.PrefetchScalarGridSpec(
            num_scalar_prefetch=2,
            grid=(N_EXPERTS,),
            in_specs=[pl.BlockSpec(memory_space=pl.ANY),
                      pl.BlockSpec((1, D_MODEL, D_EXPERT), w_map),
                      pl.BlockSpec((1, D_MODEL, D_EXPERT), w_map),
                      pl.BlockSpec((1, D_EXPERT, D_MODEL), w_map)],
            out_specs=pl.BlockSpec(memory_space=pl.ANY),
            scratch_shapes=[pltpu.VMEM((IN_BUFS, blk_rows, LANES), F32),
                            pltpu.VMEM((OUT_BUFS, blk_rows, LANES), F32),
                            pltpu.VMEM((D_MODEL, 2 * D_EXPERT), BF16),
                            pltpu.VMEM((D_EXPERT, D_MODEL), BF16),
                            pltpu.SemaphoreType.DMA((IN_BUFS,)),
                            pltpu.SemaphoreType.DMA((OUT_BUFS,))]),
        compiler_params=pltpu.CompilerParams(dimension_semantics=("arbitrary",),
                                             vmem_limit_bytes=VMEM_LIMIT),
        name="moe_experts",
    )(pstart, pend, xs, w_gate, w_up, w_down)


def _combine_body(n_prompt_tiles, n_tiles, dst_ref, dnext_ref, xs1_ref, g2_ref, wts_ref, fg_ref,
                  ys_hbm, yp_ref, ysm_ref, buf_ref, wb_ref, sems):
    i = pl.program_id(0)

    def issue(d_ref, slot):
        def body(t, carry):
            for k in range(TOP_K):
                src = d_ref[0, 0, t * TOP_K + k]
                pltpu.make_async_copy(ys_hbm.at[_token_rows(src)],
                                      buf_ref.at[slot, k, pl.ds(t * STAGE_STRIDE, ROW_TILES)],
                                      sems.at[slot]).start(priority=k % 2)
            return carry
        lax.fori_loop(0, TOK_TILE, body, 0)

    def drain(slot):
        for k in range(TOP_K):
            pltpu.make_async_copy(ys_hbm.at[pl.ds(0, TOK_TILE * ROW_TILES)],
                                  buf_ref.at[slot, k, pl.ds(0, TOK_TILE * ROW_TILES)],
                                  sems.at[slot]).wait()

    def staged(slot, k, j):
        return buf_ref[slot, k, pl.ds(j, TOK_TILE, stride=STAGE_STRIDE), :]

    @pl.when(i == 0)
    def _():
        issue(dst_ref, 0)

    def step(slot):
        @pl.when(i + 1 < n_tiles)
        def _():
            issue(dnext_ref, 1 - slot)
        wts = wts_ref[...]
        for k in range(TOP_K):
            wb_ref[k] = jnp.broadcast_to(wts[:, k:k + 1], (TOK_TILE, LANES))
        drain(slot)
        parts = []
        for j in range(ROW_TILES):
            acc = wb_ref[0] * staged(slot, 0, j)
            for k in range(1, TOP_K):
                acc = acc + wb_ref[k] * staged(slot, k, j)
            parts.append(acc)
        seg = TOK_TILE // g2_ref.shape[0]
        g2 = jnp.concatenate(
            [jnp.broadcast_to(g2_ref[s], (seg, D_MODEL)) for s in range(g2_ref.shape[0])], axis=0)
        x = xs1_ref[...] + g2 * jnp.concatenate(parts, axis=1)
        y = (x * _rms(x)) * fg_ref[...]

        @pl.when(i < n_prompt_tiles)
        def _():
            yp_ref[...] = y

        @pl.when(i >= n_prompt_tiles)
        def _():
            ysm_ref[...] = y

    for slot in range(2):
        pl.when(lax.rem(i, 2) == slot)(functools.partial(step, slot))


def _combine(dst, xs1, g2_blocks, wts, final_g, ys, n_p):
    n_tok = xs1.shape[0]
    n_tiles = n_tok // TOK_TILE
    n_pt = n_p // TOK_TILE
    segs = TOK_TILE // CHUNK
    smem_tile = pl.BlockSpec((1, 1, TOK_TILE * TOP_K), lambda i: (i, 0, 0), memory_space=pltpu.SMEM)
    smem_next = pl.BlockSpec((1, 1, TOK_TILE * TOP_K), lambda i: (jnp.minimum(i + 1, n_tiles - 1), 0, 0),
                             memory_space=pltpu.SMEM)
    dst = dst.reshape(n_tiles, 1, TOK_TILE * TOP_K)
    return pl.pallas_call(
        functools.partial(_combine_body, n_pt, n_tiles),
        out_shape=[jax.ShapeDtypeStruct((n_p, D_MODEL), F32),
                   jax.ShapeDtypeStruct((n_tok - n_p, D_MODEL), F32)],
        grid=(n_tiles,),
        in_specs=[smem_tile, smem_next,
                  pl.BlockSpec((TOK_TILE, D_MODEL), lambda i: (i, 0)),
                  pl.BlockSpec((segs, 1, D_MODEL), lambda i: (i, 0, 0)),
                  pl.BlockSpec((TOK_TILE, TOP_K), lambda i: (i, 0)),
                  pl.BlockSpec((1, D_MODEL), lambda i: (0, 0)),
                  pl.BlockSpec(memory_space=pl.ANY)],
        out_specs=[pl.BlockSpec((TOK_TILE, D_MODEL), lambda i: (jnp.minimum(i, n_pt - 1), 0)),
                   pl.BlockSpec((TOK_TILE, D_MODEL), lambda i: (jnp.maximum(i - n_pt, 0), 0))],
        scratch_shapes=[pltpu.VMEM((2, TOP_K, TOK_TILE * STAGE_STRIDE, LANES), F32),
                        pltpu.VMEM((TOP_K, TOK_TILE, LANES), F32),
                        pltpu.SemaphoreType.DMA((2,))],
        compiler_params=pltpu.CompilerParams(dimension_semantics=("arbitrary",),
                                             vmem_limit_bytes=VMEM_LIMIT),
        name="moe_combine",
    )(dst, dst, xs1, g2_blocks, wts, final_g.reshape(1, D_MODEL), ys)


def kernel(x_prompt, x_sample, c_prompt, c_sample, state_ssm_re, state_ssm_im, norm1_g, norm2_g,
           w_ada, b_ada, w_in, w_s, b_s, g_v, lam_re, lam_im, log_dt, b_re, b_im, c_re, c_im,
           d_skip, w_glu, b_glu, out_g_a, out_g_b, w_out, w_router, router_bias, w_gate, w_up,
           w_down, ws_gate, ws_up, ws_down, final_g):
    assert norm1_g.shape[0] == 1
    bp, sp, _ = x_prompt.shape
    bs, ss, _ = x_sample.shape
    n_p, n_s = bp * sp, bs * ss
    n_tok = n_p + n_s
    l = 0

    mod = _adaln(jnp.concatenate([c_prompt, c_sample], axis=0), w_ada[l], b_ada[l])
    mod = mod.reshape(bp + bs, 6, D_MODEL)

    wb, wc, tabs = _s5_tables(lam_re[l], lam_im[l], log_dt[l], b_re[l], b_im[l], c_re[l], c_im[l])
    pos = jnp.arange(SGU_LEN)
    mask = (pos[:, None] // CHUNK) >= (pos[None, :] // CHUNK)
    ws_masked = jnp.where(mask[None], w_s[l], 0.0)
    row = lax.broadcasted_iota(I32, (MIX_ROWS, MIX_ROWS), 0)
    col = lax.broadcasted_iota(I32, (MIX_ROWS, MIX_ROWS), 1)
    ltri = (col < row).astype(BF16)

    def sgu_weights(ln):
        bsf = jnp.repeat(b_s[l][:, :ln].T, HEAD_A, axis=1)
        return ws_masked[:, :ln, :ln].astype(BF16), bsf

    ws_p, bsf_p = sgu_weights(SGU_LEN)
    ws_s, bsf_s = sgu_weights(ss)
    shared = [norm1_g[l].reshape(1, D_MODEL), w_in[l].astype(BF16), g_v[l].reshape(1, D_A),
              wb, wc, tabs, d_skip[l].reshape(1, D_B), w_glu[l].astype(BF16),
              b_glu[l].reshape(1, D_B), out_g_a[l].reshape(1, D_A), out_g_b[l].reshape(1, D_B),
              w_out[l].astype(BF16), norm2_g[l].reshape(1, D_MODEL), w_router[l].astype(BF16),
              router_bias[l].reshape(1, N_EXPERTS), ws_gate[l].astype(BF16),
              ws_up[l].astype(BF16), ws_down[l].astype(BF16), ltri]
    h0_s = jnp.broadcast_to(_lay(state_ssm_re[l], state_ssm_im[l])[:, None, :], (bs, SUB, STATE_COLS))
    xs1, h2t, slab, wslab, cnt_all, hfin_p, hfin_s, v_rows = _mix(
        x_prompt, x_sample, mod, h0_s, ws_p, bsf_p, ws_s, bsf_s, shared)

    wts = wslab[:, :TOP_K]
    counts = cnt_all[0].astype(I32)
    padded = (counts + MOE_BLK - 1) // MOE_BLK * MOE_BLK
    pend = jnp.cumsum(padded).astype(I32)
    pstart = pend - padded
    n_blocks = -(-n_tok * TOP_K // MOE_BLK) + N_EXPERTS

    dst = _slots(slab, pstart)[:, :TOP_K]
    xs = _dispatch(dst, pstart, pend, h2t, n_blocks * MOE_BLK)
    ys = _experts(pstart, pend, xs, w_gate[l], w_up[l], w_down[l])

    g2 = mod[:, 5, :]
    g2_blocks = jnp.concatenate([jnp.repeat(g2[:bp], sp // CHUNK, axis=0),
                                 jnp.repeat(g2[bp:], ss // CHUNK, axis=0)], axis=0)
    y_p, y_s = _combine(dst, xs1, g2_blocks.reshape(n_tok // CHUNK, 1, D_MODEL), wts, final_g, ys, n_p)

    re_p, im_p = _unlay(hfin_p[:, 0, :])
    re_s, im_s = _unlay(hfin_s[:, 0, :])
    return (y_p.reshape(bp, sp, D_MODEL), y_s.reshape(bs, ss, D_MODEL),
            re_p[None], im_p[None], re_s[None], im_s[None], v_rows[None])
```
